```python
import jax
import jax.numpy as jnp
from jax import lax
import numpy as np


D_MODEL = 2048
BATCH = 1
SEQ = 8192
DEPTH = 4

GRID_W = 64
CTX_LEN = 256
EPS = 1e-6

NA_HEADS = 8
NA_HEAD_DIM = 64
NA_WIN_H = 8
NA_WIN_W = 16
GLA_HEADS = 4
GLA_DK = 128
GLA_DV = 256
GLA_GATE_RANK = 16
GLA_GATE_NORM = 16.0
GLA_CHUNK = 64
GQA_HEADS = 8
GQA_KV_HEADS = 2
GQA_HEAD_DIM = 64
GQA_QBLOCK = 128
ROPE_BASE = 10000.0
PEER_HEADS = 8
PEER_NKEYS = 128
PEER_EXPERTS = PEER_NKEYS * PEER_NKEYS
PEER_KEY_DIM = 256
PEER_TOPK = 16
PEER_TOKEN_BLOCK = 128

NA_W = NA_HEADS * NA_HEAD_DIM
GLA_KW = GLA_HEADS * GLA_DK
GLA_VW = GLA_HEADS * GLA_DV
GQA_QW = GQA_HEADS * GQA_HEAD_DIM
GQA_KVW = GQA_KV_HEADS * GQA_HEAD_DIM
MIX_W = NA_W + GLA_VW + GQA_QW
IN_WIDTHS = (NA_W, NA_W, NA_W, GLA_KW, GLA_KW, GLA_VW, GLA_VW, 2 * GLA_GATE_RANK, GQA_QW, GQA_KVW, GQA_KVW)
IN_W = sum(IN_WIDTHS)
IN_OFFSETS = [int(v) for v in np.cumsum(IN_WIDTHS)[:-1]]

DEEPNORM_ALPHA = (2.0 * DEPTH) ** 0.25
DEEPNORM_BETA = (8.0 * DEPTH) ** -0.25

kernel_name = 'hybrid_na_gla_gqa_peer_diffusion_trunk'


def layer_norm(x):
    xf = x.astype(jnp.float32)
    mu = jnp.mean(xf, axis=-1, keepdims=True)
    var = jnp.mean(jnp.square(xf - mu), axis=-1, keepdims=True)
    return ((xf - mu) * lax.rsqrt(var + EPS)).astype(x.dtype)


def post_ln(x, g, b):
    return layer_norm(x) * g + b


def rms_norm(x, w):
    xf = x.astype(jnp.float32)
    y = xf * lax.rsqrt(jnp.mean(xf * xf, axis=-1, keepdims=True) + EPS)
    return (y * w.astype(jnp.float32)).astype(x.dtype)


def axial_rope_tables(n_tokens, head_dim):
    half = head_dim // 2
    inv = ROPE_BASE ** (-jnp.arange(0, half, 2, dtype=jnp.float32) / half)
    t = jnp.arange(n_tokens)
    row = (t // GRID_W).astype(jnp.float32)
    col = (t % GRID_W).astype(jnp.float32)
    ang = jnp.concatenate([row[:, None] * inv, col[:, None] * inv], axis=-1)
    return jnp.cos(ang), jnp.sin(ang)


def apply_axial_rope(x, cos, sin):
    B, T, H, d = x.shape
    xp = x.reshape(B, T, H, d // 2, 2)
    x0, x1 = xp[..., 0], xp[..., 1]
    c = cos[None, :, None, :].astype(x.dtype)
    s = sin[None, :, None, :].astype(x.dtype)
    return jnp.stack([x0 * c - x1 * s, x0 * s + x1 * c], axis=-1).reshape(B, T, H, d)


def mixer_projections(h, w_in, gla_gate_w, gla_gate_b, gqa_q_norm, gqa_k_norm):
    B, T, _ = h.shape
    na_q, na_k, na_v, gq, gk, gv, gg, glr, cq, ck, cv = jnp.split(h @ w_in, IN_OFFSETS, axis=-1)
    heads = lambda a, n: a.reshape(B, T, n, -1)
    to_bhtd = lambda a: heads(a, GLA_HEADS).transpose(0, 2, 1, 3).astype(jnp.float32)
    lr_f, lr_b = jnp.split(glr.astype(jnp.float32), 2, axis=-1)
    log_gate = lambda lr, w, b: jax.nn.log_sigmoid(lr @ w.astype(jnp.float32) + b.astype(jnp.float32)) / GLA_GATE_NORM
    return dict(
        na_q=heads(na_q, NA_HEADS), na_k=heads(na_k, NA_HEADS), na_v=heads(na_v, NA_HEADS),
        gla_q=to_bhtd(gq) * (GLA_DK ** -0.5), gla_k=to_bhtd(gk), gla_v=to_bhtd(gv),
        gla_la_f=to_bhtd(log_gate(lr_f, gla_gate_w[0], gla_gate_b[0])),
        gla_la_b=to_bhtd(log_gate(lr_b, gla_gate_w[1], gla_gate_b[1])),
        gla_gate=gg,
        gqa_q=rms_norm(heads(cq, GQA_HEADS), gqa_q_norm),
        gqa_k=rms_norm(heads(ck, GQA_KV_HEADS), gqa_k_norm),
        gqa_v=heads(cv, GQA_KV_HEADS))


def neighbourhood_attention(q, k, v, k_ctx, v_ctx, rpb, rows):
    B, T, H, d = q.shape
    kh = min(NA_WIN_H, rows)
    kw = NA_WIN_W
    r = jnp.arange(rows)
    row_idx = jnp.clip(r - kh // 2, 0, rows - kh)[:, None] + jnp.arange(kh)
    cq = jnp.arange(GRID_W)
    col_idx = jnp.clip(cq - kw // 2, 0, GRID_W - kw)[:, None] + jnp.arange(kw)
    qg = q.reshape(B, rows, GRID_W, H, d) * (d ** -0.5)
    kg = k.reshape(B, rows, GRID_W, H, d)[:, row_idx]
    vg = v.reshape(B, rows, GRID_W, H, d)[:, row_idx]
    s_band = jnp.einsum('brqhd,briwhd->bhriqw', qg, kg)
    s_win = s_band[..., cq[:, None], col_idx]
    dr = row_idx - r[:, None] + (NA_WIN_H - 1)
    dc = col_idx - cq[:, None] + (NA_WIN_W - 1)
    bias = rpb[:, dr[:, :, None, None], dc[None, None, :, :]]
    s_win = (s_win + bias[None]).transpose(0, 1, 2, 4, 3, 5).reshape(B, H, rows, GRID_W, kh * kw)
    s_ctx = jnp.einsum('brqhd,blhd->bhrql', qg, k_ctx)
    p = jax.nn.softmax(jnp.concatenate([s_win, s_ctx], axis=-1).astype(jnp.float32), axis=-1).astype(v.dtype)
    p_win = p[..., :kh * kw].reshape(B, H, rows, GRID_W, kh, kw).transpose(0, 1, 2, 4, 3, 5)
    p_ctx = p[..., kh * kw:]
    p_band = jnp.zeros((B, H, rows, kh, GRID_W, GRID_W), p.dtype).at[..., cq[:, None], col_idx].set(p_win)
    out = jnp.einsum('bhriqw,briwhd->brqhd', p_band, vg) + jnp.einsum('bhrql,blhd->brqhd', p_ctx, v_ctx)
    return out.reshape(B, T, H * d)


def ctx_attention(q, k, v):
    B, L, H, d = q.shape
    hkv = k.shape[2]
    qg = q.reshape(B, L, hkv, H // hkv, d) * (d ** -0.5)
    s = jnp.einsum('blkgd,bmkd->bkglm', qg, k)
    p = jax.nn.softmax(s.astype(jnp.float32), axis=-1).astype(v.dtype)
    return jnp.einsum('bkglm,bmkd->blkgd', p, v).reshape(B, L, H * d)


def gqa_block_attention(q, k_all, v_all):
    B, T, H, d = q.shape
    hkv = k_all.shape[2]
    nb = T // GQA_QBLOCK
    qb = (q.reshape(B, nb, GQA_QBLOCK, hkv, H // hkv, d) * (d ** -0.5)).transpose(1, 0, 2, 3, 4, 5)

    def one_block(qblk):
        s = jnp.einsum('bqkgd,bmkd->bkgqm', qblk, k_all)
        p = jax.nn.softmax(s.astype(jnp.float32), axis=-1).astype(v_all.dtype)
        return jnp.einsum('bkgqm,bmkd->bqkgd', p, v_all)

    o = lax.map(one_block, qb)
    return o.transpose(1, 0, 2, 3, 4, 5).reshape(B, T, H * d)


def gla_chunked(q, k, v, log_a, s0):
    B, H, T, dk = q.shape
    dv = v.shape[-1]
    n = T // GLA_CHUNK
    chunks = lambda a: a.reshape(B, H, n, GLA_CHUNK, a.shape[-1])
    q, k, v, log_a = chunks(q), chunks(k), chunks(v), chunks(log_a)
    b = jnp.cumsum(log_a, axis=3)
    b_last = b[:, :, :, -1:, :]
    q_t = q * jnp.exp(b)
    k_t = k * jnp.exp(-b)
    k_end = k * jnp.exp(b_last - b)
    mask = jnp.tril(jnp.ones((GLA_CHUNK, GLA_CHUNK), dtype=bool))
    attn = jnp.where(mask, jnp.einsum('bhnid,bhnjd->bhnij', q_t, k_t), 0.0)
    o_intra = jnp.einsum('bhnij,bhnjv->bhniv', attn, v)
    upd = jnp.einsum('bhncd,bhncv->nbhdv', k_end, v)
    decay = jnp.exp(b_last[:, :, :, 0, :]).transpose(2, 0, 1, 3)

    def step(s, inp):
        dec, u = inp
        return dec[..., None] * s + u, s

    s_final, s_prev = lax.scan(step, s0, (decay, upd))
    o_inter = jnp.einsum('bhncd,nbhdv->bhncv', q_t, s_prev)
    return (o_intra + o_inter).reshape(B, H, T, dv), s_final


def gla_bidirectional(q, k, v, la_f, la_b, s0_f, s0_b):
    flip = lambda a: a[:, :, ::-1]
    o_f, s_f = gla_chunked(q, k, v, la_f, s0_f)
    o_b, s_b = gla_chunked(flip(q), flip(k), flip(v), flip(la_b), s0_b)
    return o_f + flip(o_b), s_f, s_b


def gla_output(o, gate, norm_w):
    B, H, T, dv = o.shape
    o = rms_norm(o.transpose(0, 2, 1, 3), norm_w).reshape(B, T, H * dv)
    return (o * jax.nn.silu(gate.astype(jnp.float32))).astype(gate.dtype)


def peer(h, w_q, sub_keys, u_tab, v_tab):
    B, T, D = h.shape
    n_tok = B * T
    hf = h.reshape(n_tok, D)
    q = (hf @ w_q).reshape(n_tok, PEER_HEADS, 2, PEER_KEY_DIM // 2)
    s = jnp.einsum('thpd,hpkd->thpk', q, sub_keys).astype(jnp.float32)
    s1, i1 = lax.top_k(s[:, :, 0], PEER_TOPK)
    s2, i2 = lax.top_k(s[:, :, 1], PEER_TOPK)
    cand = (s1[..., :, None] + s2[..., None, :]).reshape(n_tok, PEER_HEADS, PEER_TOPK * PEER_TOPK)
    cand_idx = (i1[..., :, None] * PEER_NKEYS + i2[..., None, :]).reshape(n_tok, PEER_HEADS, PEER_TOPK * PEER_TOPK)
    top_s, pos = lax.top_k(cand, PEER_TOPK)
    idx = jnp.take_along_axis(cand_idx, pos, axis=-1).reshape(n_tok, PEER_HEADS * PEER_TOPK)
    gate = jax.nn.softmax(top_s, axis=-1).reshape(n_tok, PEER_HEADS * PEER_TOPK).astype(h.dtype)
    nb = n_tok // PEER_TOKEN_BLOCK

    def block(args):
        hb, ib, gb = args
        act = jax.nn.gelu(jnp.einsum('tkd,td->tk', u_tab[ib], hb)) * gb
        return jnp.einsum('tk,tkd->td', act, v_tab[ib])

    out = lax.map(block, (hf.reshape(nb, PEER_TOKEN_BLOCK, D),
                          idx.reshape(nb, PEER_TOKEN_BLOCK, -1),
                          gate.reshape(nb, PEER_TOKEN_BLOCK, -1)))
    return out.reshape(B, T, D)


def trunk_layer(x, xc, mod, modc, w_in, na_rpb, gla_gate_w, gla_gate_b, gla_norm_w, gqa_q_norm, gqa_k_norm,
                w_out, ln1_g, ln1_b, peer_wq, peer_subkeys, peer_u, peer_v, ln2_g, ln2_b,
                rope_cos, rope_sin, rows, update_ctx):
    B = x.shape[0]
    sh1, sc1, g1, sh2, sc2, g2 = [m[:, None, :] for m in jnp.split(mod, 6, axis=-1)]
    csh1, csc1, cg1, csh2, csc2, cg2 = jnp.split(modc, 6, axis=-1)
    h = layer_norm(x) * (1 + sc1) + sh1
    hc = layer_norm(xc) * (1 + csc1) + csh1
    pl = mixer_projections(h, w_in, gla_gate_w, gla_gate_b, gqa_q_norm, gqa_k_norm)
    pc = mixer_projections(hc, w_in, gla_gate_w, gla_gate_b, gqa_q_norm, gqa_k_norm)
    y_na = neighbourhood_attention(pl['na_q'], pl['na_k'], pl['na_v'], pc['na_k'], pc['na_v'], na_rpb, rows)
    s0 = jnp.zeros((B, GLA_HEADS, GLA_DK, GLA_DV), jnp.float32)
    o_gla_c, s_f, s_b = gla_bidirectional(pc['gla_q'], pc['gla_k'], pc['gla_v'], pc['gla_la_f'], pc['gla_la_b'], s0, s0)
    o_gla_l, _, _ = gla_bidirectional(pl['gla_q'], pl['gla_k'], pl['gla_v'], pl['gla_la_f'], pl['gla_la_b'], s_f, s_b)
    y_gla = gla_output(o_gla_l, pl['gla_gate'], gla_norm_w)
    q_l = apply_axial_rope(pl['gqa_q'], rope_cos, rope_sin)
    k_l = apply_axial_rope(pl['gqa_k'], rope_cos, rope_sin)
    k_all = jnp.concatenate([pc['gqa_k'], k_l], axis=1)
    v_all = jnp.concatenate([pc['gqa_v'], pl['gqa_v']], axis=1)
    y_gqa = gqa_block_attention(q_l, k_all, v_all)
    y = jnp.concatenate([y_na, y_gla, y_gqa], axis=-1) @ w_out
    x = post_ln(DEEPNORM_ALPHA * x + g1 * y, ln1_g, ln1_b)
    h2 = layer_norm(x) * (1 + sc2) + sh2
    x = post_ln(DEEPNORM_ALPHA * x + g2 * peer(h2, peer_wq, peer_subkeys, peer_u, peer_v), ln2_g, ln2_b)
    if update_ctx:
        yc = jnp.concatenate([
            ctx_attention(pc['na_q'], pc['na_k'], pc['na_v']),
            gla_output(o_gla_c, pc['gla_gate'], gla_norm_w),
            ctx_attention(pc['gqa_q'], pc['gqa_k'], pc['gqa_v'])], axis=-1) @ w_out
        xc = post_ln(DEEPNORM_ALPHA * xc + cg1 * yc, ln1_g, ln1_b)
        hc2 = layer_norm(xc) * (1 + csc2) + csh2
        xc = post_ln(DEEPNORM_ALPHA * xc + cg2 * peer(hc2, peer_wq, peer_subkeys, peer_u, peer_v), ln2_g, ln2_b)
    return x, xc


def setup_inputs(seed: int = 0) -> dict:
    key = jax.random.key(seed)
    ks = jax.random.split(key, 22)
    nrm = lambda k, shape, s: jax.random.normal(k, shape, jnp.float32) * s
    D = D_MODEL
    return {
        'x': nrm(ks[0], (BATCH, SEQ, D), 1.0),
        'c': nrm(ks[1], (BATCH, D), 1.0),
        'ctx': nrm(ks[2], (BATCH, CTX_LEN, D), 1.0),
        'c_ctx': nrm(ks[3], (D,), 1.0),
        'w_mod': nrm(ks[4], (DEPTH, D, 6 * D), 0.5 * D ** -0.5),
        'b_mod': nrm(ks[5], (DEPTH, 6 * D), 0.01),
        'w_in': nrm(ks[6], (DEPTH, D, IN_W), D ** -0.5),
        'na_rpb': nrm(ks[7], (DEPTH, NA_HEADS, 2 * NA_WIN_H - 1, 2 * NA_WIN_W - 1), 0.02),
        'gla_gate_w': nrm(ks[8], (DEPTH, 2, GLA_GATE_RANK, GLA_KW), GLA_GATE_RANK ** -0.5),
        'gla_gate_b': nrm(ks[9], (DEPTH, 2, GLA_KW), 0.1),
        'gla_norm_w': 1.0 + nrm(ks[10], (DEPTH, GLA_DV), 0.02),
        'gqa_q_norm': 1.0 + nrm(ks[11], (DEPTH, GQA_HEAD_DIM), 0.02),
        'gqa_k_norm': 1.0 + nrm(ks[12], (DEPTH, GQA_HEAD_DIM), 0.02),
        'w_out': nrm(ks[13], (DEPTH, MIX_W, D), DEEPNORM_BETA * MIX_W ** -0.5),
        'ln1_g': 1.0 + nrm(ks[14], (DEPTH, D), 0.02),
        'ln1_b': nrm(ks[15], (DEPTH, D), 0.02),
        'peer_wq': nrm(ks[16], (DEPTH, D, PEER_HEADS * PEER_KEY_DIM), D ** -0.5),
        'peer_subkeys': nrm(ks[17], (DEPTH, PEER_HEADS, 2, PEER_NKEYS, PEER_KEY_DIM // 2), (PEER_KEY_DIM // 2) ** -0.5),
        'peer_u': nrm(ks[18], (DEPTH, PEER_EXPERTS, D), D ** -0.5),
        'peer_v': nrm(ks[19], (DEPTH, PEER_EXPERTS, D), DEEPNORM_BETA),
        'ln2_g': 1.0 + nrm(ks[20], (DEPTH, D), 0.02),
        'ln2_b': nrm(ks[21], (DEPTH, D), 0.02),
    }


def reference(x, c, ctx, c_ctx, w_mod, b_mod, w_in, na_rpb, gla_gate_w, gla_gate_b, gla_norm_w,
              gqa_q_norm, gqa_k_norm, w_out, ln1_g, ln1_b, peer_wq, peer_subkeys, peer_u, peer_v,
              ln2_g, ln2_b):
    T = x.shape[1]
    rows = T // GRID_W
    rope_cos, rope_sin = axial_rope_tables(T, GQA_HEAD_DIM)
    sc = jax.nn.silu(c)
    scc = jax.nn.silu(c_ctx)
    xc = ctx
    for l in range(DEPTH):
        mod = sc @ w_mod[l] + b_mod[l]
        modc = scc @ w_mod[l] + b_mod[l]
        x, xc = trunk_layer(x, xc, mod, modc, w_in[l], na_rpb[l], gla_gate_w[l], gla_gate_b[l], gla_norm_w[l],
                            gqa_q_norm[l], gqa_k_norm[l], w_out[l], ln1_g[l], ln1_b[l], peer_wq[l],
                            peer_subkeys[l], peer_u[l], peer_v[l], ln2_g[l], ln2_b[l],
                            rope_cos, rope_sin, rows, l < DEPTH - 1)
    return x
```

```python
import functools
import math

import numpy as np
import jax
import jax.numpy as jnp
from jax import lax
from jax.experimental import pallas as pl
from jax.experimental.pallas import tpu as pltpu

F32 = jnp.float32
BF16 = jnp.bfloat16

D_MODEL = 2048
DEPTH = 4
GRID_W = 64
EPS = 1e-6

NA_HEADS = 8
NA_HEAD_DIM = 64
NA_WIN_H = 8
NA_WIN_W = 16
GLA_HEADS = 4
GLA_DK = 128
GLA_DV = 256
GLA_GATE_RANK = 16
GLA_GATE_NORM = 16.0
GLA_CHUNK = 64
GQA_HEADS = 8
GQA_KV_HEADS = 2
GQA_HEAD_DIM = 64
ROPE_BASE = 10000.0
PEER_HEADS = 8
PEER_NKEYS = 128
PEER_KEY_DIM = 256
PEER_TOPK = 16

NA_W = NA_HEADS * NA_HEAD_DIM
GLA_KW = GLA_HEADS * GLA_DK
GLA_VW = GLA_HEADS * GLA_DV
GQA_QW = GQA_HEADS * GQA_HEAD_DIM
GQA_KVW = GQA_KV_HEADS * GQA_HEAD_DIM
DEEPNORM_ALPHA = (2.0 * DEPTH) ** 0.25

LANES = 128
VMEM_LIMIT_BYTES = 56 * 1024 * 1024

NA_QROWS = 4
NA_BAND = NA_QROWS + NA_WIN_H
NA_QBLK = NA_QROWS * GRID_W
NA_KBLK = NA_BAND * GRID_W
GLA_BLK = 256
NEG = -1e30


def _cparams(sem):
    return pltpu.CompilerParams(dimension_semantics=sem, vmem_limit_bytes=VMEM_LIMIT_BYTES)


def _pick(n, cands):
    for c in cands:
        if n % c == 0:
            return c
    raise ValueError(f"no block size in {cands} divides {n}")


def _dot(a, b):
    return jnp.dot(a, b, preferred_element_type=F32)


def _dot_nt(a, b):
    return lax.dot_general(a, b, (((1,), (1,)), ((), ())), preferred_element_type=F32)


def _dot_tn(a, b):
    return lax.dot_general(a, b, (((0,), (0,)), ((), ())), preferred_element_type=F32)


def _layer_norm(x):
    mu = jnp.mean(x, axis=-1, keepdims=True)
    xc = x - mu
    var = jnp.mean(xc * xc, axis=-1, keepdims=True)
    return xc * lax.rsqrt(var + EPS)


def _row_is_ctx(block_idx, tm, n_ctx):
    rows = block_idx * tm + lax.broadcasted_iota(jnp.int32, (tm, 1), 0)
    return rows < n_ctx


def _mod_row(mod_ref, idx, is_ctx):
    return jnp.where(is_ctx, mod_ref[0, idx:idx + 1, :], mod_ref[1, idx:idx + 1, :])


def _mod_kernel(c_ref, w_ref, b_ref, o_ref):
    w = w_ref[0]
    reps = w.shape[1] // LANES
    for m in range(2):
        cb = c_ref[m]
        s = cb / (1.0 + jnp.exp(-cb))
        o_ref[0, m:m + 1, :] = jnp.sum(w * jnp.tile(s, (1, reps)), axis=0, keepdims=True) + b_ref[0]


def _modulation(c2, w_mod, b_mod):
    depth, d, n = w_mod.shape
    tn = _pick(n, (512, 256, 128))
    cb = jnp.broadcast_to(c2[:, :, None], (2, d, LANES))
    return pl.pallas_call(
        _mod_kernel,
        grid=(depth, n // tn),
        in_specs=[pl.BlockSpec((2, d, LANES), lambda l, j: (0, 0, 0)),
                  pl.BlockSpec((1, d, tn), lambda l, j: (l, 0, j)),
                  pl.BlockSpec((1, 1, tn), lambda l, j: (l, 0, j))],
        out_specs=pl.BlockSpec((1, 2, tn), lambda l, j: (l, 0, j)),
        out_shape=jax.ShapeDtypeStruct((depth, 2, n), F32),
        compiler_params=_cparams(("arbitrary", "arbitrary")),
        name="modulation",
    )(cb, w_mod, b_mod.reshape(depth, 1, n))


def _proj_kernel(x_ref, mod_ref, w_ref, o_ref, *, n_ctx, tm):
    is_ctx = _row_is_ctx(pl.program_id(1), tm, n_ctx)
    xn = _layer_norm(x_ref[...])
    h = xn * (1.0 + _mod_row(mod_ref, 1, is_ctx)) + _mod_row(mod_ref, 0, is_ctx)
    o_ref[...] = _dot(h.astype(BF16), w_ref[...]).astype(o_ref.dtype)


def _project(xa, mod, w, out_dtype, n_ctx, tn):
    tt, d = xa.shape
    n = w.shape[1]
    tm = _pick(tt, (384, 256))
    return pl.pallas_call(
        functools.partial(_proj_kernel, n_ctx=n_ctx, tm=tm),
        grid=(n // tn, tt // tm),
        in_specs=[pl.BlockSpec((tm, d), lambda j, i: (i, 0)),
                  pl.BlockSpec((2, 6, d), lambda j, i: (0, 0, 0)),
                  pl.BlockSpec((d, tn), lambda j, i: (0, j))],
        out_specs=pl.BlockSpec((tm, tn), lambda j, i: (i, j)),
        out_shape=jax.ShapeDtypeStruct((tt, n), out_dtype),
        compiler_params=_cparams(("arbitrary", "arbitrary")),
        name="ln_mod_project",
    )(xa, mod, w)


def _na_kernel(q_ref, k_ref, v_ref, bias_ref, o_ref, *, n_ctx, n_band_starts):
    qb = pl.program_id(1)
    scale = NA_HEAD_DIM ** -0.5
    lane = lax.broadcasted_iota(jnp.int32, (1, 2 * NA_HEAD_DIM), 1)
    q = q_ref[...]
    kc = k_ref[0:n_ctx, :]
    vc = v_ref[0:n_ctx, :]

    def head_q(h):
        return jnp.where((lane // NA_HEAD_DIM) == h, q, jnp.zeros_like(q))

    @pl.when(qb == 0)
    def _():
        outs = []
        for h in range(2):
            s = _dot_nt(head_q(h), kc) * scale
            m = jnp.max(s, axis=-1, keepdims=True)
            p = jnp.exp(s - m)
            l = jnp.sum(p, axis=-1, keepdims=True)
            outs.append(_dot(p.astype(BF16), vc) / l)
        o_ref[...] = jnp.where((lane // NA_HEAD_DIM) == 0, outs[0], outs[1]).astype(o_ref.dtype)

    @pl.when(qb > 0)
    def _():
        start_blk = jnp.clip(qb - 2, 0, n_band_starts - 1)
        start = pl.multiple_of(n_ctx + start_blk * NA_QBLK, NA_QBLK)
        kb = k_ref[pl.ds(start, NA_KBLK), :]
        vb = v_ref[pl.ds(start, NA_KBLK), :]
        outs = []
        for h in range(2):
            qh = head_q(h)
            sw = _dot_nt(qh, kb) * scale + bias_ref[0, h]
            sc = _dot_nt(qh, kc) * scale
            m = jnp.maximum(jnp.max(sw, axis=-1, keepdims=True), jnp.max(sc, axis=-1, keepdims=True))
            pw = jnp.exp(sw - m)
            pc = jnp.exp(sc - m)
            l = jnp.sum(pw, axis=-1, keepdims=True) + jnp.sum(pc, axis=-1, keepdims=True)
            outs.append((_dot(pw.astype(BF16), vb) + _dot(pc.astype(BF16), vc)) / l)
        o_ref[...] = jnp.where((lane // NA_HEAD_DIM) == 0, outs[0], outs[1]).astype(o_ref.dtype)


def _na_bias(rpb, rows):
    a = np.arange(NA_QROWS)[:, None]
    j = np.arange(NA_BAND)[None, :]
    ws = [np.zeros_like(a), a, np.full_like(a, NA_BAND - NA_WIN_H)]
    off = [0, -NA_WIN_H // 2, -(NA_QROWS + NA_WIN_H // 2)]
    qc = np.arange(GRID_W)[:, None]
    kc = np.arange(GRID_W)[None, :]
    wcs = np.clip(qc - NA_WIN_W // 2, 0, GRID_W - NA_WIN_W)
    col_ok = (kc >= wcs) & (kc < wcs + NA_WIN_W)
    dc = np.clip(kc - qc + NA_WIN_W - 1, 0, 2 * NA_WIN_W - 2)
    out = []
    for p in range(3):
        row_ok = (j >= ws[p]) & (j < ws[p] + NA_WIN_H)
        dr = np.clip(j - a + off[p] + NA_WIN_H - 1, 0, 2 * NA_WIN_H - 2)
        ok = row_ok[:, None, :, None] & col_ok[None, :, None, :]
        b = rpb[:, dr[:, None, :, None], dc[None, :, None, :]]
        b = jnp.where(ok[None], b, NEG)
        out.append(b.reshape(rpb.shape[0], NA_QBLK, NA_KBLK))
    return jnp.stack(out)


def _neighbourhood_attention(p_na, bias, n_ctx):
    tt = p_na.shape[0]
    n_qb = tt // NA_QBLK
    n_lat_blk = n_qb - 1
    n_band_starts = n_lat_blk - NA_BAND // NA_QROWS + 1
    hp = NA_HEADS // 2

    def bias_idx(h, qb):
        pat = jnp.where(qb <= 1, 0, jnp.where(qb == n_qb - 1, 2, 1))
        return (pat, h, 0, 0)

    return pl.pallas_call(
        functools.partial(_na_kernel, n_ctx=n_ctx, n_band_starts=n_band_starts),
        grid=(hp, n_qb),
        in_specs=[pl.BlockSpec((NA_QBLK, 128), lambda h, qb: (qb, h)),
                  pl.BlockSpec((tt, 128), lambda h, qb: (0, hp + h)),
                  pl.BlockSpec((tt, 128), lambda h, qb: (0, 2 * hp + h)),
                  pl.BlockSpec((1, 2, NA_QBLK, NA_KBLK), bias_idx)],
        out_specs=pl.BlockSpec((NA_QBLK, 128), lambda h, qb: (qb, h)),
        out_shape=jax.ShapeDtypeStruct((tt, NA_W), BF16),
        compiler_params=_cparams(("arbitrary", "arbitrary")),
        name="neighbourhood_attention",
    )(p_na, p_na, p_na, bias)


def _log_sigmoid(x):
    return jnp.minimum(x, 0.0) - jnp.log(1.0 + jnp.exp(-jnp.abs(x)))


def _gla_chunk(q, k, v, lr, wg, bg, tri, causal, s_ref, last_row):
    x = jnp.dot(lr, wg, preferred_element_type=F32, precision=lax.Precision.HIGHEST) + bg
    la = _log_sigmoid(x) * (1.0 / GLA_GATE_NORM)
    b = jnp.dot(tri, la, preferred_element_type=F32, precision=lax.Precision.HIGHEST)
    b_last = b[last_row:last_row + 1, :]
    q_t = (q * (GLA_DK ** -0.5) * jnp.exp(b)).astype(BF16)
    k_t = (k * jnp.exp(-b)).astype(BF16)
    k_end = (k * jnp.exp(b_last - b)).astype(BF16)
    dec = jnp.exp(b_last)
    outs = []
    for h in range(GLA_HEADS):
        ks = slice(h * GLA_DK, (h + 1) * GLA_DK)
        vh = v[:, h * GLA_DV:(h + 1) * GLA_DV].astype(BF16)
        att = jnp.where(causal, _dot_nt(q_t[:, ks], k_t[:, ks]), 0.0)
        st = s_ref[h]
        o = _dot(att.astype(BF16), vh) + _dot_nt(q_t[:, ks], st.astype(BF16))
        s_ref[h] = st * dec[:, ks] + _dot_tn(vh, k_end[:, ks])
        outs.append(o)
    return jnp.concatenate(outs, axis=-1)


def _gla_kernel(qf_ref, kf_ref, vf_ref, lf_ref, qb_ref, kb_ref, vb_ref, lb_ref, wg_ref, bg_ref,
                of_ref, ob_ref, sf_ref, sb_ref):
    @pl.when(pl.program_id(0) == 0)
    def _():
        sf_ref[...] = jnp.zeros_like(sf_ref)
        sb_ref[...] = jnp.zeros_like(sb_ref)

    r = lax.broadcasted_iota(jnp.int32, (GLA_CHUNK, GLA_CHUNK), 0)
    c = lax.broadcasted_iota(jnp.int32, (GLA_CHUNK, GLA_CHUNK), 1)
    lower = r >= c
    upper = r <= c
    tri_f = lower.astype(F32)
    tri_b = upper.astype(F32)
    n_chunks = GLA_BLK // GLA_CHUNK
    for ci in range(n_chunks):
        rows = slice(ci * GLA_CHUNK, (ci + 1) * GLA_CHUNK)
        of_ref[rows, :] = _gla_chunk(qf_ref[rows, :], kf_ref[rows, :], vf_ref[rows, :], lf_ref[rows, :],
                                     wg_ref[0], bg_ref[0], tri_f, lower, sf_ref, GLA_CHUNK - 1)
        cb = n_chunks - 1 - ci
        rows = slice(cb * GLA_CHUNK, (cb + 1) * GLA_CHUNK)
        ob_ref[rows, :] = _gla_chunk(qb_ref[rows, :], kb_ref[rows, :], vb_ref[rows, :], lb_ref[rows, :],
                                     wg_ref[1], bg_ref[1], tri_b, upper, sb_ref, 0)


def _gla(p_gla, wg, bg):
    tt = p_gla.shape[0]
    nb = tt // GLA_BLK
    fwd = lambda i: i
    bwd = lambda i: jnp.where(i == 0, 0, nb - i)
    lr_blk = (2 * GLA_KW + 2 * GLA_VW) // 128

    def specs(rowmap):
        return [pl.BlockSpec((GLA_BLK, GLA_KW), lambda i: (rowmap(i), 0)),
                pl.BlockSpec((GLA_BLK, GLA_KW), lambda i: (rowmap(i), 1)),
                pl.BlockSpec((GLA_BLK, GLA_VW), lambda i: (rowmap(i), 1)),
                pl.BlockSpec((GLA_BLK, 128), lambda i: (rowmap(i), lr_blk))]

    return pl.pallas_call(
        _gla_kernel,
        grid=(nb,),
        in_specs=specs(fwd) + specs(bwd) + [
            pl.BlockSpec((2, 128, GLA_KW), lambda i: (0, 0, 0)),
            pl.BlockSpec((2, 1, GLA_KW), lambda i: (0, 0, 0))],
        out_specs=[pl.BlockSpec((GLA_BLK, GLA_VW), lambda i: (fwd(i), 0)),
                   pl.BlockSpec((GLA_BLK, GLA_VW), lambda i: (bwd(i), 0))],
        out_shape=[jax.ShapeDtypeStruct((tt, GLA_VW), F32)] * 2,
        scratch_shapes=[pltpu.VMEM((GLA_HEADS, GLA_DV, GLA_DK), F32)] * 2,
        compiler_params=_cparams(("arbitrary",)),
        name="gla_bidirectional",
    )(*([p_gla] * 8), wg, bg)


def _group_sum_sq(x, ones_bd):
    sq = x * x
    hi = sq.astype(BF16)
    lo = (sq - hi.astype(F32)).astype(BF16)
    return _dot(hi, ones_bd) + _dot(lo, ones_bd)


def _gqa_prep_kernel(p_ref, ta_ref, tb_ref, ones_ref, qt_ref, kz_ref, vt_ref):
    d = GQA_HEAD_DIM
    x = p_ref[...]
    q, qs = x[:, 0:GQA_QW], x[:, GQA_QW:2 * GQA_QW]
    o = 2 * GQA_QW
    k, ks, v = x[:, o:o + GQA_KVW], x[:, o + GQA_KVW:o + 2 * GQA_KVW], x[:, o + 2 * GQA_KVW:o + 3 * GQA_KVW]
    ta, tb = ta_ref[...], tb_ref[...]
    taq = jnp.tile(ta[:, 0:128], (1, GQA_QW // 128))
    tbq = jnp.tile(tb[:, 0:128], (1, GQA_QW // 128))
    rq = lax.rsqrt(_group_sum_sq(q, ones_ref[...]) * (1.0 / d) + EPS)
    qr = rq * (q * taq + qs * tbq)
    rk = lax.rsqrt(_group_sum_sq(k, ones_ref[0:GQA_KVW, 0:GQA_KVW]) * (1.0 / d) + EPS)
    kr = rk * (k * ta[:, 128:256] + ks * tb[:, 128:256])
    qt_ref[...] = qr.T.astype(BF16)
    lane = lax.broadcasted_iota(jnp.int32, (1, GQA_KVW), 1)
    for g in range(GQA_KV_HEADS):
        kz_ref[g] = jnp.where((lane // d) == g, kr, 0.0).astype(BF16)
    vt_ref[...] = v.T.astype(BF16)


def _gqa_prep(p_gqa, ta, tb):
    tt, n = p_gqa.shape
    tm = _pick(tt, (384, 256, 128))
    hid = np.arange(GQA_QW) // GQA_HEAD_DIM
    ones_bd = jnp.asarray(hid[:, None] == hid[None, :], BF16)
    return pl.pallas_call(
        _gqa_prep_kernel,
        grid=(tt // tm,),
        in_specs=[pl.BlockSpec((tm, n), lambda i: (i, 0)),
                  pl.BlockSpec((tm, 256), lambda i: (i, 0)),
                  pl.BlockSpec((tm, 256), lambda i: (i, 0)),
                  pl.BlockSpec((GQA_QW, GQA_QW), lambda i: (0, 0))],
        out_specs=[pl.BlockSpec((GQA_QW, tm), lambda i: (0, i)),
                   pl.BlockSpec((GQA_KV_HEADS, tm, GQA_KVW), lambda i: (0, i, 0)),
                   pl.BlockSpec((GQA_KVW, tm), lambda i: (0, i))],
        out_shape=[jax.ShapeDtypeStruct((GQA_QW, tt), BF16),
                   jax.ShapeDtypeStruct((GQA_KV_HEADS, tt, GQA_KVW), BF16),
                   jax.ShapeDtypeStruct((GQA_KVW, tt), BF16)],
        compiler_params=_cparams(("arbitrary",)),
        name="gqa_prep",
    )(p_gqa, ta, tb, ones_bd)


def _gqa_kernel(qt_ref, kz_ref, vt_ref, o_ref, *, n_ctx):
    qb = pl.program_id(1)
    d = GQA_HEAD_DIM
    group = GQA_HEADS // GQA_KV_HEADS

    def attend(n_keys):
        kz = kz_ref[0, 0:n_keys, :]
        vt = vt_ref[:, 0:n_keys]
        outs = []
        for g in range(group):
            qg = qt_ref[g * d:(g + 1) * d, :]
            s = _dot(kz, jnp.concatenate([qg, qg], axis=0))
            m = jnp.max(s, axis=0, keepdims=True)
            p = jnp.exp(s - m)
            l = jnp.sum(p, axis=0, keepdims=True)
            outs.append(_dot(vt, p.astype(BF16)) / l)
        o_ref[...] = jnp.concatenate(outs, axis=0).T.astype(o_ref.dtype)

    @pl.when(qb == 0)
    def _():
        attend(n_ctx)

    @pl.when(qb > 0)
    def _():
        attend(kz_ref.shape[1])


def _gqa_attention(qt, kz, vt, n_ctx):
    tt = qt.shape[1]
    tq = n_ctx
    group = GQA_HEADS // GQA_KV_HEADS
    gw = group * GQA_HEAD_DIM
    return pl.pallas_call(
        functools.partial(_gqa_kernel, n_ctx=n_ctx),
        grid=(GQA_KV_HEADS, tt // tq),
        in_specs=[pl.BlockSpec((gw, tq), lambda g, i: (g, i)),
                  pl.BlockSpec((1, tt, GQA_KVW), lambda g, i: (g, 0, 0)),
                  pl.BlockSpec((GQA_HEAD_DIM, tt), lambda g, i: (g, 0))],
        out_specs=pl.BlockSpec((tq, gw), lambda g, i: (i, g)),
        out_shape=jax.ShapeDtypeStruct((tt, GQA_QW), BF16),
        compiler_params=_cparams(("arbitrary", "arbitrary")),
        name="gqa_attention",
    )(qt, kz, vt)


def _mix_out_kernel(x_ref, mod_ref, yna_ref, of_ref, ob_ref, gate_ref, ygqa_ref, w_ref, nw_ref,
                    g_ref, b_ref, x1_ref, h2t_ref, *, n_ctx, tm):
    is_ctx = _row_is_ctx(pl.program_id(0), tm, n_ctx)
    o = of_ref[...] + ob_ref[...]
    nw = nw_ref[...]
    parts = []
    for h in range(GLA_HEADS):
        oh = o[:, h * GLA_DV:(h + 1) * GLA_DV]
        parts.append(oh * lax.rsqrt(jnp.mean(oh * oh, axis=-1, keepdims=True) + EPS) * nw)
    gate = gate_ref[...]
    y_gla = (jnp.concatenate(parts, axis=-1) * (gate / (1.0 + jnp.exp(-gate)))).astype(BF16)
    y = (_dot(yna_ref[...], w_ref[0:NA_W, :]) + _dot(y_gla, w_ref[NA_W:NA_W + GLA_VW, :])
         + _dot(ygqa_ref[...], w_ref[NA_W + GLA_VW:, :]))
    x1 = _layer_norm(DEEPNORM_ALPHA * x_ref[...] + _mod_row(mod_ref, 2, is_ctx) * y) * g_ref[...] + b_ref[...]
    x1_ref[...] = x1
    h2 = _layer_norm(x1) * (1.0 + _mod_row(mod_ref, 4, is_ctx)) + _mod_row(mod_ref, 3, is_ctx)
    h2t_ref[...] = h2.T.astype(BF16)


def _mix_out(xa, mod, y_na, o_f, o_b, p_gla, y_gqa, w_out, norm_w, ln_g, ln_b, n_ctx):
    tt, d = xa.shape
    tm = _pick(tt, (384, 256, 128))
    row = lambda i: (i, 0)
    const2 = lambda i: (0, 0)
    return pl.pallas_call(
        functools.partial(_mix_out_kernel, n_ctx=n_ctx, tm=tm),
        grid=(tt // tm,),
        in_specs=[pl.BlockSpec((tm, d), row),
                  pl.BlockSpec((2, 6, d), lambda i: (0, 0, 0)),
                  pl.BlockSpec((tm, NA_W), row),
                  pl.BlockSpec((tm, GLA_VW), row),
                  pl.BlockSpec((tm, GLA_VW), row),
                  pl.BlockSpec((tm, GLA_VW), lambda i: (i, 2)),
                  pl.BlockSpec((tm, GQA_QW), row),
                  pl.BlockSpec(w_out.shape, const2),
                  pl.BlockSpec((1, GLA_DV), const2),
                  pl.BlockSpec((1, d), const2),
                  pl.BlockSpec((1, d), const2)],
        out_specs=[pl.BlockSpec((tm, d), row), pl.BlockSpec((d, tm), lambda i: (0, i))],
        out_shape=[jax.ShapeDtypeStruct((tt, d), F32), jax.ShapeDtypeStruct((d, tt), BF16)],
        compiler_params=_cparams(("arbitrary",)),
        name="mixer_out_postln",
    )(xa, mod, y_na, o_f, o_b, p_gla, y_gqa, w_out, norm_w, ln_g, ln_b)


def _top16(s, key_iota):
    cur = s
    rank = jnp.full(s.shape, float(PEER_TOPK), F32)
    vals = []
    for r in range(PEER_TOPK):
        m = jnp.max(cur, axis=0, keepdims=True)
        idx = jnp.min(jnp.where(cur == m, key_iota, float(PEER_NKEYS)), axis=0, keepdims=True)
        sel = key_iota == idx
        rank = jnp.where(sel, float(r), rank)
        cur = jnp.where(sel, -jnp.inf, cur)
        vals.append(m)
    return jnp.concatenate(vals, axis=0), rank


def _cand_layout():
    k = PEER_TOPK
    rows = [(0, rb) for rb in range(k)]
    for ra in range(1, 5):
        rows += [(ra, rb) for rb in range(8)]
    rows += [(ra, 0) for ra in range(8, k)]
    rows += [(ra, 1) for ra in range(8)]
    rows += [(ra, 0) for ra in range(8)]
    seen, valid = set(), []
    for pair in rows:
        valid.append(pair not in seen)
        seen.add(pair)
    needed = {(ra, rb) for ra in range(k) for rb in range(k) if (ra + 1) * (rb + 1) <= k}
    assert needed <= seen
    ra = np.array([p[0] for p in rows], np.float32)
    rb = np.array([p[1] for p in rows], np.float32)
    return ra, rb, np.array(valid)


def _route_kernel(h_ref, wq_ref, sk_ref, cst_ref, r2_ref, e2_ref, n1_ref, e1_ref):
    tt = h_ref.shape[1]
    k = PEER_TOPK
    qt = _dot(wq_ref[...], h_ref[...])
    key_iota = lax.broadcasted_iota(jnp.int32, (PEER_NKEYS, tt), 0).astype(F32)
    rank_iota = lax.broadcasted_iota(jnp.int32, (k, tt), 0).astype(F32)
    cst = cst_ref[...]
    n_rows = cst.shape[0]
    c_ra = jnp.broadcast_to(cst[:, 0:1], (n_rows, tt))
    c_flat = jnp.broadcast_to(cst[:, 1:2], (n_rows, tt))
    c_ok = jnp.broadcast_to(cst[:, 2:3], (n_rows, tt)) > 0.5
    half = PEER_KEY_DIM // 2
    for h in range(PEER_HEADS):
        base = h * PEER_KEY_DIM
        s1 = _dot(sk_ref[h, 0], qt[base:base + half, :].astype(BF16))
        s2 = _dot(sk_ref[h, 1], qt[base + half:base + 2 * half, :].astype(BF16))
        a, r1 = _top16(s1, key_iota)
        b, r2 = _top16(s2, key_iota)
        blocks = [a[0:1, :] + b]
        for ra in range(1, 5):
            blocks.append(a[ra:ra + 1, :] + b[0:8, :])
        blocks.append(a[8:k, :] + b[0:1, :])
        blocks.append(a[0:8, :] + b[1:2, :])
        blocks.append(a[0:8, :] + b[0:1, :])
        cand = jnp.where(c_ok, jnp.concatenate(blocks, axis=0), -jnp.inf)
        m0 = a[0:1, :] + b[0:1, :]
        cnt = jnp.zeros((k, tt), F32)
        z = jnp.zeros((1, tt), F32)
        for _ in range(k):
            m = jnp.max(cand, axis=0, keepdims=True)
            idx = jnp.min(jnp.where(cand == m, c_flat, 1e9), axis=0, keepdims=True)
            sel = c_flat == idx
            ra_sel = jnp.sum(jnp.where(sel, c_ra, 0.0), axis=0, keepdims=True)
            cnt = cnt + jnp.where(rank_iota == ra_sel, 1.0, 0.0)
            z = z + jnp.exp(m - m0)
            cand = jnp.where(sel, -jnp.inf, cand)
        n1 = jnp.zeros((PEER_NKEYS, tt), F32)
        for r in range(k):
            n1 = n1 + jnp.where(r1 == float(r), cnt[r:r + 1, :], 0.0)
        r2_ref[h] = r2
        e2_ref[h] = jnp.exp(s2 - b[0:1, :]) / z
        n1_ref[h] = n1
        e1_ref[h] = jnp.exp(s1 - a[0:1, :])


def _peer_route(h2t, wq_t, sub_keys):
    d, tt = h2t.shape
    tb = _pick(tt, (256, 128))
    ra, rb, valid = _cand_layout()
    cst = np.zeros((ra.shape[0], LANES), np.float32)
    flat = np.where(valid, ra * PEER_TOPK + rb, 1000.0 + np.arange(ra.shape[0]))
    cst[:, 0], cst[:, 1], cst[:, 2] = ra, flat, valid
    shp = (PEER_HEADS, PEER_NKEYS, tt)
    out_spec = pl.BlockSpec((PEER_HEADS, PEER_NKEYS, tb), lambda i: (0, 0, i))
    return pl.pallas_call(
        _route_kernel,
        grid=(tt // tb,),
        in_specs=[pl.BlockSpec((d, tb), lambda i: (0, i)),
                  pl.BlockSpec(wq_t.shape, lambda i: (0, 0)),
                  pl.BlockSpec(sub_keys.shape, lambda i: (0, 0, 0, 0)),
                  pl.BlockSpec(cst.shape, lambda i: (0, 0))],
        out_specs=[out_spec] * 4,
        out_shape=[jax.ShapeDtypeStruct(shp, F32)] * 4,
        compiler_params=_cparams(("arbitrary",)),
        name="peer_route",
    )(h2t, wq_t, sub_keys, jnp.asarray(cst))


def _gelu_tanh(x):
    return 0.5 * x * (1.0 + jnp.tanh(math.sqrt(2.0 / math.pi) * (x + 0.044715 * (x * x * x))))


def _peer_kernel(h_ref, u_ref, vt_ref, r2_ref, e2_ref, n1_ref, e1_ref, o_ref):
    e = pl.program_id(1)
    eb = u_ref.shape[0]
    n_i = eb // PEER_NKEYS

    @pl.when(e == 0)
    def _():
        o_ref[...] = jnp.zeros_like(o_ref)

    act = _gelu_tanh(_dot(u_ref[...], h_ref[...]))
    parts = []
    for ii in range(n_i):
        i = e * n_i + ii
        g = jnp.zeros((PEER_NKEYS, act.shape[1]), F32)
        for h in range(PEER_HEADS):
            n1 = n1_ref[h, pl.ds(i, 1), :]
            e1 = e1_ref[h, pl.ds(i, 1), :]
            g = g + jnp.where(r2_ref[h] < n1, e2_ref[h], 0.0) * e1
        parts.append((act[ii * PEER_NKEYS:(ii + 1) * PEER_NKEYS, :] * g).astype(BF16))
    o_ref[...] += _dot(vt_ref[...], jnp.concatenate(parts, axis=0))


def _peer_experts(h2t, u_tab, vt_tab, r2, e2, n1, e1):
    d, tt = h2t.shape
    n_exp = u_tab.shape[0]
    tb = _pick(tt, (384, 256, 128))
    eb = 512
    route_spec = pl.BlockSpec((PEER_HEADS, PEER_NKEYS, tb), lambda i, e: (0, 0, i))
    return pl.pallas_call(
        _peer_kernel,
        grid=(tt // tb, n_exp // eb),
        in_specs=[pl.BlockSpec((d, tb), lambda i, e: (0, i)),
                  pl.BlockSpec((eb, d), lambda i, e: (e, 0)),
                  pl.BlockSpec((d, eb), lambda i, e: (0, e)),
                  route_spec, route_spec, route_spec, route_spec],
        out_specs=pl.BlockSpec((d, tb), lambda i, e: (0, i)),
        out_shape=jax.ShapeDtypeStruct((d, tt), F32),
        compiler_params=_cparams(("arbitrary", "arbitrary")),
        name="peer_experts",
    )(h2t, u_tab, vt_tab, r2, e2, n1, e1)


def _peer_out_kernel(x_ref, mod_ref, pt_ref, g_ref, b_ref, o_ref, *, n_ctx, tm):
    is_ctx = _row_is_ctx(pl.program_id(0), tm, n_ctx)
    y = pt_ref[...].T
    o_ref[...] = (_layer_norm(DEEPNORM_ALPHA * x_ref[...] + _mod_row(mod_ref, 5, is_ctx) * y)
                  * g_ref[...] + b_ref[...])


def _peer_out(x1, mod, peer_t, ln_g, ln_b, n_ctx):
    tt, d = x1.shape
    tm = _pick(tt, (384, 256, 128))
    return pl.pallas_call(
        functools.partial(_peer_out_kernel, n_ctx=n_ctx, tm=tm),
        grid=(tt // tm,),
        in_specs=[pl.BlockSpec((tm, d), lambda i: (i, 0)),
                  pl.BlockSpec((2, 6, d), lambda i: (0, 0, 0)),
                  pl.BlockSpec((d, tm), lambda i: (0, i)),
                  pl.BlockSpec((1, d), lambda i: (0, 0)),
                  pl.BlockSpec((1, d), lambda i: (0, 0))],
        out_specs=pl.BlockSpec((tm, d), lambda i: (i, 0)),
        out_shape=jax.ShapeDtypeStruct((tt, d), F32),
        compiler_params=_cparams(("arbitrary",)),
        name="peer_out_postln",
    )(x1, mod, peer_t, ln_g, ln_b)


def _rope_tables(n_ctx, n_lat):
    half = GQA_HEAD_DIM // 2
    inv = ROPE_BASE ** (-jnp.arange(0, half, 2, dtype=F32) / half)
    t = jnp.arange(n_lat)
    row = (t // GRID_W).astype(F32)
    col = (t % GRID_W).astype(F32)
    ang = jnp.concatenate([row[:, None] * inv, col[:, None] * inv], axis=-1)
    cos, sin = jnp.cos(ang), jnp.sin(ang)
    cos2 = jnp.concatenate([jnp.ones((n_ctx, 2 * half), F32), jnp.concatenate([cos, cos], -1)], 0)
    sin2 = jnp.concatenate([jnp.zeros((n_ctx, 2 * half), F32), jnp.concatenate([-sin, sin], -1)], 0)
    return cos2, sin2


_EVEN_ODD = np.concatenate([np.arange(0, GQA_HEAD_DIM, 2), np.arange(1, GQA_HEAD_DIM, 2)])
_ODD_EVEN = np.concatenate([np.arange(1, GQA_HEAD_DIM, 2), np.arange(0, GQA_HEAD_DIM, 2)])


def _head_cols(perm, n_heads):
    return np.concatenate([h * GQA_HEAD_DIM + perm for h in range(n_heads)])


def kernel(x, c, ctx, c_ctx, w_mod, b_mod, w_in, na_rpb, gla_gate_w, gla_gate_b, gla_norm_w,
           gqa_q_norm, gqa_k_norm, w_out, ln1_g, ln1_b, peer_wq, peer_subkeys, peer_u, peer_v,
           ln2_g, ln2_b):
    batch, n_lat, d = x.shape
    assert batch == 1 and d == D_MODEL
    n_ctx = ctx.shape[1]
    rows = n_lat // GRID_W
    depth = w_mod.shape[0]

    xa = jnp.concatenate([ctx[0], x[0]], axis=0)
    mods = _modulation(jnp.stack([c_ctx, c[0]]), w_mod, b_mod).reshape(depth, 2, 6, d)

    o = np.cumsum([0, NA_W, NA_W, NA_W, GLA_KW, GLA_KW, GLA_VW, GLA_VW, 2 * GLA_GATE_RANK,
                   GQA_QW, GQA_KVW, GQA_KVW])
    w_na = w_in[:, :, o[0]:o[3]].astype(BF16)
    w_gla = jnp.concatenate([w_in[:, :, o[3]:o[8]],
                             jnp.zeros((depth, d, 128 - 2 * GLA_GATE_RANK), F32)], -1).astype(BF16)
    wq_c, wk_c, wv_c = w_in[:, :, o[8]:o[9]], w_in[:, :, o[9]:o[10]], w_in[:, :, o[10]:o[11]]
    w_gqa = jnp.concatenate([wq_c[:, :, _head_cols(_EVEN_ODD, GQA_HEADS)],
                             wq_c[:, :, _head_cols(_ODD_EVEN, GQA_HEADS)],
                             wk_c[:, :, _head_cols(_EVEN_ODD, GQA_KV_HEADS)],
                             wk_c[:, :, _head_cols(_ODD_EVEN, GQA_KV_HEADS)],
                             wv_c], -1).astype(BF16)
    wg = jnp.zeros((depth, 2, 128, GLA_KW), F32)
    wg = wg.at[:, 0, 0:GLA_GATE_RANK].set(gla_gate_w[:, 0])
    wg = wg.at[:, 1, GLA_GATE_RANK:2 * GLA_GATE_RANK].set(gla_gate_w[:, 1])
    bg = gla_gate_b.reshape(depth, 2, 1, GLA_KW)
    w_out_b = w_out.astype(BF16)
    wq_t = jnp.swapaxes(peer_wq, 1, 2).astype(BF16)
    sub_keys = peer_subkeys.astype(BF16)
    u_tab = peer_u.astype(BF16)
    vt_tab = jnp.swapaxes(peer_v, 1, 2).astype(BF16)

    cos2, sin2 = _rope_tables(n_ctx, n_lat)
    q_scale = GQA_HEAD_DIM ** -0.5

    for l in range(depth):
        mod = mods[l]
        p_na = _project(xa, mod, w_na[l], BF16, n_ctx, w_na.shape[-1])
        p_gla = _project(xa, mod, w_gla[l], F32, n_ctx, w_gla.shape[-1])
        p_gqa = _project(xa, mod, w_gqa[l], F32, n_ctx, w_gqa.shape[-1])

        y_na = _neighbourhood_attention(p_na, _na_bias(na_rpb[l], rows), n_ctx)
        o_f, o_b = _gla(p_gla, wg[l], bg[l])

        wqn, wkn = gqa_q_norm[l], gqa_k_norm[l]
        ta = jnp.concatenate([jnp.tile(cos2 * wqn[_EVEN_ODD] * q_scale, (1, 2)),
                              jnp.tile(cos2 * wkn[_EVEN_ODD], (1, 2))], -1)
        tb = jnp.concatenate([jnp.tile(sin2 * wqn[_ODD_EVEN] * q_scale, (1, 2)),
                              jnp.tile(sin2 * wkn[_ODD_EVEN], (1, 2))], -1)
        qt, kz, vt = _gqa_prep(p_gqa, ta, tb)
        y_gqa = _gqa_attention(qt, kz, vt, n_ctx)

        x1, h2t = _mix_out(xa, mod, y_na, o_f, o_b, p_gla, y_gqa, w_out_b[l],
                           gla_norm_w[l].reshape(1, GLA_DV), ln1_g[l].reshape(1, d), ln1_b[l].reshape(1, d), n_ctx)
        r2, e2, n1, e1 = _peer_route(h2t, wq_t[l], sub_keys[l])
        peer_t = _peer_experts(h2t, u_tab[l], vt_tab[l], r2, e2, n1, e1)
        xa = _peer_out(x1, mod, peer_t, ln2_g[l].reshape(1, d), ln2_b[l].reshape(1, d), n_ctx)

    return xa[n_ctx:][None]
```

```python
import functools
import math

import numpy as np
import jax
import jax.numpy as jnp
from jax import lax
from jax.experimental import pallas as pl
from jax.experimental.pallas import tpu as pltpu

F32 = jnp.float32
BF16 = jnp.bfloat16

D_MODEL = 2048
DEPTH = 4
GRID_W = 64
EPS = 1e-6

NA_HEADS = 8
NA_HEAD_DIM = 64
NA_WIN_H = 8
NA_WIN_W = 16
GLA_HEADS = 4
GLA_DK = 128
GLA_DV = 256
GLA_GATE_RANK = 16
GLA_GATE_NORM = 16.0
GLA_CHUNK = 64
GQA_HEADS = 8
GQA_KV_HEADS = 2
GQA_HEAD_DIM = 64
ROPE_BASE = 10000.0
PEER_HEADS = 8
PEER_NKEYS = 128
PEER_KEY_DIM = 256
PEER_TOPK = 16

NA_W = NA_HEADS * NA_HEAD_DIM
GLA_KW = GLA_HEADS * GLA_DK
GLA_VW = GLA_HEADS * GLA_DV
GQA_QW = GQA_HEADS * GQA_HEAD_DIM
GQA_KVW = GQA_KV_HEADS * GQA_HEAD_DIM
DEEPNORM_ALPHA = (2.0 * DEPTH) ** 0.25

LANES = 128
VMEM_LIMIT_BYTES = 56 * 1024 * 1024

NA_QROWS = 4
NA_BAND = NA_QROWS + NA_WIN_H
NA_QBLK = NA_QROWS * GRID_W
NA_KBLK = NA_BAND * GRID_W
GLA_BLK = 256
GQA_KEY_CHUNK = 256
GQA_ONES_ROWS = 16
PEER_EXPERT_BLK = 1024
PEER_SUB = 256
NEG = -1e30


def _cparams(sem):
    return pltpu.CompilerParams(dimension_semantics=sem, vmem_limit_bytes=VMEM_LIMIT_BYTES)


def _pick(n, cands):
    for c in cands:
        if n % c == 0:
            return c
    raise ValueError(f"no block size in {cands} divides {n}")


def _dot(a, b):
    return jnp.dot(a, b, preferred_element_type=F32)


def _dot_nt(a, b):
    return lax.dot_general(a, b, (((1,), (1,)), ((), ())), preferred_element_type=F32)


def _dot_tn(a, b):
    return lax.dot_general(a, b, (((0,), (0,)), ((), ())), preferred_element_type=F32)


def _layer_norm(x):
    mu = jnp.mean(x, axis=-1, keepdims=True)
    xc = x - mu
    var = jnp.mean(xc * xc, axis=-1, keepdims=True)
    return xc * lax.rsqrt(var + EPS)


def _row_is_ctx(block_idx, tm, n_ctx):
    rows = block_idx * tm + lax.broadcasted_iota(jnp.int32, (tm, 1), 0)
    return rows < n_ctx


def _mod_row(mod_ref, idx, is_ctx):
    return jnp.where(is_ctx, mod_ref[0, idx:idx + 1, :], mod_ref[1, idx:idx + 1, :])


def _mod_kernel(c_ref, w_ref, b_ref, o_ref):
    w = w_ref[0]
    reps = w.shape[1] // LANES
    for m in range(2):
        cb = c_ref[m]
        s = cb / (1.0 + jnp.exp(-cb))
        o_ref[0, m:m + 1, :] = jnp.sum(w * jnp.tile(s, (1, reps)), axis=0, keepdims=True) + b_ref[0]


def _modulation(c2, w_mod, b_mod):
    depth, d, n = w_mod.shape
    tn = _pick(n, (512, 256, 128))
    cb = jnp.broadcast_to(c2[:, :, None], (2, d, LANES))
    return pl.pallas_call(
        _mod_kernel,
        grid=(depth, n // tn),
        in_specs=[pl.BlockSpec((2, d, LANES), lambda l, j: (0, 0, 0)),
                  pl.BlockSpec((1, d, tn), lambda l, j: (l, 0, j)),
                  pl.BlockSpec((1, 1, tn), lambda l, j: (l, 0, j))],
        out_specs=pl.BlockSpec((1, 2, tn), lambda l, j: (l, 0, j)),
        out_shape=jax.ShapeDtypeStruct((depth, 2, n), F32),
        compiler_params=_cparams(("arbitrary", "arbitrary")),
        name="modulation",
    )(cb, w_mod, b_mod.reshape(depth, 1, n))


def _proj_kernel(x_ref, mod_ref, w_ref, o_ref, *, n_ctx, tm):
    is_ctx = _row_is_ctx(pl.program_id(1), tm, n_ctx)
    xn = _layer_norm(x_ref[...])
    h = xn * (1.0 + _mod_row(mod_ref, 1, is_ctx)) + _mod_row(mod_ref, 0, is_ctx)
    o_ref[...] = _dot(h.astype(BF16), w_ref[...]).astype(o_ref.dtype)


def _project(xa, mod, w, out_dtype, n_ctx, tn):
    tt, d = xa.shape
    n = w.shape[1]
    tm = _pick(tt, (384, 256))
    return pl.pallas_call(
        functools.partial(_proj_kernel, n_ctx=n_ctx, tm=tm),
        grid=(n // tn, tt // tm),
        in_specs=[pl.BlockSpec((tm, d), lambda j, i: (i, 0)),
                  pl.BlockSpec((2, 6, d), lambda j, i: (0, 0, 0)),
                  pl.BlockSpec((d, tn), lambda j, i: (0, j))],
        out_specs=pl.BlockSpec((tm, tn), lambda j, i: (i, j)),
        out_shape=jax.ShapeDtypeStruct((tt, n), out_dtype),
        compiler_params=_cparams(("arbitrary", "arbitrary")),
        name="ln_mod_project",
    )(xa, mod, w)


def _na_kernel(q_ref, k_ref, v_ref, bias_ref, o_ref, *, n_ctx, n_band_starts):
    qb = pl.program_id(1)
    scale = NA_HEAD_DIM ** -0.5
    lane = lax.broadcasted_iota(jnp.int32, (1, 2 * NA_HEAD_DIM), 1)
    q = q_ref[...]
    kc = k_ref[0:n_ctx, :]
    vc = v_ref[0:n_ctx, :]

    def head_q(h):
        return jnp.where((lane // NA_HEAD_DIM) == h, q, jnp.zeros_like(q))

    @pl.when(qb == 0)
    def _():
        outs = []
        for h in range(2):
            s = _dot_nt(head_q(h), kc) * scale
            m = jnp.max(s, axis=-1, keepdims=True)
            p = jnp.exp(s - m)
            l = jnp.sum(p, axis=-1, keepdims=True)
            outs.append(_dot(p.astype(BF16), vc) / l)
        o_ref[...] = jnp.where((lane // NA_HEAD_DIM) == 0, outs[0], outs[1]).astype(o_ref.dtype)

    @pl.when(qb > 0)
    def _():
        start_blk = jnp.clip(qb - 2, 0, n_band_starts - 1)
        start = pl.multiple_of(n_ctx + start_blk * NA_QBLK, NA_QBLK)
        kb = k_ref[pl.ds(start, NA_KBLK), :]
        vb = v_ref[pl.ds(start, NA_KBLK), :]
        outs = []
        for h in range(2):
            qh = head_q(h)
            sw = _dot_nt(qh, kb) * scale + bias_ref[0, h]
            sc = _dot_nt(qh, kc) * scale
            m = jnp.maximum(jnp.max(sw, axis=-1, keepdims=True), jnp.max(sc, axis=-1, keepdims=True))
            pw = jnp.exp(sw - m)
            pc = jnp.exp(sc - m)
            l = jnp.sum(pw, axis=-1, keepdims=True) + jnp.sum(pc, axis=-1, keepdims=True)
            outs.append((_dot(pw.astype(BF16), vb) + _dot(pc.astype(BF16), vc)) / l)
        o_ref[...] = jnp.where((lane // NA_HEAD_DIM) == 0, outs[0], outs[1]).astype(o_ref.dtype)


def _na_bias(rpb):
    a = np.arange(NA_QROWS)[:, None]
    j = np.arange(NA_BAND)[None, :]
    ws = [np.zeros_like(a), a, np.full_like(a, NA_BAND - NA_WIN_H)]
    off = [0, -NA_WIN_H // 2, -(NA_QROWS + NA_WIN_H // 2)]
    qc = np.arange(GRID_W)[:, None]
    kc = np.arange(GRID_W)[None, :]
    wcs = np.clip(qc - NA_WIN_W // 2, 0, GRID_W - NA_WIN_W)
    col_ok = (kc >= wcs) & (kc < wcs + NA_WIN_W)
    dc = np.clip(kc - qc + NA_WIN_W - 1, 0, 2 * NA_WIN_W - 2)
    oh_c = (dc[:, :, None] == np.arange(2 * NA_WIN_W - 1)).astype(np.float32)
    oh_r, ok = [], []
    for p in range(3):
        row_ok = (j >= ws[p]) & (j < ws[p] + NA_WIN_H)
        dr = np.clip(j - a + off[p] + NA_WIN_H - 1, 0, 2 * NA_WIN_H - 2)
        oh_r.append((dr[:, :, None] == np.arange(2 * NA_WIN_H - 1)).astype(np.float32))
        ok.append(row_ok[:, None, :, None] & col_ok[None, :, None, :])
    b = jnp.einsum('pajr,lhrc,qkc->lphaqjk', np.stack(oh_r), rpb, oh_c, precision=lax.Precision.HIGHEST)
    b = jnp.where(np.stack(ok)[None, :, None], b, NEG)
    return b.reshape(rpb.shape[0], 3, rpb.shape[1], NA_QBLK, NA_KBLK)


def _neighbourhood_attention(p_na, bias, n_ctx):
    tt = p_na.shape[0]
    n_qb = tt // NA_QBLK
    n_lat_blk = n_qb - 1
    n_band_starts = n_lat_blk - NA_BAND // NA_QROWS + 1
    hp = NA_HEADS // 2

    def bias_idx(h, qb):
        pat = jnp.where(qb <= 1, 0, jnp.where(qb == n_qb - 1, 2, 1))
        return (pat, h, 0, 0)

    return pl.pallas_call(
        functools.partial(_na_kernel, n_ctx=n_ctx, n_band_starts=n_band_starts),
        grid=(hp, n_qb),
        in_specs=[pl.BlockSpec((NA_QBLK, 128), lambda h, qb: (qb, h)),
                  pl.BlockSpec((tt, 128), lambda h, qb: (0, hp + h)),
                  pl.BlockSpec((tt, 128), lambda h, qb: (0, 2 * hp + h)),
                  pl.BlockSpec((1, 2, NA_QBLK, NA_KBLK), bias_idx)],
        out_specs=pl.BlockSpec((NA_QBLK, 128), lambda h, qb: (qb, h)),
        out_shape=jax.ShapeDtypeStruct((tt, NA_W), BF16),
        compiler_params=_cparams(("arbitrary", "arbitrary")),
        name="neighbourhood_attention",
    )(p_na, p_na, p_na, bias)


def _log_sigmoid(x):
    return jnp.minimum(x, 0.0) - jnp.log(1.0 + jnp.exp(-jnp.abs(x)))


def _gla_chunk(q, k, v, lr, wg, bg, tri, causal, s_ref, last_row):
    x = jnp.dot(lr, wg, preferred_element_type=F32, precision=lax.Precision.HIGHEST) + bg
    la = _log_sigmoid(x) * (1.0 / GLA_GATE_NORM)
    b = jnp.dot(tri, la, preferred_element_type=F32, precision=lax.Precision.HIGHEST)
    b_last = b[last_row:last_row + 1, :]
    q_t = (q * (GLA_DK ** -0.5) * jnp.exp(b)).astype(BF16)
    k_t = (k * jnp.exp(-b)).astype(BF16)
    k_end = (k * jnp.exp(b_last - b)).astype(BF16)
    dec = jnp.exp(b_last)
    outs = []
    for h in range(GLA_HEADS):
        ks = slice(h * GLA_DK, (h + 1) * GLA_DK)
        vh = v[:, h * GLA_DV:(h + 1) * GLA_DV].astype(BF16)
        att = jnp.where(causal, _dot_nt(q_t[:, ks], k_t[:, ks]), 0.0)
        st = s_ref[h]
        o = _dot(att.astype(BF16), vh) + _dot_nt(q_t[:, ks], st.astype(BF16))
        s_ref[h] = st * dec[:, ks] + _dot_tn(vh, k_end[:, ks])
        outs.append(o)
    return jnp.concatenate(outs, axis=-1)


def _gla_kernel(qf_ref, kf_ref, vf_ref, lf_ref, qb_ref, kb_ref, vb_ref, lb_ref, wg_ref, bg_ref,
                of_ref, ob_ref, sf_ref, sb_ref):
    @pl.when(pl.program_id(0) == 0)
    def _():
        sf_ref[...] = jnp.zeros_like(sf_ref)
        sb_ref[...] = jnp.zeros_like(sb_ref)

    r = lax.broadcasted_iota(jnp.int32, (GLA_CHUNK, GLA_CHUNK), 0)
    c = lax.broadcasted_iota(jnp.int32, (GLA_CHUNK, GLA_CHUNK), 1)
    lower = r >= c
    upper = r <= c
    tri_f = lower.astype(F32)
    tri_b = upper.astype(F32)
    n_chunks = GLA_BLK // GLA_CHUNK
    for ci in range(n_chunks):
        rows = slice(ci * GLA_CHUNK, (ci + 1) * GLA_CHUNK)
        of_ref[rows, :] = _gla_chunk(qf_ref[rows, :], kf_ref[rows, :], vf_ref[rows, :], lf_ref[rows, :],
                                     wg_ref[0], bg_ref[0], tri_f, lower, sf_ref, GLA_CHUNK - 1)
        cb = n_chunks - 1 - ci
        rows = slice(cb * GLA_CHUNK, (cb + 1) * GLA_CHUNK)
        ob_ref[rows, :] = _gla_chunk(qb_ref[rows, :], kb_ref[rows, :], vb_ref[rows, :], lb_ref[rows, :],
                                     wg_ref[1], bg_ref[1], tri_b, upper, sb_ref, 0)


def _gla(p_gla, wg, bg):
    tt = p_gla.shape[0]
    nb = tt // GLA_BLK
    fwd = lambda i: i
    bwd = lambda i: jnp.where(i == 0, 0, nb - i)
    lr_blk = (2 * GLA_KW + 2 * GLA_VW) // 128

    def specs(rowmap):
        return [pl.BlockSpec((GLA_BLK, GLA_KW), lambda i: (rowmap(i), 0)),
                pl.BlockSpec((GLA_BLK, GLA_KW), lambda i: (rowmap(i), 1)),
                pl.BlockSpec((GLA_BLK, GLA_VW), lambda i: (rowmap(i), 1)),
                pl.BlockSpec((GLA_BLK, 128), lambda i: (rowmap(i), lr_blk))]

    return pl.pallas_call(
        _gla_kernel,
        grid=(nb,),
        in_specs=specs(fwd) + specs(bwd) + [
            pl.BlockSpec((2, 128, GLA_KW), lambda i: (0, 0, 0)),
            pl.BlockSpec((2, 1, GLA_KW), lambda i: (0, 0, 0))],
        out_specs=[pl.BlockSpec((GLA_BLK, GLA_VW), lambda i: (fwd(i), 0)),
                   pl.BlockSpec((GLA_BLK, GLA_VW), lambda i: (bwd(i), 0))],
        out_shape=[jax.ShapeDtypeStruct((tt, GLA_VW), F32)] * 2,
        scratch_shapes=[pltpu.VMEM((GLA_HEADS, GLA_DV, GLA_DK), F32)] * 2,
        compiler_params=_cparams(("arbitrary",)),
        name="gla_bidirectional",
    )(*([p_gla] * 8), wg, bg)


def _group_sum_sq(x, ones_bd):
    sq = x * x
    hi = sq.astype(BF16)
    lo = (sq - hi.astype(F32)).astype(BF16)
    return _dot(hi, ones_bd) + _dot(lo, ones_bd)


def _gqa_prep_kernel(p_ref, ta_ref, tb_ref, ones_ref, qt_ref, kz_ref, vt_ref):
    d = GQA_HEAD_DIM
    x = p_ref[...]
    q, qs = x[:, 0:GQA_QW], x[:, GQA_QW:2 * GQA_QW]
    o = 2 * GQA_QW
    k, ks, v = x[:, o:o + GQA_KVW], x[:, o + GQA_KVW:o + 2 * GQA_KVW], x[:, o + 2 * GQA_KVW:o + 3 * GQA_KVW]
    ta, tb = ta_ref[...], tb_ref[...]
    taq = jnp.tile(ta[:, 0:128], (1, GQA_QW // 128))
    tbq = jnp.tile(tb[:, 0:128], (1, GQA_QW // 128))
    rq = lax.rsqrt(_group_sum_sq(q, ones_ref[...]) * (1.0 / d) + EPS)
    qr = rq * (q * taq + qs * tbq)
    rk = lax.rsqrt(_group_sum_sq(k, ones_ref[0:GQA_KVW, 0:GQA_KVW]) * (1.0 / d) + EPS)
    kr = rk * (k * ta[:, 128:256] + ks * tb[:, 128:256])
    qt_ref[...] = qr.T.astype(BF16)
    lane = lax.broadcasted_iota(jnp.int32, (1, GQA_KVW), 1)
    for g in range(GQA_KV_HEADS):
        kz_ref[g] = jnp.where((lane // d) == g, kr, 0.0).astype(BF16)
    vt_ref[...] = v.T.astype(BF16)


def _gqa_prep(p_gqa, ta, tb):
    tt, n = p_gqa.shape
    tm = _pick(tt, (384, 256, 128))
    hid = np.arange(GQA_QW) // GQA_HEAD_DIM
    ones_bd = jnp.asarray(hid[:, None] == hid[None, :], BF16)
    return pl.pallas_call(
        _gqa_prep_kernel,
        grid=(tt // tm,),
        in_specs=[pl.BlockSpec((tm, n), lambda i: (i, 0)),
                  pl.BlockSpec((tm, 256), lambda i: (i, 0)),
                  pl.BlockSpec((tm, 256), lambda i: (i, 0)),
                  pl.BlockSpec((GQA_QW, GQA_QW), lambda i: (0, 0))],
        out_specs=[pl.BlockSpec((GQA_QW, tm), lambda i: (0, i)),
                   pl.BlockSpec((GQA_KV_HEADS, tm, GQA_KVW), lambda i: (0, i, 0)),
                   pl.BlockSpec((GQA_KVW, tm), lambda i: (0, i))],
        out_shape=[jax.ShapeDtypeStruct((GQA_QW, tt), BF16),
                   jax.ShapeDtypeStruct((GQA_KV_HEADS, tt, GQA_KVW), BF16),
                   jax.ShapeDtypeStruct((GQA_KVW, tt), BF16)],
        compiler_params=_cparams(("arbitrary",)),
        name="gqa_prep",
    )(p_gqa, ta, tb, ones_bd)


def _gqa_kernel(qt_ref, kz_ref, vt_ref, o_ref, s0_ref, s1_ref, acc_ref, *, n_ctx):
    qb = pl.program_id(1)
    d = GQA_HEAD_DIM
    group = GQA_HEADS // GQA_KV_HEADS
    kc = GQA_KEY_CHUNK
    tq = qt_ref.shape[1]
    q4 = jnp.concatenate([jnp.concatenate([qt_ref[g * d:(g + 1) * d, :]] * 2, axis=0) for g in range(group)],
                         axis=1)

    def scores(c, s_ref):
        start = pl.multiple_of(c * kc, kc)
        s = _dot(kz_ref[0, pl.ds(start, kc), :], q4)
        s_ref[...] = s
        return jnp.max(s, axis=0, keepdims=True)

    def accumulate(c, s_ref, m_chunk, m):
        m_new = jnp.maximum(m, m_chunk)
        p = jnp.exp2(s_ref[...] - m_new)
        acc_ref[...] = acc_ref[...] * jnp.exp2(m - m_new) + _dot(vt_ref[0, c], p.astype(BF16))
        return m_new

    def attend(n_chunks):
        acc_ref[...] = jnp.zeros_like(acc_ref)
        m = jnp.full((1, group * tq), -jnp.inf, F32)
        mc0 = scores(0, s0_ref)

        def body(j, carry):
            m, mc0 = carry
            mc1 = scores(2 * j + 1, s1_ref)
            m = accumulate(2 * j, s0_ref, mc0, m)
            mc0 = scores(2 * j + 2, s0_ref)
            m = accumulate(2 * j + 1, s1_ref, mc1, m)
            return m, mc0

        m, mc0 = lax.fori_loop(0, (n_chunks - 1) // 2, body, (m, mc0))
        accumulate(n_chunks - 1, s0_ref, mc0, m)
        out = acc_ref[0:d, :] / acc_ref[d:d + 1, :]
        o_ref[...] = jnp.concatenate([out[:, g * tq:(g + 1) * tq] for g in range(group)],
                                     axis=0).T.astype(o_ref.dtype)

    @pl.when(qb == 0)
    def _():
        attend(n_ctx // kc)

    @pl.when(qb > 0)
    def _():
        attend(kz_ref.shape[1] // kc)


def _gqa_attention(qt, kz, vt, n_ctx):
    tt = qt.shape[1]
    tq = n_ctx
    kc = GQA_KEY_CHUNK
    group = GQA_HEADS // GQA_KV_HEADS
    gw = group * GQA_HEAD_DIM
    assert tt % kc == 0 and (tt // kc) % 2 == 1 and n_ctx % kc == 0 and (n_ctx // kc) % 2 == 1
    vt3 = vt.reshape(GQA_KV_HEADS, GQA_HEAD_DIM, tt // kc, kc).transpose(0, 2, 1, 3)
    extra = jnp.zeros((GQA_KV_HEADS, tt // kc, GQA_ONES_ROWS, kc), BF16).at[:, :, 0, :].set(1.0)
    vt3 = jnp.concatenate([vt3, extra], axis=2)
    vrows = GQA_HEAD_DIM + GQA_ONES_ROWS
    return pl.pallas_call(
        functools.partial(_gqa_kernel, n_ctx=n_ctx),
        grid=(GQA_KV_HEADS, tt // tq),
        in_specs=[pl.BlockSpec((gw, tq), lambda g, i: (g, i)),
                  pl.BlockSpec((1, tt, GQA_KVW), lambda g, i: (g, 0, 0)),
                  pl.BlockSpec((1, tt // kc, vrows, kc), lambda g, i: (g, 0, 0, 0))],
        out_specs=pl.BlockSpec((tq, gw), lambda g, i: (i, g)),
        out_shape=jax.ShapeDtypeStruct((tt, GQA_QW), BF16),
        scratch_shapes=[pltpu.VMEM((kc, group * tq), F32), pltpu.VMEM((kc, group * tq), F32),
                        pltpu.VMEM((vrows, group * tq), F32)],
        compiler_params=_cparams(("arbitrary", "arbitrary")),
        name="gqa_attention",
    )(qt, kz, vt3)


def _mix_out_kernel(x_ref, mod_ref, yna_ref, of_ref, ob_ref, gate_ref, ygqa_ref, w_ref, nw_ref,
                    g_ref, b_ref, x1_ref, h2t_ref, *, n_ctx, tm):
    is_ctx = _row_is_ctx(pl.program_id(0), tm, n_ctx)
    o = of_ref[...] + ob_ref[...]
    nw = nw_ref[...]
    parts = []
    for h in range(GLA_HEADS):
        oh = o[:, h * GLA_DV:(h + 1) * GLA_DV]
        parts.append(oh * lax.rsqrt(jnp.mean(oh * oh, axis=-1, keepdims=True) + EPS) * nw)
    gate = gate_ref[...]
    y_gla = (jnp.concatenate(parts, axis=-1) * (gate / (1.0 + jnp.exp(-gate)))).astype(BF16)
    y = (_dot(yna_ref[...], w_ref[0:NA_W, :]) + _dot(y_gla, w_ref[NA_W:NA_W + GLA_VW, :])
         + _dot(ygqa_ref[...], w_ref[NA_W + GLA_VW:, :]))
    x1 = _layer_norm(DEEPNORM_ALPHA * x_ref[...] + _mod_row(mod_ref, 2, is_ctx) * y) * g_ref[...] + b_ref[...]
    x1_ref[...] = x1
    h2 = _layer_norm(x1) * (1.0 + _mod_row(mod_ref, 4, is_ctx)) + _mod_row(mod_ref, 3, is_ctx)
    h2t_ref[...] = h2.T.astype(BF16)


def _mix_out(xa, mod, y_na, o_f, o_b, p_gla, y_gqa, w_out, norm_w, ln_g, ln_b, n_ctx):
    tt, d = xa.shape
    tm = _pick(tt, (384, 256, 128))
    row = lambda i: (i, 0)
    const2 = lambda i: (0, 0)
    return pl.pallas_call(
        functools.partial(_mix_out_kernel, n_ctx=n_ctx, tm=tm),
        grid=(tt // tm,),
        in_specs=[pl.BlockSpec((tm, d), row),
                  pl.BlockSpec((2, 6, d), lambda i: (0, 0, 0)),
                  pl.BlockSpec((tm, NA_W), row),
                  pl.BlockSpec((tm, GLA_VW), row),
                  pl.BlockSpec((tm, GLA_VW), row),
                  pl.BlockSpec((tm, GLA_VW), lambda i: (i, 2)),
                  pl.BlockSpec((tm, GQA_QW), row),
                  pl.BlockSpec(w_out.shape, const2),
                  pl.BlockSpec((1, GLA_DV), const2),
                  pl.BlockSpec((1, d), const2),
                  pl.BlockSpec((1, d), const2)],
        out_specs=[pl.BlockSpec((tm, d), row), pl.BlockSpec((d, tm), lambda i: (0, i))],
        out_shape=[jax.ShapeDtypeStruct((tt, d), F32), jax.ShapeDtypeStruct((d, tt), BF16)],
        compiler_params=_cparams(("arbitrary",)),
        name="mixer_out_postln",
    )(xa, mod, y_na, o_f, o_b, p_gla, y_gqa, w_out, norm_w, ln_g, ln_b)


def _top16(s, key_iota):
    cur = s
    rank = jnp.full(s.shape, float(PEER_TOPK), F32)
    vals = []
    for r in range(PEER_TOPK):
        m = jnp.max(cur, axis=0, keepdims=True)
        idx = jnp.min(jnp.where(cur == m, key_iota, float(PEER_NKEYS)), axis=0, keepdims=True)
        sel = key_iota == idx
        rank = jnp.where(sel, float(r), rank)
        cur = jnp.where(sel, -jnp.inf, cur)
        vals.append(m)
    return jnp.concatenate(vals, axis=0), rank


def _cand_layout():
    k = PEER_TOPK
    rows = [(0, rb) for rb in range(k)]
    for ra in range(1, 5):
        rows += [(ra, rb) for rb in range(8)]
    rows += [(ra, 0) for ra in range(8, k)]
    rows += [(ra, 1) for ra in range(8)]
    rows += [(ra, 0) for ra in range(8)]
    seen, valid = set(), []
    for pair in rows:
        valid.append(pair not in seen)
        seen.add(pair)
    needed = {(ra, rb) for ra in range(k) for rb in range(k) if (ra + 1) * (rb + 1) <= k}
    assert needed <= seen
    ra = np.array([p[0] for p in rows], np.float32)
    rb = np.array([p[1] for p in rows], np.float32)
    return ra, rb, np.array(valid)


def _route_kernel(h_ref, wq_ref, sk_ref, cst_ref, r2_ref, e2_ref, n1_ref, e1_ref):
    tt = h_ref.shape[1]
    k = PEER_TOPK
    qt = _dot(wq_ref[...], h_ref[...])
    key_iota = lax.broadcasted_iota(jnp.int32, (PEER_NKEYS, tt), 0).astype(F32)
    rank_iota = lax.broadcasted_iota(jnp.int32, (k, tt), 0).astype(F32)
    cst = cst_ref[...]
    n_rows = cst.shape[0]
    c_ra = jnp.broadcast_to(cst[:, 0:1], (n_rows, tt))
    c_flat = jnp.broadcast_to(cst[:, 1:2], (n_rows, tt))
    c_ok = jnp.broadcast_to(cst[:, 2:3], (n_rows, tt)) > 0.5
    half = PEER_KEY_DIM // 2
    for h in range(PEER_HEADS):
        base = h * PEER_KEY_DIM
        s1 = _dot(sk_ref[h, 0], qt[base:base + half, :].astype(BF16))
        s2 = _dot(sk_ref[h, 1], qt[base + half:base + 2 * half, :].astype(BF16))
        a, r1 = _top16(s1, key_iota)
        b, r2 = _top16(s2, key_iota)
        blocks = [a[0:1, :] + b]
        for ra in range(1, 5):
            blocks.append(a[ra:ra + 1, :] + b[0:8, :])
        blocks.append(a[8:k, :] + b[0:1, :])
        blocks.append(a[0:8, :] + b[1:2, :])
        blocks.append(a[0:8, :] + b[0:1, :])
        cand = jnp.where(c_ok, jnp.concatenate(blocks, axis=0), -jnp.inf)
        m0 = a[0:1, :] + b[0:1, :]
        cnt = jnp.zeros((k, tt), F32)
        z = jnp.zeros((1, tt), F32)
        for _ in range(k):
            m = jnp.max(cand, axis=0, keepdims=True)
            idx = jnp.min(jnp.where(cand == m, c_flat, 1e9), axis=0, keepdims=True)
            sel = c_flat == idx
            ra_sel = jnp.sum(jnp.where(sel, c_ra, 0.0), axis=0, keepdims=True)
            cnt = cnt + jnp.where(rank_iota == ra_sel, 1.0, 0.0)
            z = z + jnp.exp(m - m0)
            cand = jnp.where(sel, -jnp.inf, cand)
        n1 = jnp.zeros((PEER_NKEYS, tt), F32)
        for r in range(k):
            n1 = n1 + jnp.where(r1 == float(r), cnt[r:r + 1, :], 0.0)
        r2_ref[h] = r2
        e2_ref[h] = jnp.exp(s2 - b[0:1, :]) / z
        n1_ref[h] = n1
        e1_ref[h] = jnp.exp(s1 - a[0:1, :])


def _peer_route(h2t, wq_t, sub_keys):
    d, tt = h2t.shape
    tb = _pick(tt, (256, 128))
    ra, rb, valid = _cand_layout()
    cst = np.zeros((ra.shape[0], LANES), np.float32)
    flat = np.where(valid, ra * PEER_TOPK + rb, 1000.0 + np.arange(ra.shape[0]))
    cst[:, 0], cst[:, 1], cst[:, 2] = ra, flat, valid
    shp = (PEER_HEADS, PEER_NKEYS, tt)
    out_spec = pl.BlockSpec((PEER_HEADS, PEER_NKEYS, tb), lambda i: (0, 0, i))
    return pl.pallas_call(
        _route_kernel,
        grid=(tt // tb,),
        in_specs=[pl.BlockSpec((d, tb), lambda i: (0, i)),
                  pl.BlockSpec(wq_t.shape, lambda i: (0, 0)),
                  pl.BlockSpec(sub_keys.shape, lambda i: (0, 0, 0, 0)),
                  pl.BlockSpec(cst.shape, lambda i: (0, 0))],
        out_specs=[out_spec] * 4,
        out_shape=[jax.ShapeDtypeStruct(shp, F32)] * 4,
        compiler_params=_cparams(("arbitrary",)),
        name="peer_route",
    )(h2t, wq_t, sub_keys, jnp.asarray(cst))


def _gelu_tanh(x):
    return 0.5 * x * (1.0 + jnp.tanh(math.sqrt(2.0 / math.pi) * (x + 0.044715 * (x * x * x))))


def _peer_kernel(h_ref, u_ref, vt_ref, r2_ref, e2_ref, n1_ref, e1_ref, o_ref):
    e = pl.program_id(1)
    eb = u_ref.shape[0]
    n_i = eb // PEER_NKEYS

    @pl.when(e == 0)
    def _():
        o_ref[...] = jnp.zeros_like(o_ref)

    n_sub = eb // PEER_SUB
    sub_rows = lambda s: slice(s * PEER_SUB, (s + 1) * PEER_SUB)
    pre = _dot(u_ref[sub_rows(0), :], h_ref[...])
    for s in range(n_sub):
        rows = sub_rows(s)
        cur = pre
        if s + 1 < n_sub:
            pre = _dot(u_ref[sub_rows(s + 1), :], h_ref[...])
        act = _gelu_tanh(cur)
        parts = []
        for ii in range(PEER_SUB // PEER_NKEYS):
            i = e * n_i + s * (PEER_SUB // PEER_NKEYS) + ii
            g = jnp.zeros((PEER_NKEYS, act.shape[1]), F32)
            for h in range(PEER_HEADS):
                n1 = n1_ref[h, pl.ds(i, 1), :]
                e1 = e1_ref[h, pl.ds(i, 1), :]
                g = g + jnp.where(r2_ref[h] < n1, e2_ref[h], 0.0) * e1
            parts.append((act[ii * PEER_NKEYS:(ii + 1) * PEER_NKEYS, :] * g).astype(BF16))
        o_ref[...] += _dot(vt_ref[:, rows], jnp.concatenate(parts, axis=0))


def _peer_experts(h2t, u_tab, vt_tab, r2, e2, n1, e1):
    d, tt = h2t.shape
    n_exp = u_tab.shape[0]
    tb = _pick(tt, (768, 512, 256))
    eb = PEER_EXPERT_BLK
    route_spec = pl.BlockSpec((PEER_HEADS, PEER_NKEYS, tb), lambda i, e: (0, 0, i),
                              pipeline_mode=pl.Buffered(1))
    return pl.pallas_call(
        _peer_kernel,
        grid=(tt // tb, n_exp // eb),
        in_specs=[pl.BlockSpec((d, tb), lambda i, e: (0, i)),
                  pl.BlockSpec((eb, d), lambda i, e: (e, 0)),
                  pl.BlockSpec((d, eb), lambda i, e: (0, e)),
                  route_spec, route_spec, route_spec, route_spec],
        out_specs=pl.BlockSpec((d, tb), lambda i, e: (0, i)),
        out_shape=jax.ShapeDtypeStruct((d, tt), F32),
        compiler_params=_cparams(("arbitrary", "arbitrary")),
        name="peer_experts",
    )(h2t, u_tab, vt_tab, r2, e2, n1, e1)


def _peer_out_kernel(x_ref, mod_ref, pt_ref, g_ref, b_ref, o_ref, *, n_ctx, tm):
    is_ctx = _row_is_ctx(pl.program_id(0), tm, n_ctx)
    y = pt_ref[...].T
    o_ref[...] = (_layer_norm(DEEPNORM_ALPHA * x_ref[...] + _mod_row(mod_ref, 5, is_ctx) * y)
                  * g_ref[...] + b_ref[...])


def _peer_out(x1, mod, peer_t, ln_g, ln_b, n_ctx):
    tt, d = x1.shape
    tm = _pick(tt, (384, 256, 128))
    return pl.pallas_call(
        functools.partial(_peer_out_kernel, n_ctx=n_ctx, tm=tm),
        grid=(tt // tm,),
        in_specs=[pl.BlockSpec((tm, d), lambda i: (i, 0)),
                  pl.BlockSpec((2, 6, d), lambda i: (0, 0, 0)),
                  pl.BlockSpec((d, tm), lambda i: (0, i)),
                  pl.BlockSpec((1, d), lambda i: (0, 0)),
                  pl.BlockSpec((1, d), lambda i: (0, 0))],
        out_specs=pl.BlockSpec((tm, d), lambda i: (i, 0)),
        out_shape=jax.ShapeDtypeStruct((tt, d), F32),
        compiler_params=_cparams(("arbitrary",)),
        name="peer_out_postln",
    )(x1, mod, peer_t, ln_g, ln_b)


def _rope_tables(n_ctx, n_lat):
    half = GQA_HEAD_DIM // 2
    inv = ROPE_BASE ** (-jnp.arange(0, half, 2, dtype=F32) / half)
    t = jnp.arange(n_lat)
    row = (t // GRID_W).astype(F32)
    col = (t % GRID_W).astype(F32)
    ang = jnp.concatenate([row[:, None] * inv, col[:, None] * inv], axis=-1)
    cos, sin = jnp.cos(ang), jnp.sin(ang)
    cos2 = jnp.concatenate([jnp.ones((n_ctx, 2 * half), F32), jnp.concatenate([cos, cos], -1)], 0)
    sin2 = jnp.concatenate([jnp.zeros((n_ctx, 2 * half), F32), jnp.concatenate([-sin, sin], -1)], 0)
    return cos2, sin2


_EVEN_ODD = np.concatenate([np.arange(0, GQA_HEAD_DIM, 2), np.arange(1, GQA_HEAD_DIM, 2)])
_ODD_EVEN = np.concatenate([np.arange(1, GQA_HEAD_DIM, 2), np.arange(0, GQA_HEAD_DIM, 2)])


def _head_cols(perm, n_heads):
    return np.concatenate([h * GQA_HEAD_DIM + perm for h in range(n_heads)])


def kernel(x, c, ctx, c_ctx, w_mod, b_mod, w_in, na_rpb, gla_gate_w, gla_gate_b, gla_norm_w,
           gqa_q_norm, gqa_k_norm, w_out, ln1_g, ln1_b, peer_wq, peer_subkeys, peer_u, peer_v,
           ln2_g, ln2_b):
    batch, n_lat, d = x.shape
    assert batch == 1 and d == D_MODEL
    n_ctx = ctx.shape[1]
    assert n_lat % NA_QBLK == 0 and n_lat // GRID_W >= NA_BAND + NA_QROWS
    depth = w_mod.shape[0]

    xa = jnp.concatenate([ctx[0], x[0]], axis=0)
    mods = _modulation(jnp.stack([c_ctx, c[0]]), w_mod, b_mod).reshape(depth, 2, 6, d)

    o = np.cumsum([0, NA_W, NA_W, NA_W, GLA_KW, GLA_KW, GLA_VW, GLA_VW, 2 * GLA_GATE_RANK,
                   GQA_QW, GQA_KVW, GQA_KVW])
    w_na = w_in[:, :, o[0]:o[3]].astype(BF16)
    w_gla = jnp.concatenate([w_in[:, :, o[3]:o[8]],
                             jnp.zeros((depth, d, 128 - 2 * GLA_GATE_RANK), F32)], -1).astype(BF16)
    wq_c, wk_c, wv_c = w_in[:, :, o[8]:o[9]], w_in[:, :, o[9]:o[10]], w_in[:, :, o[10]:o[11]]
    w_gqa = jnp.concatenate([wq_c[:, :, _head_cols(_EVEN_ODD, GQA_HEADS)],
                             wq_c[:, :, _head_cols(_ODD_EVEN, GQA_HEADS)],
                             wk_c[:, :, _head_cols(_EVEN_ODD, GQA_KV_HEADS)],
                             wk_c[:, :, _head_cols(_ODD_EVEN, GQA_KV_HEADS)],
                             wv_c], -1).astype(BF16)
    wg = jnp.zeros((depth, 2, 128, GLA_KW), F32)
    wg = wg.at[:, 0, 0:GLA_GATE_RANK].set(gla_gate_w[:, 0])
    wg = wg.at[:, 1, GLA_GATE_RANK:2 * GLA_GATE_RANK].set(gla_gate_w[:, 1])
    bg = gla_gate_b.reshape(depth, 2, 1, GLA_KW)
    w_out_b = w_out.astype(BF16)
    wq_t = jnp.swapaxes(peer_wq, 1, 2).astype(BF16)
    sub_keys = peer_subkeys.astype(BF16)
    u_tab = peer_u.astype(BF16)
    vt_tab = jnp.swapaxes(peer_v, 1, 2).astype(BF16)

    cos2, sin2 = _rope_tables(n_ctx, n_lat)
    q_scale = GQA_HEAD_DIM ** -0.5 * math.log2(math.e)
    na_bias = _na_bias(na_rpb)

    for l in range(depth):
        mod = mods[l]
        p_na = _project(xa, mod, w_na[l], BF16, n_ctx, w_na.shape[-1])
        p_gla = _project(xa, mod, w_gla[l], F32, n_ctx, w_gla.shape[-1])
        p_gqa = _project(xa, mod, w_gqa[l], F32, n_ctx, w_gqa.shape[-1])

        y_na = _neighbourhood_attention(p_na, na_bias[l], n_ctx)
        o_f, o_b = _gla(p_gla, wg[l], bg[l])

        wqn, wkn = gqa_q_norm[l], gqa_k_norm[l]
        ta = jnp.concatenate([jnp.tile(cos2 * wqn[_EVEN_ODD] * q_scale, (1, 2)),
                              jnp.tile(cos2 * wkn[_EVEN_ODD], (1, 2))], -1)
        tb = jnp.concatenate([jnp.tile(sin2 * wqn[_ODD_EVEN] * q_scale, (1, 2)),
                              jnp.tile(sin2 * wkn[_ODD_EVEN], (1, 2))], -1)
        qt, kz, vt = _gqa_prep(p_gqa, ta, tb)
        y_gqa = _gqa_attention(qt, kz, vt, n_ctx)

        x1, h2t = _mix_out(xa, mod, y_na, o_f, o_b, p_gla, y_gqa, w_out_b[l],
                           gla_norm_w[l].reshape(1, GLA_DV), ln1_g[l].reshape(1, d), ln1_b[l].reshape(1, d), n_ctx)
        r2, e2, n1, e1 = _peer_route(h2t, wq_t[l], sub_keys[l])
        peer_t = _peer_experts(h2t, u_tab[l], vt_tab[l], r2, e2, n1, e1)
        xa = _peer_out(x1, mod, peer_t, ln2_g[l].reshape(1, d), ln2_b[l].reshape(1, d), n_ctx)

    return xa[n_ctx:][None]
```

```python
import functools
import math

import numpy as np
import jax
import jax.numpy as jnp
from jax import lax
from jax.experimental import pallas as pl
from jax.experimental.pallas import tpu as pltpu

F32 = jnp.float32
BF16 = jnp.bfloat16

D_MODEL = 2048
DEPTH = 4
GRID_W = 64
EPS = 1e-6

NA_HEADS = 8
NA_HEAD_DIM = 64
NA_WIN_H = 8
NA_WIN_W = 16
GLA_HEADS = 4
GLA_DK = 128
GLA_DV = 256
GLA_GATE_RANK = 16
GLA_GATE_NORM = 16.0
GLA_CHUNK = 64
GQA_HEADS = 8
GQA_KV_HEADS = 2
GQA_HEAD_DIM = 64
ROPE_BASE = 10000.0
PEER_HEADS = 8
PEER_NKEYS = 128
PEER_KEY_DIM = 256
PEER_TOPK = 16

NA_W = NA_HEADS * NA_HEAD_DIM
GLA_KW = GLA_HEADS * GLA_DK
GLA_VW = GLA_HEADS * GLA_DV
GQA_QW = GQA_HEADS * GQA_HEAD_DIM
GQA_KVW = GQA_KV_HEADS * GQA_HEAD_DIM
DEEPNORM_ALPHA = (2.0 * DEPTH) ** 0.25

LANES = 128
VMEM_LIMIT_BYTES = 56 * 1024 * 1024

NA_QROWS = 4
NA_BAND = NA_QROWS + NA_WIN_H
NA_QBLK = NA_QROWS * GRID_W
NA_KBLK = NA_BAND * GRID_W
GLA_BLK = 256
GQA_KEY_CHUNK = 256
GQA_ONES_ROWS = 16
PEER_EXPERT_BLK = 1024
PEER_SUB = 256
NEG = -1e30


def _cparams(sem):
    return pltpu.CompilerParams(dimension_semantics=sem, vmem_limit_bytes=VMEM_LIMIT_BYTES)


def _pick(n, cands):
    for c in cands:
        if n % c == 0:
            return c
    raise ValueError(f"no block size in {cands} divides {n}")


def _dot(a, b):
    return jnp.dot(a, b, preferred_element_type=F32)


def _dot_nt(a, b):
    return lax.dot_general(a, b, (((1,), (1,)), ((), ())), preferred_element_type=F32)


def _dot_tn(a, b):
    return lax.dot_general(a, b, (((0,), (0,)), ((), ())), preferred_element_type=F32)


def _layer_norm(x):
    mu = jnp.mean(x, axis=-1, keepdims=True)
    xc = x - mu
    var = jnp.mean(xc * xc, axis=-1, keepdims=True)
    return xc * lax.rsqrt(var + EPS)


def _row_is_ctx(block_idx, tm, n_ctx):
    rows = block_idx * tm + lax.broadcasted_iota(jnp.int32, (tm, 1), 0)
    return rows < n_ctx


def _mod_row(mod_ref, idx, is_ctx):
    return jnp.where(is_ctx, mod_ref[0, idx:idx + 1, :], mod_ref[1, idx:idx + 1, :])


def _mod_kernel(c_ref, w_ref, b_ref, o_ref):
    w = w_ref[0]
    reps = w.shape[1] // LANES
    for m in range(2):
        cb = c_ref[m]
        s = cb / (1.0 + jnp.exp(-cb))
        o_ref[0, m:m + 1, :] = jnp.sum(w * jnp.tile(s, (1, reps)), axis=0, keepdims=True) + b_ref[0]


def _modulation(c2, w_mod, b_mod):
    depth, d, n = w_mod.shape
    tn = _pick(n, (512, 256, 128))
    cb = jnp.broadcast_to(c2[:, :, None], (2, d, LANES))
    return pl.pallas_call(
        _mod_kernel,
        grid=(depth, n // tn),
        in_specs=[pl.BlockSpec((2, d, LANES), lambda l, j: (0, 0, 0)),
                  pl.BlockSpec((1, d, tn), lambda l, j: (l, 0, j)),
                  pl.BlockSpec((1, 1, tn), lambda l, j: (l, 0, j))],
        out_specs=pl.BlockSpec((1, 2, tn), lambda l, j: (l, 0, j)),
        out_shape=jax.ShapeDtypeStruct((depth, 2, n), F32),
        compiler_params=_cparams(("arbitrary", "arbitrary")),
        name="modulation",
    )(cb, w_mod, b_mod.reshape(depth, 1, n))


def _proj_kernel(x_ref, mod_ref, *refs, n_ctx, tm):
    is_ctx = _row_is_ctx(pl.program_id(0), tm, n_ctx)
    xn = _layer_norm(x_ref[...])
    h = (xn * (1.0 + _mod_row(mod_ref, 1, is_ctx)) + _mod_row(mod_ref, 0, is_ctx)).astype(BF16)
    n_groups = len(refs) // 2
    for w_ref, o_ref in zip(refs[:n_groups], refs[n_groups:]):
        o_ref[...] = _dot(h, w_ref[...]).astype(o_ref.dtype)


def _project(xa, mod, weights, out_dtypes, n_ctx):
    tt, d = xa.shape
    tm = _pick(tt, (256, 128))
    w_specs = [pl.BlockSpec(w.shape, lambda i: (0, 0), pipeline_mode=pl.Buffered(1)) for w in weights]
    return pl.pallas_call(
        functools.partial(_proj_kernel, n_ctx=n_ctx, tm=tm),
        grid=(tt // tm,),
        in_specs=[pl.BlockSpec((tm, d), lambda i: (i, 0)),
                  pl.BlockSpec((2, 6, d), lambda i: (0, 0, 0))] + w_specs,
        out_specs=[pl.BlockSpec((tm, w.shape[1]), lambda i: (i, 0)) for w in weights],
        out_shape=[jax.ShapeDtypeStruct((tt, w.shape[1]), dt) for w, dt in zip(weights, out_dtypes)],
        compiler_params=_cparams(("arbitrary",)),
        name="ln_mod_project",
    )(xa, mod, *weights)


def _na_kernel(q_ref, k_ref, v_ref, bias_ref, o_ref, *, n_ctx, n_band_starts):
    qb = pl.program_id(1)
    scale = NA_HEAD_DIM ** -0.5
    lane = lax.broadcasted_iota(jnp.int32, (1, 2 * NA_HEAD_DIM), 1)
    q = q_ref[...]
    kc = k_ref[0:n_ctx, :]
    vc = v_ref[0:n_ctx, :]

    def head_q(h):
        return jnp.where((lane // NA_HEAD_DIM) == h, q, jnp.zeros_like(q))

    @pl.when(qb == 0)
    def _():
        outs = []
        for h in range(2):
            s = _dot_nt(head_q(h), kc) * scale
            m = jnp.max(s, axis=-1, keepdims=True)
            p = jnp.exp(s - m)
            l = jnp.sum(p, axis=-1, keepdims=True)
            outs.append(_dot(p.astype(BF16), vc) / l)
        o_ref[...] = jnp.where((lane // NA_HEAD_DIM) == 0, outs[0], outs[1]).astype(o_ref.dtype)

    @pl.when(qb > 0)
    def _():
        start_blk = jnp.clip(qb - 2, 0, n_band_starts - 1)
        start = pl.multiple_of(n_ctx + start_blk * NA_QBLK, NA_QBLK)
        kb = k_ref[pl.ds(start, NA_KBLK), :]
        vb = v_ref[pl.ds(start, NA_KBLK), :]
        outs = []
        for h in range(2):
            qh = head_q(h)
            sw = _dot_nt(qh, kb) * scale + bias_ref[0, h]
            sc = _dot_nt(qh, kc) * scale
            m = jnp.maximum(jnp.max(sw, axis=-1, keepdims=True), jnp.max(sc, axis=-1, keepdims=True))
            pw = jnp.exp(sw - m)
            pc = jnp.exp(sc - m)
            l = jnp.sum(pw, axis=-1, keepdims=True) + jnp.sum(pc, axis=-1, keepdims=True)
            outs.append((_dot(pw.astype(BF16), vb) + _dot(pc.astype(BF16), vc)) / l)
        o_ref[...] = jnp.where((lane // NA_HEAD_DIM) == 0, outs[0], outs[1]).astype(o_ref.dtype)


def _na_bias(rpb):
    a = np.arange(NA_QROWS)[:, None]
    j = np.arange(NA_BAND)[None, :]
    ws = [np.zeros_like(a), a, np.full_like(a, NA_BAND - NA_WIN_H)]
    off = [0, -NA_WIN_H // 2, -(NA_QROWS + NA_WIN_H // 2)]
    qc = np.arange(GRID_W)[:, None]
    kc = np.arange(GRID_W)[None, :]
    wcs = np.clip(qc - NA_WIN_W // 2, 0, GRID_W - NA_WIN_W)
    col_ok = (kc >= wcs) & (kc < wcs + NA_WIN_W)
    dc = np.clip(kc - qc + NA_WIN_W - 1, 0, 2 * NA_WIN_W - 2)
    oh_c = (dc[:, :, None] == np.arange(2 * NA_WIN_W - 1)).astype(np.float32)
    oh_r, ok = [], []
    for p in range(3):
        row_ok = (j >= ws[p]) & (j < ws[p] + NA_WIN_H)
        dr = np.clip(j - a + off[p] + NA_WIN_H - 1, 0, 2 * NA_WIN_H - 2)
        oh_r.append((dr[:, :, None] == np.arange(2 * NA_WIN_H - 1)).astype(np.float32))
        ok.append(row_ok[:, None, :, None] & col_ok[None, :, None, :])
    b = jnp.einsum('pajr,lhrc,qkc->lphaqjk', np.stack(oh_r), rpb, oh_c, precision=lax.Precision.HIGHEST)
    b = jnp.where(np.stack(ok)[None, :, None], b, NEG)
    return b.reshape(rpb.shape[0], 3, rpb.shape[1], NA_QBLK, NA_KBLK)


def _neighbourhood_attention(p_na, bias, n_ctx):
    tt = p_na.shape[0]
    n_qb = tt // NA_QBLK
    n_lat_blk = n_qb - 1
    n_band_starts = n_lat_blk - NA_BAND // NA_QROWS + 1
    hp = NA_HEADS // 2

    def bias_idx(h, qb):
        pat = jnp.where(qb <= 1, 0, jnp.where(qb == n_qb - 1, 2, 1))
        return (pat, h, 0, 0)

    return pl.pallas_call(
        functools.partial(_na_kernel, n_ctx=n_ctx, n_band_starts=n_band_starts),
        grid=(hp, n_qb),
        in_specs=[pl.BlockSpec((NA_QBLK, 128), lambda h, qb: (qb, h)),
                  pl.BlockSpec((tt, 128), lambda h, qb: (0, hp + h)),
                  pl.BlockSpec((tt, 128), lambda h, qb: (0, 2 * hp + h)),
                  pl.BlockSpec((1, 2, NA_QBLK, NA_KBLK), bias_idx)],
        out_specs=pl.BlockSpec((NA_QBLK, 128), lambda h, qb: (qb, h)),
        out_shape=jax.ShapeDtypeStruct((tt, NA_W), BF16),
        compiler_params=_cparams(("arbitrary", "arbitrary")),
        name="neighbourhood_attention",
    )(p_na, p_na, p_na, bias)


def _log_sigmoid(x):
    return jnp.minimum(x, 0.0) - jnp.log(1.0 + jnp.exp(-jnp.abs(x)))


def _gla_chunk(q, k, v, lr, wg, bg, tri, causal, s_ref, last_row):
    x = jnp.dot(lr, wg, preferred_element_type=F32, precision=lax.Precision.HIGHEST) + bg
    la = _log_sigmoid(x) * (1.0 / GLA_GATE_NORM)
    b = jnp.dot(tri, la, preferred_element_type=F32, precision=lax.Precision.HIGHEST)
    b_last = b[last_row:last_row + 1, :]
    q_t = (q * (GLA_DK ** -0.5) * jnp.exp(b)).astype(BF16)
    k_t = (k * jnp.exp(-b)).astype(BF16)
    k_end = (k * jnp.exp(b_last - b)).astype(BF16)
    dec = jnp.exp(b_last)
    outs = []
    for h in range(GLA_HEADS):
        ks = slice(h * GLA_DK, (h + 1) * GLA_DK)
        vh = v[:, h * GLA_DV:(h + 1) * GLA_DV].astype(BF16)
        att = jnp.where(causal, _dot_nt(q_t[:, ks], k_t[:, ks]), 0.0)
        st = s_ref[h]
        o = _dot(att.astype(BF16), vh) + _dot_nt(q_t[:, ks], st.astype(BF16))
        s_ref[h] = st * dec[:, ks] + _dot_tn(vh, k_end[:, ks])
        outs.append(o)
    return jnp.concatenate(outs, axis=-1)


def _gla_kernel(qf_ref, kf_ref, vf_ref, lf_ref, qb_ref, kb_ref, vb_ref, lb_ref, wg_ref, bg_ref,
                of_ref, ob_ref, sf_ref, sb_ref):
    @pl.when(pl.program_id(0) == 0)
    def _():
        sf_ref[...] = jnp.zeros_like(sf_ref)
        sb_ref[...] = jnp.zeros_like(sb_ref)

    r = lax.broadcasted_iota(jnp.int32, (GLA_CHUNK, GLA_CHUNK), 0)
    c = lax.broadcasted_iota(jnp.int32, (GLA_CHUNK, GLA_CHUNK), 1)
    lower = r >= c
    upper = r <= c
    tri_f = lower.astype(F32)
    tri_b = upper.astype(F32)
    n_chunks = GLA_BLK // GLA_CHUNK
    for ci in range(n_chunks):
        rows = slice(ci * GLA_CHUNK, (ci + 1) * GLA_CHUNK)
        of_ref[rows, :] = _gla_chunk(qf_ref[rows, :], kf_ref[rows, :], vf_ref[rows, :], lf_ref[rows, :],
                                     wg_ref[0], bg_ref[0], tri_f, lower, sf_ref, GLA_CHUNK - 1)
        cb = n_chunks - 1 - ci
        rows = slice(cb * GLA_CHUNK, (cb + 1) * GLA_CHUNK)
        ob_ref[rows, :] = _gla_chunk(qb_ref[rows, :], kb_ref[rows, :], vb_ref[rows, :], lb_ref[rows, :],
                                     wg_ref[1], bg_ref[1], tri_b, upper, sb_ref, 0)


def _gla(p_gla, wg, bg):
    tt = p_gla.shape[0]
    nb = tt // GLA_BLK
    fwd = lambda i: i
    bwd = lambda i: jnp.where(i == 0, 0, nb - i)
    lr_blk = (2 * GLA_KW + 2 * GLA_VW) // 128

    def specs(rowmap):
        return [pl.BlockSpec((GLA_BLK, GLA_KW), lambda i: (rowmap(i), 0)),
                pl.BlockSpec((GLA_BLK, GLA_KW), lambda i: (rowmap(i), 1)),
                pl.BlockSpec((GLA_BLK, GLA_VW), lambda i: (rowmap(i), 1)),
                pl.BlockSpec((GLA_BLK, 128), lambda i: (rowmap(i), lr_blk))]

    return pl.pallas_call(
        _gla_kernel,
        grid=(nb,),
        in_specs=specs(fwd) + specs(bwd) + [
            pl.BlockSpec((2, 128, GLA_KW), lambda i: (0, 0, 0)),
            pl.BlockSpec((2, 1, GLA_KW), lambda i: (0, 0, 0))],
        out_specs=[pl.BlockSpec((GLA_BLK, GLA_VW), lambda i: (fwd(i), 0)),
                   pl.BlockSpec((GLA_BLK, GLA_VW), lambda i: (bwd(i), 0))],
        out_shape=[jax.ShapeDtypeStruct((tt, GLA_VW), F32)] * 2,
        scratch_shapes=[pltpu.VMEM((GLA_HEADS, GLA_DV, GLA_DK), F32)] * 2,
        compiler_params=_cparams(("arbitrary",)),
        name="gla_bidirectional",
    )(*([p_gla] * 8), wg, bg)


def _group_sum_sq(x, ones_bd):
    sq = x * x
    hi = sq.astype(BF16)
    lo = (sq - hi.astype(F32)).astype(BF16)
    return _dot(hi, ones_bd) + _dot(lo, ones_bd)


def _gqa_prep_kernel(p_ref, ta_ref, tb_ref, ones_ref, qt_ref, kz_ref, vt_ref):
    d = GQA_HEAD_DIM
    x = p_ref[...]
    q, qs = x[:, 0:GQA_QW], x[:, GQA_QW:2 * GQA_QW]
    o = 2 * GQA_QW
    k, ks, v = x[:, o:o + GQA_KVW], x[:, o + GQA_KVW:o + 2 * GQA_KVW], x[:, o + 2 * GQA_KVW:o + 3 * GQA_KVW]
    ta, tb = ta_ref[...], tb_ref[...]
    taq = jnp.tile(ta[:, 0:128], (1, GQA_QW // 128))
    tbq = jnp.tile(tb[:, 0:128], (1, GQA_QW // 128))
    rq = lax.rsqrt(_group_sum_sq(q, ones_ref[...]) * (1.0 / d) + EPS)
    qr = rq * (q * taq + qs * tbq)
    rk = lax.rsqrt(_group_sum_sq(k, ones_ref[0:GQA_KVW, 0:GQA_KVW]) * (1.0 / d) + EPS)
    kr = rk * (k * ta[:, 128:256] + ks * tb[:, 128:256])
    qt_ref[...] = qr.T.astype(BF16)
    lane = lax.broadcasted_iota(jnp.int32, (1, GQA_KVW), 1)
    for g in range(GQA_KV_HEADS):
        kz_ref[g] = jnp.where((lane // d) == g, kr, 0.0).astype(BF16)
    vt_ref[...] = v.T.astype(BF16)


def _gqa_prep(p_gqa, ta, tb):
    tt, n = p_gqa.shape
    tm = _pick(tt, (384, 256, 128))
    hid = np.arange(GQA_QW) // GQA_HEAD_DIM
    ones_bd = jnp.asarray(hid[:, None] == hid[None, :], BF16)
    return pl.pallas_call(
        _gqa_prep_kernel,
        grid=(tt // tm,),
        in_specs=[pl.BlockSpec((tm, n), lambda i: (i, 0)),
                  pl.BlockSpec((tm, 256), lambda i: (i, 0)),
                  pl.BlockSpec((tm, 256), lambda i: (i, 0)),
                  pl.BlockSpec((GQA_QW, GQA_QW), lambda i: (0, 0))],
        out_specs=[pl.BlockSpec((GQA_QW, tm), lambda i: (0, i)),
                   pl.BlockSpec((GQA_KV_HEADS, tm, GQA_KVW), lambda i: (0, i, 0)),
                   pl.BlockSpec((GQA_KVW, tm), lambda i: (0, i))],
        out_shape=[jax.ShapeDtypeStruct((GQA_QW, tt), BF16),
                   jax.ShapeDtypeStruct((GQA_KV_HEADS, tt, GQA_KVW), BF16),
                   jax.ShapeDtypeStruct((GQA_KVW, tt), BF16)],
        compiler_params=_cparams(("arbitrary",)),
        name="gqa_prep",
    )(p_gqa, ta, tb, ones_bd)


def _gqa_kernel(qt_ref, kz_ref, vt_ref, o_ref, s0_ref, s1_ref, acc_ref, *, n_ctx):
    qb = pl.program_id(1)
    d = GQA_HEAD_DIM
    group = GQA_HEADS // GQA_KV_HEADS
    kc = GQA_KEY_CHUNK
    tq = qt_ref.shape[1]
    q4 = jnp.concatenate([jnp.concatenate([qt_ref[g * d:(g + 1) * d, :]] * 2, axis=0) for g in range(group)],
                         axis=1)

    def scores(c, s_ref):
        start = pl.multiple_of(c * kc, kc)
        s = _dot(kz_ref[0, pl.ds(start, kc), :], q4)
        s_ref[...] = s
        return jnp.max(s, axis=0, keepdims=True)

    def accumulate(c, s_ref, m_chunk, m):
        m_new = jnp.maximum(m, m_chunk)
        p = jnp.exp2(s_ref[...] - m_new)
        acc_ref[...] = acc_ref[...] * jnp.exp2(m - m_new) + _dot(vt_ref[0, c], p.astype(BF16))
        return m_new

    def attend(n_chunks):
        acc_ref[...] = jnp.zeros_like(acc_ref)
        m = jnp.full((1, group * tq), -jnp.inf, F32)
        mc0 = scores(0, s0_ref)

        def body(j, carry):
            m, mc0 = carry
            mc1 = scores(2 * j + 1, s1_ref)
            m = accumulate(2 * j, s0_ref, mc0, m)
            mc0 = scores(2 * j + 2, s0_ref)
            m = accumulate(2 * j + 1, s1_ref, mc1, m)
            return m, mc0

        m, mc0 = lax.fori_loop(0, (n_chunks - 1) // 2, body, (m, mc0))
        accumulate(n_chunks - 1, s0_ref, mc0, m)
        out = acc_ref[0:d, :] / acc_ref[d:d + 1, :]
        o_ref[...] = jnp.concatenate([out[:, g * tq:(g + 1) * tq] for g in range(group)],
                                     axis=0).T.astype(o_ref.dtype)

    @pl.when(qb == 0)
    def _():
        attend(n_ctx // kc)

    @pl.when(qb > 0)
    def _():
        attend(kz_ref.shape[1] // kc)


def _gqa_attention(qt, kz, vt, n_ctx):
    tt = qt.shape[1]
    tq = n_ctx
    kc = GQA_KEY_CHUNK
    group = GQA_HEADS // GQA_KV_HEADS
    gw = group * GQA_HEAD_DIM
    assert tt % kc == 0 and (tt // kc) % 2 == 1 and n_ctx % kc == 0 and (n_ctx // kc) % 2 == 1
    vt3 = vt.reshape(GQA_KV_HEADS, GQA_HEAD_DIM, tt // kc, kc).transpose(0, 2, 1, 3)
    extra = jnp.zeros((GQA_KV_HEADS, tt // kc, GQA_ONES_ROWS, kc), BF16).at[:, :, 0, :].set(1.0)
    vt3 = jnp.concatenate([vt3, extra], axis=2)
    vrows = GQA_HEAD_DIM + GQA_ONES_ROWS
    return pl.pallas_call(
        functools.partial(_gqa_kernel, n_ctx=n_ctx),
        grid=(GQA_KV_HEADS, tt // tq),
        in_specs=[pl.BlockSpec((gw, tq), lambda g, i: (g, i)),
                  pl.BlockSpec((1, tt, GQA_KVW), lambda g, i: (g, 0, 0)),
                  pl.BlockSpec((1, tt // kc, vrows, kc), lambda g, i: (g, 0, 0, 0))],
        out_specs=pl.BlockSpec((tq, gw), lambda g, i: (i, g)),
        out_shape=jax.ShapeDtypeStruct((tt, GQA_QW), BF16),
        scratch_shapes=[pltpu.VMEM((kc, group * tq), F32), pltpu.VMEM((kc, group * tq), F32),
                        pltpu.VMEM((vrows, group * tq), F32)],
        compiler_params=_cparams(("arbitrary", "arbitrary")),
        name="gqa_attention",
    )(qt, kz, vt3)


def _mix_out_kernel(x_ref, mod_ref, yna_ref, of_ref, ob_ref, gate_ref, ygqa_ref, w_ref, nw_ref,
                    g_ref, b_ref, x1_ref, h2t_ref, *, n_ctx, tm):
    is_ctx = _row_is_ctx(pl.program_id(0), tm, n_ctx)
    o = of_ref[...] + ob_ref[...]
    nw = nw_ref[...]
    parts = []
    for h in range(GLA_HEADS):
        oh = o[:, h * GLA_DV:(h + 1) * GLA_DV]
        parts.append(oh * lax.rsqrt(jnp.mean(oh * oh, axis=-1, keepdims=True) + EPS) * nw)
    gate = gate_ref[...]
    y_gla = (jnp.concatenate(parts, axis=-1) * (gate / (1.0 + jnp.exp(-gate)))).astype(BF16)
    y = (_dot(yna_ref[...], w_ref[0:NA_W, :]) + _dot(y_gla, w_ref[NA_W:NA_W + GLA_VW, :])
         + _dot(ygqa_ref[...], w_ref[NA_W + GLA_VW:, :]))
    x1 = _layer_norm(DEEPNORM_ALPHA * x_ref[...] + _mod_row(mod_ref, 2, is_ctx) * y) * g_ref[...] + b_ref[...]
    x1_ref[...] = x1
    h2 = _layer_norm(x1) * (1.0 + _mod_row(mod_ref, 4, is_ctx)) + _mod_row(mod_ref, 3, is_ctx)
    h2t_ref[...] = h2.T.astype(BF16)


def _mix_out(xa, mod, y_na, o_f, o_b, p_gla, y_gqa, w_out, norm_w, ln_g, ln_b, n_ctx):
    tt, d = xa.shape
    tm = _pick(tt, (384, 256, 128))
    row = lambda i: (i, 0)
    const2 = lambda i: (0, 0)
    return pl.pallas_call(
        functools.partial(_mix_out_kernel, n_ctx=n_ctx, tm=tm),
        grid=(tt // tm,),
        in_specs=[pl.BlockSpec((tm, d), row),
                  pl.BlockSpec((2, 6, d), lambda i: (0, 0, 0)),
                  pl.BlockSpec((tm, NA_W), row),
                  pl.BlockSpec((tm, GLA_VW), row),
                  pl.BlockSpec((tm, GLA_VW), row),
                  pl.BlockSpec((tm, GLA_VW), lambda i: (i, 2)),
                  pl.BlockSpec((tm, GQA_QW), row),
                  pl.BlockSpec(w_out.shape, const2),
                  pl.BlockSpec((1, GLA_DV), const2),
                  pl.BlockSpec((1, d), const2),
                  pl.BlockSpec((1, d), const2)],
        out_specs=[pl.BlockSpec((tm, d), row), pl.BlockSpec((d, tm), lambda i: (0, i))],
        out_shape=[jax.ShapeDtypeStruct((tt, d), F32), jax.ShapeDtypeStruct((d, tt), BF16)],
        compiler_params=_cparams(("arbitrary",)),
        name="mixer_out_postln",
    )(xa, mod, y_na, o_f, o_b, p_gla, y_gqa, w_out, norm_w, ln_g, ln_b)


def _top16(s, key_iota, exact):
    cur = s
    rank = jnp.full(s.shape, float(PEER_TOPK), F32)
    vals = []
    for r in range(PEER_TOPK):
        m = jnp.max(cur, axis=0, keepdims=True)
        if exact:
            idx = jnp.min(jnp.where(cur == m, key_iota, float(PEER_NKEYS)), axis=0, keepdims=True)
            sel = key_iota == idx
        else:
            sel = cur == m
        rank = jnp.where(sel, float(r), rank)
        cur = jnp.where(sel, -jnp.inf, cur)
        vals.append(m)
    return jnp.concatenate(vals, axis=0), rank


def _cand_layout():
    k = PEER_TOPK
    rows = [(0, rb) for rb in range(k)]
    for ra in range(1, 5):
        rows += [(ra, rb) for rb in range(8)]
    rows += [(ra, 0) for ra in range(8, k)]
    rows += [(ra, 1) for ra in range(8)]
    rows += [(ra, 0) for ra in range(8)]
    seen, valid = set(), []
    for pair in rows:
        valid.append(pair not in seen)
        seen.add(pair)
    needed = {(ra, rb) for ra in range(k) for rb in range(k) if (ra + 1) * (rb + 1) <= k}
    assert needed <= seen
    ra = np.array([p[0] for p in rows], np.float32)
    rb = np.array([p[1] for p in rows], np.float32)
    return ra, rb, np.array(valid)


def _select_pairs(a, b, c_ra, c_flat, c_ok, exact):
    k = PEER_TOPK
    tt = a.shape[1]
    blocks = [a[0:1, :] + b]
    for ra in range(1, 5):
        blocks.append(a[ra:ra + 1, :] + b[0:8, :])
    blocks.append(a[8:k, :] + b[0:1, :])
    blocks.append(a[0:8, :] + b[1:2, :])
    blocks.append(a[0:8, :] + b[0:1, :])
    cand = jnp.where(c_ok, jnp.concatenate(blocks, axis=0), -jnp.inf)
    m0 = a[0:1, :] + b[0:1, :]
    z = jnp.zeros((1, tt), F32)
    if exact:
        rank_iota = lax.broadcasted_iota(jnp.int32, (k, tt), 0).astype(F32)
        cnt = jnp.zeros((k, tt), F32)
        for _ in range(k):
            m = jnp.max(cand, axis=0, keepdims=True)
            idx = jnp.min(jnp.where(cand == m, c_flat, 1e9), axis=0, keepdims=True)
            sel = c_flat == idx
            ra_sel = jnp.sum(jnp.where(sel, c_ra, 0.0), axis=0, keepdims=True)
            cnt = cnt + jnp.where(rank_iota == ra_sel, 1.0, 0.0)
            z = z + jnp.exp(m - m0)
            cand = jnp.where(sel, -jnp.inf, cand)
        return cnt, z, None
    for _ in range(k):
        m = jnp.max(cand, axis=0, keepdims=True)
        z = z + jnp.exp(m - m0)
        cand = jnp.where(cand == m, -jnp.inf, cand)
    sel = jnp.where(jnp.logical_and(c_ok, cand == -jnp.inf), 1.0, 0.0)
    low = sel[56:64, :] + sel[64:72, :]
    row8 = lax.broadcasted_iota(jnp.int32, (8, tt), 0)
    sums = [jnp.sum(sel[0:16, :], axis=0, keepdims=True)]
    sums += [jnp.sum(sel[16 + 8 * i:24 + 8 * i, :], axis=0, keepdims=True) for i in range(4)]
    for r, v in enumerate(sums):
        low = jnp.where(row8 == r, v, low)
    cnt = jnp.concatenate([low, sel[48:56, :]], axis=0)
    return cnt, z, jnp.sum(sel, axis=0, keepdims=True)


def _route_head(s1, s2, key_iota, c_ra, c_flat, c_ok, exact):
    k = PEER_TOPK
    a, r1 = _top16(s1, key_iota, exact)
    b, r2 = _top16(s2, key_iota, exact)
    cnt, z, n_sel = _select_pairs(a, b, c_ra, c_flat, c_ok, exact)
    n1 = jnp.zeros(s1.shape, F32)
    for r in range(k):
        n1 = jnp.where(r1 == float(r), cnt[r:r + 1, :], n1)
    outs = (r2, jnp.exp(s2 - b[0:1, :]) / z, n1, jnp.exp(s1 - a[0:1, :]))
    if exact:
        return outs, None
    in_top = lambda rk: jnp.sum(jnp.where(rk < float(k), 1.0, 0.0), axis=0, keepdims=True)
    clean = jnp.logical_and(n_sel == float(k), jnp.logical_and(in_top(r1) == float(k), in_top(r2) == float(k)))
    return outs, jnp.where(clean, 0.0, 1.0)


def _route_kernel(h_ref, wq_ref, sk_ref, cst_ref, r2_ref, e2_ref, n1_ref, e1_ref):
    tt = h_ref.shape[1]
    key_iota = lax.broadcasted_iota(jnp.int32, (PEER_NKEYS, tt), 0).astype(F32)
    cst = cst_ref[...]
    n_rows = cst.shape[0]
    c_ra = jnp.broadcast_to(cst[:, 0:1], (n_rows, tt))
    c_flat = jnp.broadcast_to(cst[:, 1:2], (n_rows, tt))
    c_ok = jnp.broadcast_to(cst[:, 2:3], (n_rows, tt)) > 0.5
    half = PEER_KEY_DIM // 2

    def run(exact):
        qt = _dot(wq_ref[...], h_ref[...])
        tied = jnp.zeros((1, tt), F32)
        for h in range(PEER_HEADS):
            base = h * PEER_KEY_DIM
            s1 = _dot(sk_ref[h, 0], qt[base:base + half, :].astype(BF16))
            s2 = _dot(sk_ref[h, 1], qt[base + half:base + 2 * half, :].astype(BF16))
            outs, bad = _route_head(s1, s2, key_iota, c_ra, c_flat, c_ok, exact)
            for ref, val in zip((r2_ref, e2_ref, n1_ref, e1_ref), outs):
                ref[h] = val
            if not exact:
                tied = jnp.maximum(tied, bad)
        return tied

    tied = run(exact=False)

    @pl.when(jnp.max(tied) > 0.0)
    def _():
        run(exact=True)


def _peer_route(h2t, wq_t, sub_keys):
    d, tt = h2t.shape
    tb = _pick(tt, (256, 128))
    ra, rb, valid = _cand_layout()
    cst = np.zeros((ra.shape[0], LANES), np.float32)
    flat = np.where(valid, ra * PEER_TOPK + rb, 1000.0 + np.arange(ra.shape[0]))
    cst[:, 0], cst[:, 1], cst[:, 2] = ra, flat, valid
    shp = (PEER_HEADS, PEER_NKEYS, tt)
    out_spec = pl.BlockSpec((PEER_HEADS, PEER_NKEYS, tb), lambda i: (0, 0, i))
    return pl.pallas_call(
        _route_kernel,
        grid=(tt // tb,),
        in_specs=[pl.BlockSpec((d, tb), lambda i: (0, i)),
                  pl.BlockSpec(wq_t.shape, lambda i: (0, 0)),
                  pl.BlockSpec(sub_keys.shape, lambda i: (0, 0, 0, 0)),
                  pl.BlockSpec(cst.shape, lambda i: (0, 0))],
        out_specs=[out_spec] * 4,
        out_shape=[jax.ShapeDtypeStruct(shp, F32)] * 4,
        compiler_params=_cparams(("arbitrary",)),
        name="peer_route",
    )(h2t, wq_t, sub_keys, jnp.asarray(cst))


def _gelu_tanh(x):
    return 0.5 * x * (1.0 + jnp.tanh(math.sqrt(2.0 / math.pi) * (x + 0.044715 * (x * x * x))))


def _peer_kernel(h_ref, u_ref, vt_ref, r2_ref, e2_ref, n1_ref, e1_ref, o_ref):
    e = pl.program_id(1)
    eb = u_ref.shape[0]
    n_i = eb // PEER_NKEYS

    @pl.when(e == 0)
    def _():
        o_ref[...] = jnp.zeros_like(o_ref)

    n_sub = eb // PEER_SUB
    sub_rows = lambda s: slice(s * PEER_SUB, (s + 1) * PEER_SUB)
    pre = _dot(u_ref[sub_rows(0), :], h_ref[...])
    for s in range(n_sub):
        rows = sub_rows(s)
        cur = pre
        if s + 1 < n_sub:
            pre = _dot(u_ref[sub_rows(s + 1), :], h_ref[...])
        act = _gelu_tanh(cur)
        parts = []
        for ii in range(PEER_SUB // PEER_NKEYS):
            i = e * n_i + s * (PEER_SUB // PEER_NKEYS) + ii
            g = jnp.zeros((PEER_NKEYS, act.shape[1]), F32)
            for h in range(PEER_HEADS):
                n1 = n1_ref[h, pl.ds(i, 1), :]
                e1 = e1_ref[h, pl.ds(i, 1), :]
                g = g + jnp.where(r2_ref[h] < n1, e2_ref[h], 0.0) * e1
            parts.append((act[ii * PEER_NKEYS:(ii + 1) * PEER_NKEYS, :] * g).astype(BF16))
        o_ref[...] += _dot(vt_ref[:, rows], jnp.concatenate(parts, axis=0))


def _peer_experts(h2t, u_tab, vt_tab, r2, e2, n1, e1):
    d, tt = h2t.shape
    n_exp = u_tab.shape[0]
    tb = _pick(tt, (768, 512, 256))
    eb = PEER_EXPERT_BLK
    route_spec = pl.BlockSpec((PEER_HEADS, PEER_NKEYS, tb), lambda i, e: (0, 0, i),
                              pipeline_mode=pl.Buffered(1))
    return pl.pallas_call(
        _peer_kernel,
        grid=(tt // tb, n_exp // eb),
        in_specs=[pl.BlockSpec((d, tb), lambda i, e: (0, i)),
                  pl.BlockSpec((eb, d), lambda i, e: (e, 0)),
                  pl.BlockSpec((d, eb), lambda i, e: (0, e)),
                  route_spec, route_spec, route_spec, route_spec],
        out_specs=pl.BlockSpec((d, tb), lambda i, e: (0, i)),
        out_shape=jax.ShapeDtypeStruct((d, tt), F32),
        compiler_params=_cparams(("arbitrary", "arbitrary")),
        name="peer_experts",
    )(h2t, u_tab, vt_tab, r2, e2, n1, e1)


def _peer_out_kernel(x_ref, mod_ref, pt_ref, g_ref, b_ref, o_ref, *, n_ctx, tm):
    is_ctx = _row_is_ctx(pl.program_id(0), tm, n_ctx)
    y = pt_ref[...].T
    o_ref[...] = (_layer_norm(DEEPNORM_ALPHA * x_ref[...] + _mod_row(mod_ref, 5, is_ctx) * y)
                  * g_ref[...] + b_ref[...])


def _peer_out(x1, mod, peer_t, ln_g, ln_b, n_ctx):
    tt, d = x1.shape
    tm = _pick(tt, (384, 256, 128))
    return pl.pallas_call(
        functools.partial(_peer_out_kernel, n_ctx=n_ctx, tm=tm),
        grid=(tt // tm,),
        in_specs=[pl.BlockSpec((tm, d), lambda i: (i, 0)),
                  pl.BlockSpec((2, 6, d), lambda i: (0, 0, 0)),
                  pl.BlockSpec((d, tm), lambda i: (0, i)),
                  pl.BlockSpec((1, d), lambda i: (0, 0)),
                  pl.BlockSpec((1, d), lambda i: (0, 0))],
        out_specs=pl.BlockSpec((tm, d), lambda i: (i, 0)),
        out_shape=jax.ShapeDtypeStruct((tt, d), F32),
        compiler_params=_cparams(("arbitrary",)),
        name="peer_out_postln",
    )(x1, mod, peer_t, ln_g, ln_b)


def _rope_tables(n_ctx, n_lat):
    half = GQA_HEAD_DIM // 2
    inv = ROPE_BASE ** (-jnp.arange(0, half, 2, dtype=F32) / half)
    t = jnp.arange(n_lat)
    row = (t // GRID_W).astype(F32)
    col = (t % GRID_W).astype(F32)
    ang = jnp.concatenate([row[:, None] * inv, col[:, None] * inv], axis=-1)
    cos, sin = jnp.cos(ang), jnp.sin(ang)
    cos2 = jnp.concatenate([jnp.ones((n_ctx, 2 * half), F32), jnp.concatenate([cos, cos], -1)], 0)
    sin2 = jnp.concatenate([jnp.zeros((n_ctx, 2 * half), F32), jnp.concatenate([-sin, sin], -1)], 0)
    return cos2, sin2


_EVEN_ODD = np.concatenate([np.arange(0, GQA_HEAD_DIM, 2), np.arange(1, GQA_HEAD_DIM, 2)])
_ODD_EVEN = np.concatenate([np.arange(1, GQA_HEAD_DIM, 2), np.arange(0, GQA_HEAD_DIM, 2)])


def _head_cols(perm, n_heads):
    return np.concatenate([h * GQA_HEAD_DIM + perm for h in range(n_heads)])


def kernel(x, c, ctx, c_ctx, w_mod, b_mod, w_in, na_rpb, gla_gate_w, gla_gate_b, gla_norm_w,
           gqa_q_norm, gqa_k_norm, w_out, ln1_g, ln1_b, peer_wq, peer_subkeys, peer_u, peer_v,
           ln2_g, ln2_b):
    batch, n_lat, d = x.shape
    assert batch == 1 and d == D_MODEL
    n_ctx = ctx.shape[1]
    assert n_lat % NA_QBLK == 0 and n_lat // GRID_W >= NA_BAND + NA_QROWS
    depth = w_mod.shape[0]

    xa = jnp.concatenate([ctx[0], x[0]], axis=0)
    mods = _modulation(jnp.stack([c_ctx, c[0]]), w_mod, b_mod).reshape(depth, 2, 6, d)

    o = np.cumsum([0, NA_W, NA_W, NA_W, GLA_KW, GLA_KW, GLA_VW, GLA_VW, 2 * GLA_GATE_RANK,
                   GQA_QW, GQA_KVW, GQA_KVW])
    w_na = w_in[:, :, o[0]:o[3]].astype(BF16)
    w_gla = jnp.concatenate([w_in[:, :, o[3]:o[8]],
                             jnp.zeros((depth, d, 128 - 2 * GLA_GATE_RANK), F32)], -1).astype(BF16)
    wq_c, wk_c, wv_c = w_in[:, :, o[8]:o[9]], w_in[:, :, o[9]:o[10]], w_in[:, :, o[10]:o[11]]
    w_gqa = jnp.concatenate([wq_c[:, :, _head_cols(_EVEN_ODD, GQA_HEADS)],
                             wq_c[:, :, _head_cols(_ODD_EVEN, GQA_HEADS)],
                             wk_c[:, :, _head_cols(_EVEN_ODD, GQA_KV_HEADS)],
                             wk_c[:, :, _head_cols(_ODD_EVEN, GQA_KV_HEADS)],
                             wv_c], -1).astype(BF16)
    wg = jnp.zeros((depth, 2, 128, GLA_KW), F32)
    wg = wg.at[:, 0, 0:GLA_GATE_RANK].set(gla_gate_w[:, 0])
    wg = wg.at[:, 1, GLA_GATE_RANK:2 * GLA_GATE_RANK].set(gla_gate_w[:, 1])
    bg = gla_gate_b.reshape(depth, 2, 1, GLA_KW)
    w_out_b = w_out.astype(BF16)
    wq_t = jnp.swapaxes(peer_wq, 1, 2).astype(BF16)
    sub_keys = peer_subkeys.astype(BF16)
    u_tab = peer_u.astype(BF16)
    vt_tab = jnp.swapaxes(peer_v, 1, 2).astype(BF16)

    cos2, sin2 = _rope_tables(n_ctx, n_lat)
    q_scale = GQA_HEAD_DIM ** -0.5 * math.log2(math.e)
    na_bias = _na_bias(na_rpb)

    for l in range(depth):
        mod = mods[l]
        p_na, p_gla, p_gqa = _project(xa, mod, (w_na[l], w_gla[l], w_gqa[l]), (BF16, F32, F32), n_ctx)

        y_na = _neighbourhood_attention(p_na, na_bias[l], n_ctx)
        o_f, o_b = _gla(p_gla, wg[l], bg[l])

        wqn, wkn = gqa_q_norm[l], gqa_k_norm[l]
        ta = jnp.concatenate([jnp.tile(cos2 * wqn[_EVEN_ODD] * q_scale, (1, 2)),
                              jnp.tile(cos2 * wkn[_EVEN_ODD], (1, 2))], -1)
        tb = jnp.concatenate([jnp.tile(sin2 * wqn[_ODD_EVEN] * q_scale, (1, 2)),
                              jnp.tile(sin2 * wkn[_ODD_EVEN], (1, 2))], -1)
        qt, kz, vt = _gqa_prep(p_gqa, ta, tb)
        y_gqa = _gqa_attention(qt, kz, vt, n_ctx)

        x1, h2t = _mix_out(xa, mod, y_na, o_f, o_b, p_gla, y_gqa, w_out_b[l],
                           gla_norm_w[l].reshape(1, GLA_DV), ln1_g[l].reshape(1, d), ln1_b[l].reshape(1, d), n_ctx)
        r2, e2, n1, e1 = _peer_route(h2t, wq_t[l], sub_keys[l])
        peer_t = _peer_experts(h2t, u_tab[l], vt_tab[l], r2, e2, n1, e1)
        xa = _peer_out(x1, mod, peer_t, ln2_g[l].reshape(1, d), ln2_b[l].reshape(1, d), n_ctx)

    return xa[n_ctx:][None]
```

```python
import functools
import math

import numpy as np
import jax
import jax.numpy as jnp
from jax import lax
from jax.experimental import pallas as pl
from jax.experimental.pallas import tpu as pltpu

F32 = jnp.float32
BF16 = jnp.bfloat16

D_MODEL = 2048
DEPTH = 4
GRID_W = 64
EPS = 1e-6

NA_HEADS = 8
NA_HEAD_DIM = 64
NA_WIN_H = 8
NA_WIN_W = 16
GLA_HEADS = 4
GLA_DK = 128
GLA_DV = 256
GLA_GATE_RANK = 16
GLA_GATE_NORM = 16.0
GLA_CHUNK = 64
GQA_HEADS = 8
GQA_KV_HEADS = 2
GQA_HEAD_DIM = 64
ROPE_BASE = 10000.0
PEER_HEADS = 8
PEER_NKEYS = 128
PEER_KEY_DIM = 256
PEER_TOPK = 16

NA_W = NA_HEADS * NA_HEAD_DIM
GLA_KW = GLA_HEADS * GLA_DK
GLA_VW = GLA_HEADS * GLA_DV
GQA_QW = GQA_HEADS * GQA_HEAD_DIM
GQA_KVW = GQA_KV_HEADS * GQA_HEAD_DIM
DEEPNORM_ALPHA = (2.0 * DEPTH) ** 0.25

LANES = 128
BF16_SUBLANES = 16
VMEM_LIMIT_BYTES = 56 * 1024 * 1024

NA_QROWS = 4
NA_BAND = NA_QROWS + NA_WIN_H
NA_QBLK = NA_QROWS * GRID_W
NA_KBLK = NA_BAND * GRID_W
GLA_BLK = 256
GQA_KEY_CHUNK = 256
GQA_LOOP_CHUNKS = 8
GQA_ONES_ROWS = 16
PEER_EXPERT_BLK = 1024
PEER_SUB = 512
NEG = -1e30


def _cparams(sem):
    return pltpu.CompilerParams(dimension_semantics=sem, vmem_limit_bytes=VMEM_LIMIT_BYTES)


def _pick(n, cands):
    for c in cands:
        if n % c == 0:
            return c
    raise ValueError(f"no block size in {cands} divides {n}")


def _dot(a, b):
    return jnp.dot(a, b, preferred_element_type=F32)


def _dot_nt(a, b):
    return lax.dot_general(a, b, (((1,), (1,)), ((), ())), preferred_element_type=F32)


def _dot_tn(a, b):
    return lax.dot_general(a, b, (((0,), (0,)), ((), ())), preferred_element_type=F32)


def _layer_norm(x):
    mu = jnp.mean(x, axis=-1, keepdims=True)
    xc = x - mu
    var = jnp.mean(xc * xc, axis=-1, keepdims=True)
    return xc * lax.rsqrt(var + EPS)


def _row_is_ctx(block_idx, tm, n_ctx):
    rows = block_idx * tm + lax.broadcasted_iota(jnp.int32, (tm, 1), 0)
    return rows < n_ctx


def _mod_row(mod_ref, idx, is_ctx):
    return jnp.where(is_ctx, mod_ref[0, idx:idx + 1, :], mod_ref[1, idx:idx + 1, :])


def _mod_kernel(c_ref, w_ref, b_ref, o_ref):
    w = w_ref[0]
    reps = w.shape[1] // LANES
    for m in range(2):
        cb = c_ref[m]
        s = cb / (1.0 + jnp.exp(-cb))
        o_ref[0, m:m + 1, :] = jnp.sum(w * jnp.tile(s, (1, reps)), axis=0, keepdims=True) + b_ref[0]


def _modulation(c2, w_mod, b_mod):
    depth, d, n = w_mod.shape
    tn = _pick(n, (512, 256, 128))
    cb = jnp.broadcast_to(c2[:, :, None], (2, d, LANES))
    return pl.pallas_call(
        _mod_kernel,
        grid=(depth, n // tn),
        in_specs=[pl.BlockSpec((2, d, LANES), lambda l, j: (0, 0, 0)),
                  pl.BlockSpec((1, d, tn), lambda l, j: (l, 0, j)),
                  pl.BlockSpec((1, 1, tn), lambda l, j: (l, 0, j))],
        out_specs=pl.BlockSpec((1, 2, tn), lambda l, j: (l, 0, j)),
        out_shape=jax.ShapeDtypeStruct((depth, 2, n), F32),
        compiler_params=_cparams(("arbitrary", "arbitrary")),
        name="modulation",
    )(cb, w_mod, b_mod.reshape(depth, 1, n))


def _proj_kernel(x_ref, mod_ref, *refs, n_ctx, tm):
    is_ctx = _row_is_ctx(pl.program_id(0), tm, n_ctx)
    xn = _layer_norm(x_ref[...])
    h = (xn * (1.0 + _mod_row(mod_ref, 1, is_ctx)) + _mod_row(mod_ref, 0, is_ctx)).astype(BF16)
    n_groups = len(refs) // 2
    for w_ref, o_ref in zip(refs[:n_groups], refs[n_groups:]):
        o_ref[...] = _dot(h, w_ref[...]).astype(o_ref.dtype)


def _project(xa, mod, weights, out_dtypes, n_ctx):
    tt, d = xa.shape
    tm = _pick(tt, (256, 128))
    w_specs = [pl.BlockSpec(w.shape, lambda i: (0, 0), pipeline_mode=pl.Buffered(1)) for w in weights]
    return pl.pallas_call(
        functools.partial(_proj_kernel, n_ctx=n_ctx, tm=tm),
        grid=(tt // tm,),
        in_specs=[pl.BlockSpec((tm, d), lambda i: (i, 0)),
                  pl.BlockSpec((2, 6, d), lambda i: (0, 0, 0))] + w_specs,
        out_specs=[pl.BlockSpec((tm, w.shape[1]), lambda i: (i, 0)) for w in weights],
        out_shape=[jax.ShapeDtypeStruct((tt, w.shape[1]), dt) for w, dt in zip(weights, out_dtypes)],
        compiler_params=_cparams(("arbitrary",)),
        name="ln_mod_project",
    )(xa, mod, *weights)


def _na_kernel(q_ref, k_ref, v_ref, bias_ref, o_ref, *, n_ctx, n_band_starts):
    qb = pl.program_id(1)
    scale = NA_HEAD_DIM ** -0.5
    lane = lax.broadcasted_iota(jnp.int32, (1, 2 * NA_HEAD_DIM), 1)
    q = q_ref[...]
    kc = k_ref[0:n_ctx, :]
    vc = v_ref[0:n_ctx, :]

    def head_q(h):
        return jnp.where((lane // NA_HEAD_DIM) == h, q, jnp.zeros_like(q))

    @pl.when(qb == 0)
    def _():
        outs = []
        for h in range(2):
            s = _dot_nt(head_q(h), kc) * scale
            m = jnp.max(s, axis=-1, keepdims=True)
            p = jnp.exp(s - m)
            l = jnp.sum(p, axis=-1, keepdims=True)
            outs.append(_dot(p.astype(BF16), vc) / l)
        o_ref[...] = jnp.where((lane // NA_HEAD_DIM) == 0, outs[0], outs[1]).astype(o_ref.dtype)

    @pl.when(qb > 0)
    def _():
        start_blk = jnp.clip(qb - 2, 0, n_band_starts - 1)
        start = pl.multiple_of(n_ctx + start_blk * NA_QBLK, NA_QBLK)
        kb = k_ref[pl.ds(start, NA_KBLK), :]
        vb = v_ref[pl.ds(start, NA_KBLK), :]
        outs = []
        for h in range(2):
            qh = head_q(h)
            sw = _dot_nt(qh, kb) * scale + bias_ref[0, h]
            sc = _dot_nt(qh, kc) * scale
            m = jnp.maximum(jnp.max(sw, axis=-1, keepdims=True), jnp.max(sc, axis=-1, keepdims=True))
            pw = jnp.exp(sw - m)
            pc = jnp.exp(sc - m)
            l = jnp.sum(pw, axis=-1, keepdims=True) + jnp.sum(pc, axis=-1, keepdims=True)
            outs.append((_dot(pw.astype(BF16), vb) + _dot(pc.astype(BF16), vc)) / l)
        o_ref[...] = jnp.where((lane // NA_HEAD_DIM) == 0, outs[0], outs[1]).astype(o_ref.dtype)


def _na_bias(rpb):
    a = np.arange(NA_QROWS)[:, None]
    j = np.arange(NA_BAND)[None, :]
    ws = [np.zeros_like(a), a, np.full_like(a, NA_BAND - NA_WIN_H)]
    off = [0, -NA_WIN_H // 2, -(NA_QROWS + NA_WIN_H // 2)]
    qc = np.arange(GRID_W)[:, None]
    kc = np.arange(GRID_W)[None, :]
    wcs = np.clip(qc - NA_WIN_W // 2, 0, GRID_W - NA_WIN_W)
    col_ok = (kc >= wcs) & (kc < wcs + NA_WIN_W)
    dc = np.clip(kc - qc + NA_WIN_W - 1, 0, 2 * NA_WIN_W - 2)
    oh_c = (dc[:, :, None] == np.arange(2 * NA_WIN_W - 1)).astype(np.float32)
    oh_r, ok = [], []
    for p in range(3):
        row_ok = (j >= ws[p]) & (j < ws[p] + NA_WIN_H)
        dr = np.clip(j - a + off[p] + NA_WIN_H - 1, 0, 2 * NA_WIN_H - 2)
        oh_r.append((dr[:, :, None] == np.arange(2 * NA_WIN_H - 1)).astype(np.float32))
        ok.append(row_ok[:, None, :, None] & col_ok[None, :, None, :])
    b = jnp.einsum('pajr,lhrc,qkc->lphaqjk', np.stack(oh_r), rpb, oh_c, precision=lax.Precision.HIGHEST)
    b = jnp.where(np.stack(ok)[None, :, None], b, NEG)
    return b.reshape(rpb.shape[0], 3, rpb.shape[1], NA_QBLK, NA_KBLK)


def _neighbourhood_attention(p_na, bias, n_ctx):
    tt = p_na.shape[0]
    n_qb = tt // NA_QBLK
    n_lat_blk = n_qb - 1
    n_band_starts = n_lat_blk - NA_BAND // NA_QROWS + 1
    hp = NA_HEADS // 2

    def bias_idx(h, qb):
        pat = jnp.where(qb <= 1, 0, jnp.where(qb == n_qb - 1, 2, 1))
        return (pat, h, 0, 0)

    return pl.pallas_call(
        functools.partial(_na_kernel, n_ctx=n_ctx, n_band_starts=n_band_starts),
        grid=(hp, n_qb),
        in_specs=[pl.BlockSpec((NA_QBLK, 128), lambda h, qb: (qb, h)),
                  pl.BlockSpec((tt, 128), lambda h, qb: (0, hp + h)),
                  pl.BlockSpec((tt, 128), lambda h, qb: (0, 2 * hp + h)),
                  pl.BlockSpec((1, 2, NA_QBLK, NA_KBLK), bias_idx)],
        out_specs=pl.BlockSpec((NA_QBLK, 128), lambda h, qb: (qb, h)),
        out_shape=jax.ShapeDtypeStruct((tt, NA_W), BF16),
        compiler_params=_cparams(("arbitrary", "arbitrary")),
        name="neighbourhood_attention",
    )(p_na, p_na, p_na, bias)


def _log_sigmoid(x):
    return jnp.minimum(x, 0.0) - jnp.log(1.0 + jnp.exp(-jnp.abs(x)))


def _gla_chunk(q, k, v, lr, wg, bg, tri, causal, s_ref, last_row):
    x = jnp.dot(lr, wg, preferred_element_type=F32, precision=lax.Precision.HIGHEST) + bg
    la = _log_sigmoid(x) * (1.0 / GLA_GATE_NORM)
    b = jnp.dot(tri, la, preferred_element_type=F32, precision=lax.Precision.HIGHEST)
    b_last = b[last_row:last_row + 1, :]
    q_t = (q * (GLA_DK ** -0.5) * jnp.exp(b)).astype(BF16)
    k_t = (k * jnp.exp(-b)).astype(BF16)
    k_end = (k * jnp.exp(b_last - b)).astype(BF16)
    dec = jnp.exp(b_last)
    outs = []
    for h in range(GLA_HEADS):
        ks = slice(h * GLA_DK, (h + 1) * GLA_DK)
        vh = v[:, h * GLA_DV:(h + 1) * GLA_DV].astype(BF16)
        att = jnp.where(causal, _dot_nt(q_t[:, ks], k_t[:, ks]), 0.0)
        st = s_ref[h]
        o = _dot(att.astype(BF16), vh) + _dot_nt(q_t[:, ks], st.astype(BF16))
        s_ref[h] = st * dec[:, ks] + _dot_tn(vh, k_end[:, ks])
        outs.append(o)
    return jnp.concatenate(outs, axis=-1)


def _gla_kernel(qf_ref, kf_ref, vf_ref, lf_ref, qb_ref, kb_ref, vb_ref, lb_ref, wg_ref, bg_ref,
                of_ref, ob_ref, sf_ref, sb_ref):
    @pl.when(pl.program_id(0) == 0)
    def _():
        sf_ref[...] = jnp.zeros_like(sf_ref)
        sb_ref[...] = jnp.zeros_like(sb_ref)

    r = lax.broadcasted_iota(jnp.int32, (GLA_CHUNK, GLA_CHUNK), 0)
    c = lax.broadcasted_iota(jnp.int32, (GLA_CHUNK, GLA_CHUNK), 1)
    lower = r >= c
    upper = r <= c
    tri_f = lower.astype(F32)
    tri_b = upper.astype(F32)
    n_chunks = GLA_BLK // GLA_CHUNK
    for ci in range(n_chunks):
        rows = slice(ci * GLA_CHUNK, (ci + 1) * GLA_CHUNK)
        of_ref[rows, :] = _gla_chunk(qf_ref[rows, :], kf_ref[rows, :], vf_ref[rows, :], lf_ref[rows, :],
                                     wg_ref[0], bg_ref[0], tri_f, lower, sf_ref, GLA_CHUNK - 1)
        cb = n_chunks - 1 - ci
        rows = slice(cb * GLA_CHUNK, (cb + 1) * GLA_CHUNK)
        ob_ref[rows, :] = _gla_chunk(qb_ref[rows, :], kb_ref[rows, :], vb_ref[rows, :], lb_ref[rows, :],
                                     wg_ref[1], bg_ref[1], tri_b, upper, sb_ref, 0)


def _gla(p_gla, wg, bg):
    tt = p_gla.shape[0]
    nb = tt // GLA_BLK
    fwd = lambda i: i
    bwd = lambda i: jnp.where(i == 0, 0, nb - i)
    lr_blk = (2 * GLA_KW + 2 * GLA_VW) // 128

    def specs(rowmap):
        return [pl.BlockSpec((GLA_BLK, GLA_KW), lambda i: (rowmap(i), 0)),
                pl.BlockSpec((GLA_BLK, GLA_KW), lambda i: (rowmap(i), 1)),
                pl.BlockSpec((GLA_BLK, GLA_VW), lambda i: (rowmap(i), 1)),
                pl.BlockSpec((GLA_BLK, 128), lambda i: (rowmap(i), lr_blk))]

    return pl.pallas_call(
        _gla_kernel,
        grid=(nb,),
        in_specs=specs(fwd) + specs(bwd) + [
            pl.BlockSpec((2, 128, GLA_KW), lambda i: (0, 0, 0)),
            pl.BlockSpec((2, 1, GLA_KW), lambda i: (0, 0, 0))],
        out_specs=[pl.BlockSpec((GLA_BLK, GLA_VW), lambda i: (fwd(i), 0)),
                   pl.BlockSpec((GLA_BLK, GLA_VW), lambda i: (bwd(i), 0))],
        out_shape=[jax.ShapeDtypeStruct((tt, GLA_VW), F32)] * 2,
        scratch_shapes=[pltpu.VMEM((GLA_HEADS, GLA_DV, GLA_DK), F32)] * 2,
        compiler_params=_cparams(("arbitrary",)),
        name="gla_bidirectional",
    )(*([p_gla] * 8), wg, bg)


def _group_sum_sq(x, ones_bd):
    sq = x * x
    hi = sq.astype(BF16)
    lo = (sq - hi.astype(F32)).astype(BF16)
    return _dot(hi, ones_bd) + _dot(lo, ones_bd)


def _gqa_prep_kernel(p_ref, ta_ref, tb_ref, ones_ref, qt_ref, kz_ref, vt_ref):
    d = GQA_HEAD_DIM
    x = p_ref[...]
    q, qs = x[:, 0:GQA_QW], x[:, GQA_QW:2 * GQA_QW]
    o = 2 * GQA_QW
    k, ks, v = x[:, o:o + GQA_KVW], x[:, o + GQA_KVW:o + 2 * GQA_KVW], x[:, o + 2 * GQA_KVW:o + 3 * GQA_KVW]
    ta, tb = ta_ref[...], tb_ref[...]
    taq = jnp.tile(ta[:, 0:128], (1, GQA_QW // 128))
    tbq = jnp.tile(tb[:, 0:128], (1, GQA_QW // 128))
    rq = lax.rsqrt(_group_sum_sq(q, ones_ref[...]) * (1.0 / d) + EPS)
    qr = rq * (q * taq + qs * tbq)
    rk = lax.rsqrt(_group_sum_sq(k, ones_ref[0:GQA_KVW, 0:GQA_KVW]) * (1.0 / d) + EPS)
    kr = rk * (k * ta[:, 128:256] + ks * tb[:, 128:256])
    qt_ref[...] = qr.T.astype(BF16)
    lane = lax.broadcasted_iota(jnp.int32, (1, GQA_KVW), 1)
    for g in range(GQA_KV_HEADS):
        kz_ref[g] = jnp.where((lane // d) == g, kr, 0.0).astype(BF16)
    vt_ref[...] = v.T.astype(BF16)


def _gqa_prep(p_gqa, ta, tb):
    tt, n = p_gqa.shape
    tm = _pick(tt, (384, 256, 128))
    hid = np.arange(GQA_QW) // GQA_HEAD_DIM
    ones_bd = jnp.asarray(hid[:, None] == hid[None, :], BF16)
    return pl.pallas_call(
        _gqa_prep_kernel,
        grid=(tt // tm,),
        in_specs=[pl.BlockSpec((tm, n), lambda i: (i, 0)),
                  pl.BlockSpec((tm, 256), lambda i: (i, 0)),
                  pl.BlockSpec((tm, 256), lambda i: (i, 0)),
                  pl.BlockSpec((GQA_QW, GQA_QW), lambda i: (0, 0))],
        out_specs=[pl.BlockSpec((GQA_QW, tm), lambda i: (0, i)),
                   pl.BlockSpec((GQA_KV_HEADS, tm, GQA_KVW), lambda i: (0, i, 0)),
                   pl.BlockSpec((GQA_KVW, tm), lambda i: (0, i))],
        out_shape=[jax.ShapeDtypeStruct((GQA_QW, tt), BF16),
                   jax.ShapeDtypeStruct((GQA_KV_HEADS, tt, GQA_KVW), BF16),
                   jax.ShapeDtypeStruct((GQA_KVW, tt), BF16)],
        compiler_params=_cparams(("arbitrary",)),
        name="gqa_prep",
    )(p_gqa, ta, tb, ones_bd)


def _gqa_kernel(qt_ref, kz_ref, vt_ref, o_ref, s0_ref, s1_ref, acc_ref, *, n_ctx):
    qb = pl.program_id(1)
    d = GQA_HEAD_DIM
    group = GQA_HEADS // GQA_KV_HEADS
    kc = GQA_KEY_CHUNK
    tq = qt_ref.shape[1]
    q4 = jnp.concatenate([jnp.concatenate([qt_ref[g * d:(g + 1) * d, :]] * 2, axis=0) for g in range(group)],
                         axis=1)

    def scores(c, s_ref):
        start = pl.multiple_of(c * kc, kc)
        s = _dot(kz_ref[0, pl.ds(start, kc), :], q4)
        s_ref[...] = s
        return jnp.max(s, axis=0, keepdims=True)

    def accumulate(c, s_ref, m_chunk, m):
        m_new = jnp.maximum(m, m_chunk)
        p = jnp.exp2(s_ref[...] - m_new)
        acc_ref[...] = acc_ref[...] * jnp.exp2(m - m_new) + _dot(vt_ref[0, c], p.astype(BF16))
        return m_new

    def attend(n_chunks):
        acc_ref[...] = jnp.zeros_like(acc_ref)
        m = jnp.full((1, group * tq), -jnp.inf, F32)
        mc0 = scores(0, s0_ref)

        def body(j, carry):
            m, mc0 = carry
            for k in range(0, GQA_LOOP_CHUNKS, 2):
                c = GQA_LOOP_CHUNKS * j + k
                mc1 = scores(c + 1, s1_ref)
                m = accumulate(c, s0_ref, mc0, m)
                mc0 = scores(c + 2, s0_ref)
                m = accumulate(c + 1, s1_ref, mc1, m)
            return m, mc0

        m, mc0 = lax.fori_loop(0, (n_chunks - 1) // GQA_LOOP_CHUNKS, body, (m, mc0))
        accumulate(n_chunks - 1, s0_ref, mc0, m)
        out = acc_ref[0:d, :] / acc_ref[d:d + 1, :]
        o_ref[...] = jnp.concatenate([out[:, g * tq:(g + 1) * tq] for g in range(group)],
                                     axis=0).T.astype(o_ref.dtype)

    @pl.when(qb == 0)
    def _():
        attend(n_ctx // kc)

    @pl.when(qb > 0)
    def _():
        attend(kz_ref.shape[1] // kc)


def _gqa_attention(qt, kz, vt, n_ctx):
    tt = qt.shape[1]
    tq = n_ctx
    kc = GQA_KEY_CHUNK
    group = GQA_HEADS // GQA_KV_HEADS
    gw = group * GQA_HEAD_DIM
    assert tt % kc == 0 and n_ctx % kc == 0
    assert (tt // kc) % GQA_LOOP_CHUNKS == 1 and (n_ctx // kc) % GQA_LOOP_CHUNKS == 1
    vt3 = vt.reshape(GQA_KV_HEADS, GQA_HEAD_DIM, tt // kc, kc).transpose(0, 2, 1, 3)
    extra = jnp.zeros((GQA_KV_HEADS, tt // kc, GQA_ONES_ROWS, kc), BF16).at[:, :, 0, :].set(1.0)
    vt3 = jnp.concatenate([vt3, extra], axis=2)
    vrows = GQA_HEAD_DIM + GQA_ONES_ROWS
    return pl.pallas_call(
        functools.partial(_gqa_kernel, n_ctx=n_ctx),
        grid=(GQA_KV_HEADS, tt // tq),
        in_specs=[pl.BlockSpec((gw, tq), lambda g, i: (g, i)),
                  pl.BlockSpec((1, tt, GQA_KVW), lambda g, i: (g, 0, 0)),
                  pl.BlockSpec((1, tt // kc, vrows, kc), lambda g, i: (g, 0, 0, 0))],
        out_specs=pl.BlockSpec((tq, gw), lambda g, i: (i, g)),
        out_shape=jax.ShapeDtypeStruct((tt, GQA_QW), BF16),
        scratch_shapes=[pltpu.VMEM((kc, group * tq), F32), pltpu.VMEM((kc, group * tq), F32),
                        pltpu.VMEM((vrows, group * tq), F32)],
        compiler_params=_cparams(("arbitrary", "arbitrary")),
        name="gqa_attention",
    )(qt, kz, vt3)


def _mix_out_kernel(x_ref, mod_ref, yna_ref, of_ref, ob_ref, gate_ref, ygqa_ref, w_ref, nw_ref,
                    g_ref, b_ref, x1_ref, h2t_ref, *, n_ctx, tm):
    is_ctx = _row_is_ctx(pl.program_id(0), tm, n_ctx)
    o = of_ref[...] + ob_ref[...]
    nw = nw_ref[...]
    parts = []
    for h in range(GLA_HEADS):
        oh = o[:, h * GLA_DV:(h + 1) * GLA_DV]
        parts.append(oh * lax.rsqrt(jnp.mean(oh * oh, axis=-1, keepdims=True) + EPS) * nw)
    gate = gate_ref[...]
    y_gla = (jnp.concatenate(parts, axis=-1) * (gate / (1.0 + jnp.exp(-gate)))).astype(BF16)
    y = (_dot(yna_ref[...], w_ref[0:NA_W, :]) + _dot(y_gla, w_ref[NA_W:NA_W + GLA_VW, :])
         + _dot(ygqa_ref[...], w_ref[NA_W + GLA_VW:, :]))
    x1 = _layer_norm(DEEPNORM_ALPHA * x_ref[...] + _mod_row(mod_ref, 2, is_ctx) * y) * g_ref[...] + b_ref[...]
    x1_ref[...] = x1
    h2 = _layer_norm(x1) * (1.0 + _mod_row(mod_ref, 4, is_ctx)) + _mod_row(mod_ref, 3, is_ctx)
    h2t_ref[...] = h2.T.astype(BF16)


def _mix_out(xa, mod, y_na, o_f, o_b, p_gla, y_gqa, w_out, norm_w, ln_g, ln_b, n_ctx):
    tt, d = xa.shape
    tm = _pick(tt, (384, 256, 128))
    row = lambda i: (i, 0)
    const2 = lambda i: (0, 0)
    return pl.pallas_call(
        functools.partial(_mix_out_kernel, n_ctx=n_ctx, tm=tm),
        grid=(tt // tm,),
        in_specs=[pl.BlockSpec((tm, d), row),
                  pl.BlockSpec((2, 6, d), lambda i: (0, 0, 0)),
                  pl.BlockSpec((tm, NA_W), row),
                  pl.BlockSpec((tm, GLA_VW), row),
                  pl.BlockSpec((tm, GLA_VW), row),
                  pl.BlockSpec((tm, GLA_VW), lambda i: (i, 2)),
                  pl.BlockSpec((tm, GQA_QW), row),
                  pl.BlockSpec(w_out.shape, const2),
                  pl.BlockSpec((1, GLA_DV), const2),
                  pl.BlockSpec((1, d), const2),
                  pl.BlockSpec((1, d), const2)],
        out_specs=[pl.BlockSpec((tm, d), row), pl.BlockSpec((d, tm), lambda i: (0, i))],
        out_shape=[jax.ShapeDtypeStruct((tt, d), F32), jax.ShapeDtypeStruct((d, tt), BF16)],
        compiler_params=_cparams(("arbitrary",)),
        name="mixer_out_postln",
    )(xa, mod, y_na, o_f, o_b, p_gla, y_gqa, w_out, norm_w, ln_g, ln_b)


def _top16(s, key_iota, exact):
    cur = s
    rank = jnp.full(s.shape, float(PEER_TOPK), F32)
    vals = []
    for r in range(PEER_TOPK):
        m = jnp.max(cur, axis=0, keepdims=True)
        if exact:
            idx = jnp.min(jnp.where(cur == m, key_iota, float(PEER_NKEYS)), axis=0, keepdims=True)
            sel = key_iota == idx
        else:
            sel = cur == m
        rank = jnp.where(sel, float(r), rank)
        cur = jnp.where(sel, -jnp.inf, cur)
        vals.append(m)
    return jnp.concatenate(vals, axis=0), rank


def _cand_layout():
    k = PEER_TOPK
    rows = [(0, rb) for rb in range(k)]
    for ra in range(1, 5):
        rows += [(ra, rb) for rb in range(8)]
    rows += [(ra, 0) for ra in range(8, k)]
    rows += [(ra, 1) for ra in range(8)]
    rows += [(ra, 0) for ra in range(8)]
    seen, valid = set(), []
    for pair in rows:
        valid.append(pair not in seen)
        seen.add(pair)
    needed = {(ra, rb) for ra in range(k) for rb in range(k) if (ra + 1) * (rb + 1) <= k}
    assert needed <= seen
    ra = np.array([p[0] for p in rows], np.float32)
    rb = np.array([p[1] for p in rows], np.float32)
    return ra, rb, np.array(valid)


def _select_pairs(a, b, c_ra, c_flat, c_ok, exact):
    k = PEER_TOPK
    tt = a.shape[1]
    blocks = [a[0:1, :] + b]
    for ra in range(1, 5):
        blocks.append(a[ra:ra + 1, :] + b[0:8, :])
    blocks.append(a[8:k, :] + b[0:1, :])
    blocks.append(a[0:8, :] + b[1:2, :])
    blocks.append(a[0:8, :] + b[0:1, :])
    cand = jnp.where(c_ok, jnp.concatenate(blocks, axis=0), -jnp.inf)
    m0 = a[0:1, :] + b[0:1, :]
    z = jnp.zeros((1, tt), F32)
    if exact:
        rank_iota = lax.broadcasted_iota(jnp.int32, (k, tt), 0).astype(F32)
        cnt = jnp.zeros((k, tt), F32)
        for _ in range(k):
            m = jnp.max(cand, axis=0, keepdims=True)
            idx = jnp.min(jnp.where(cand == m, c_flat, 1e9), axis=0, keepdims=True)
            sel = c_flat == idx
            ra_sel = jnp.sum(jnp.where(sel, c_ra, 0.0), axis=0, keepdims=True)
            cnt = cnt + jnp.where(rank_iota == ra_sel, 1.0, 0.0)
            z = z + jnp.exp(m - m0)
            cand = jnp.where(sel, -jnp.inf, cand)
        return cnt, z, None
    for _ in range(k):
        m = jnp.max(cand, axis=0, keepdims=True)
        z = z + jnp.exp(m - m0)
        cand = jnp.where(cand == m, -jnp.inf, cand)
    sel = jnp.where(jnp.logical_and(c_ok, cand == -jnp.inf), 1.0, 0.0)
    low = sel[56:64, :] + sel[64:72, :]
    row8 = lax.broadcasted_iota(jnp.int32, (8, tt), 0)
    sums = [jnp.sum(sel[0:16, :], axis=0, keepdims=True)]
    sums += [jnp.sum(sel[16 + 8 * i:24 + 8 * i, :], axis=0, keepdims=True) for i in range(4)]
    for r, v in enumerate(sums):
        low = jnp.where(row8 == r, v, low)
    cnt = jnp.concatenate([low, sel[48:56, :]], axis=0)
    return cnt, z, jnp.sum(sel, axis=0, keepdims=True)


def _route_head(s1, s2, key_iota, c_ra, c_flat, c_ok, exact):
    k = PEER_TOPK
    a, r1 = _top16(s1, key_iota, exact)
    b, r2 = _top16(s2, key_iota, exact)
    cnt, z, n_sel = _select_pairs(a, b, c_ra, c_flat, c_ok, exact)
    n1 = jnp.zeros(s1.shape, F32)
    for r in range(k):
        n1 = jnp.where(r1 == float(r), cnt[r:r + 1, :], n1)
    outs = (r2, jnp.exp(s2 - b[0:1, :]) / z, n1, jnp.exp(s1 - a[0:1, :]))
    if exact:
        return outs, None
    in_top = lambda rk: jnp.sum(jnp.where(rk < float(k), 1.0, 0.0), axis=0, keepdims=True)
    clean = jnp.logical_and(n_sel == float(k), jnp.logical_and(in_top(r1) == float(k), in_top(r2) == float(k)))
    return outs, jnp.where(clean, 0.0, 1.0)


def _route_kernel(h_ref, wq_ref, sk_ref, cst_ref, r2_ref, e2_ref, n1_ref, e1_ref):
    tt = h_ref.shape[1]
    key_iota = lax.broadcasted_iota(jnp.int32, (PEER_NKEYS, tt), 0).astype(F32)
    cst = cst_ref[...]
    n_rows = cst.shape[0]
    c_ra = jnp.broadcast_to(cst[:, 0:1], (n_rows, tt))
    c_flat = jnp.broadcast_to(cst[:, 1:2], (n_rows, tt))
    c_ok = jnp.broadcast_to(cst[:, 2:3], (n_rows, tt)) > 0.5
    half = PEER_KEY_DIM // 2

    def head(h, qt_h, exact):
        s1 = _dot(sk_ref[h, 0], qt_h[0:half, :].astype(BF16))
        s2 = _dot(sk_ref[h, 1], qt_h[half:2 * half, :].astype(BF16))
        outs, bad = _route_head(s1, s2, key_iota, c_ra, c_flat, c_ok, exact)
        for ref, val in zip((r2_ref, e2_ref, n1_ref, e1_ref), outs):
            ref[h] = val.astype(ref.dtype)
        return bad

    qt = _dot(wq_ref[...], h_ref[...])
    tied = [jnp.max(head(h, qt[h * PEER_KEY_DIM:(h + 1) * PEER_KEY_DIM, :], False)) for h in range(PEER_HEADS)]
    for h in range(PEER_HEADS):
        @pl.when(tied[h] > 0.0)
        def _(h=h):
            head(h, _dot(wq_ref[h * PEER_KEY_DIM:(h + 1) * PEER_KEY_DIM, :], h_ref[...]), True)


def _peer_route(h2t, wq_t, sub_keys):
    d, tt = h2t.shape
    tb = _pick(tt, (256, 128))
    ra, rb, valid = _cand_layout()
    cst = np.zeros((ra.shape[0], LANES), np.float32)
    flat = np.where(valid, ra * PEER_TOPK + rb, 1000.0 + np.arange(ra.shape[0]))
    cst[:, 0], cst[:, 1], cst[:, 2] = ra, flat, valid
    shp = (PEER_HEADS, PEER_NKEYS, tt)
    out_spec = pl.BlockSpec((PEER_HEADS, PEER_NKEYS, tb), lambda i: (0, 0, i))
    return pl.pallas_call(
        _route_kernel,
        grid=(tt // tb,),
        in_specs=[pl.BlockSpec((d, tb), lambda i: (0, i)),
                  pl.BlockSpec(wq_t.shape, lambda i: (0, 0)),
                  pl.BlockSpec(sub_keys.shape, lambda i: (0, 0, 0, 0)),
                  pl.BlockSpec(cst.shape, lambda i: (0, 0))],
        out_specs=[out_spec] * 4,
        out_shape=[jax.ShapeDtypeStruct(shp, dt) for dt in (BF16, BF16, F32, F32)],
        compiler_params=_cparams(("arbitrary",)),
        name="peer_route",
    )(h2t, wq_t, sub_keys, jnp.asarray(cst))


def _gelu_tanh(x):
    c = math.sqrt(2.0 / math.pi)
    hx = 0.5 * x
    return hx * jnp.tanh(x * (c + (c * 0.044715) * (x * x))) + hx


def _peer_kernel(h_ref, u_ref, vt_ref, r2_ref, e2_ref, n1_ref, e1_ref, o_ref):
    e = pl.program_id(1)
    eb = u_ref.shape[0]
    n_i = eb // PEER_NKEYS

    @pl.when(e == 0)
    def _():
        o_ref[...] = jnp.zeros_like(o_ref)

    n_sub = eb // PEER_SUB
    sub_rows = lambda s: slice(s * PEER_SUB, (s + 1) * PEER_SUB)
    pre = _dot(u_ref[sub_rows(0), :], h_ref[...])
    for s in range(n_sub):
        rows = sub_rows(s)
        cur = pre
        if s + 1 < n_sub:
            pre = _dot(u_ref[sub_rows(s + 1), :], h_ref[...])
        act = _gelu_tanh(cur)
        parts = []
        for ii in range(PEER_SUB // PEER_NKEYS):
            i = e * n_i + s * (PEER_SUB // PEER_NKEYS) + ii
            tb = act.shape[1]
            tiles = PEER_NKEYS // BF16_SUBLANES
            g = None
            for h in range(PEER_HEADS):
                n1 = jnp.broadcast_to(n1_ref[h, pl.ds(i, 1), :], (BF16_SUBLANES, tb)).astype(BF16)
                e1 = jnp.broadcast_to(e1_ref[h, pl.ds(i, 1), :], (BF16_SUBLANES, tb)).astype(BF16)
                r2t = r2_ref[h].reshape(tiles, BF16_SUBLANES, tb)
                e2t = e2_ref[h].reshape(tiles, BF16_SUBLANES, tb)
                t = jnp.where(r2t < n1[None], e2t, jnp.zeros_like(e2t)) * e1[None]
                g = t if g is None else g + t
            a16 = act[ii * PEER_NKEYS:(ii + 1) * PEER_NKEYS, :].astype(BF16).reshape(tiles, BF16_SUBLANES, tb)
            parts.append((a16 * g).reshape(PEER_NKEYS, tb))
        o_ref[...] += _dot(vt_ref[:, rows], jnp.concatenate(parts, axis=0))


def _peer_experts(h2t, u_tab, vt_tab, r2, e2, n1, e1):
    d, tt = h2t.shape
    n_exp = u_tab.shape[0]
    tb = _pick(tt, (768, 512, 256))
    eb = PEER_EXPERT_BLK
    route_spec = pl.BlockSpec((PEER_HEADS, PEER_NKEYS, tb), lambda i, e: (0, 0, i),
                              pipeline_mode=pl.Buffered(1))
    return pl.pallas_call(
        _peer_kernel,
        grid=(tt // tb, n_exp // eb),
        in_specs=[pl.BlockSpec((d, tb), lambda i, e: (0, i)),
                  pl.BlockSpec((eb, d), lambda i, e: (e, 0)),
                  pl.BlockSpec((d, eb), lambda i, e: (0, e)),
                  route_spec, route_spec, route_spec, route_spec],
        out_specs=pl.BlockSpec((d, tb), lambda i, e: (0, i)),
        out_shape=jax.ShapeDtypeStruct((d, tt), F32),
        compiler_params=_cparams(("arbitrary", "arbitrary")),
        name="peer_experts",
    )(h2t, u_tab, vt_tab, r2, e2, n1, e1)


def _peer_out_kernel(x_ref, mod_ref, pt_ref, g_ref, b_ref, o_ref, *, n_ctx, tm):
    is_ctx = _row_is_ctx(pl.program_id(0), tm, n_ctx)
    y = pt_ref[...].T
    o_ref[...] = (_layer_norm(DEEPNORM_ALPHA * x_ref[...] + _mod_row(mod_ref, 5, is_ctx) * y)
                  * g_ref[...] + b_ref[...])


def _peer_out(x1, mod, peer_t, ln_g, ln_b, n_ctx):
    tt, d = x1.shape
    tm = _pick(tt, (384, 256, 128))
    return pl.pallas_call(
        functools.partial(_peer_out_kernel, n_ctx=n_ctx, tm=tm),
        grid=(tt // tm,),
        in_specs=[pl.BlockSpec((tm, d), lambda i: (i, 0)),
                  pl.BlockSpec((2, 6, d), lambda i: (0, 0, 0)),
                  pl.BlockSpec((d, tm), lambda i: (0, i)),
                  pl.BlockSpec((1, d), lambda i: (0, 0)),
                  pl.BlockSpec((1, d), lambda i: (0, 0))],
        out_specs=pl.BlockSpec((tm, d), lambda i: (i, 0)),
        out_shape=jax.ShapeDtypeStruct((tt, d), F32),
        compiler_params=_cparams(("arbitrary",)),
        name="peer_out_postln",
    )(x1, mod, peer_t, ln_g, ln_b)


def _rope_tables(n_ctx, n_lat):
    half = GQA_HEAD_DIM // 2
    inv = ROPE_BASE ** (-jnp.arange(0, half, 2, dtype=F32) / half)
    t = jnp.arange(n_lat)
    row = (t // GRID_W).astype(F32)
    col = (t % GRID_W).astype(F32)
    ang = jnp.concatenate([row[:, None] * inv, col[:, None] * inv], axis=-1)
    cos, sin = jnp.cos(ang), jnp.sin(ang)
    cos2 = jnp.concatenate([jnp.ones((n_ctx, 2 * half), F32), jnp.concatenate([cos, cos], -1)], 0)
    sin2 = jnp.concatenate([jnp.zeros((n_ctx, 2 * half), F32), jnp.concatenate([-sin, sin], -1)], 0)
    return cos2, sin2


_EVEN_ODD = np.concatenate([np.arange(0, GQA_HEAD_DIM, 2), np.arange(1, GQA_HEAD_DIM, 2)])
_ODD_EVEN = np.concatenate([np.arange(1, GQA_HEAD_DIM, 2), np.arange(0, GQA_HEAD_DIM, 2)])


def _head_cols(perm, n_heads):
    return np.concatenate([h * GQA_HEAD_DIM + perm for h in range(n_heads)])


def kernel(x, c, ctx, c_ctx, w_mod, b_mod, w_in, na_rpb, gla_gate_w, gla_gate_b, gla_norm_w,
           gqa_q_norm, gqa_k_norm, w_out, ln1_g, ln1_b, peer_wq, peer_subkeys, peer_u, peer_v,
           ln2_g, ln2_b):
    batch, n_lat, d = x.shape
    assert batch == 1 and d == D_MODEL
    n_ctx = ctx.shape[1]
    assert n_lat % NA_QBLK == 0 and n_lat // GRID_W >= NA_BAND + NA_QROWS
    depth = w_mod.shape[0]

    xa = jnp.concatenate([ctx[0], x[0]], axis=0)
    mods = _modulation(jnp.stack([c_ctx, c[0]]), w_mod, b_mod).reshape(depth, 2, 6, d)

    o = np.cumsum([0, NA_W, NA_W, NA_W, GLA_KW, GLA_KW, GLA_VW, GLA_VW, 2 * GLA_GATE_RANK,
                   GQA_QW, GQA_KVW, GQA_KVW])
    w_na = w_in[:, :, o[0]:o[3]].astype(BF16)
    w_gla = jnp.concatenate([w_in[:, :, o[3]:o[8]],
                             jnp.zeros((depth, d, 128 - 2 * GLA_GATE_RANK), F32)], -1).astype(BF16)
    wq_c, wk_c, wv_c = w_in[:, :, o[8]:o[9]], w_in[:, :, o[9]:o[10]], w_in[:, :, o[10]:o[11]]
    w_gqa = jnp.concatenate([wq_c[:, :, _head_cols(_EVEN_ODD, GQA_HEADS)],
                             wq_c[:, :, _head_cols(_ODD_EVEN, GQA_HEADS)],
                             wk_c[:, :, _head_cols(_EVEN_ODD, GQA_KV_HEADS)],
                             wk_c[:, :, _head_cols(_ODD_EVEN, GQA_KV_HEADS)],
                             wv_c], -1).astype(BF16)
    wg = jnp.zeros((depth, 2, 128, GLA_KW), F32)
    wg = wg.at[:, 0, 0:GLA_GATE_RANK].set(gla_gate_w[:, 0])
    wg = wg.at[:, 1, GLA_GATE_RANK:2 * GLA_GATE_RANK].set(gla_gate_w[:, 1])
    bg = gla_gate_b.reshape(depth, 2, 1, GLA_KW)
    w_out_b = w_out.astype(BF16)
    wq_t = jnp.swapaxes(peer_wq, 1, 2).astype(BF16)
    sub_keys = peer_subkeys.astype(BF16)
    u_tab = peer_u.astype(BF16)
    vt_tab = jnp.swapaxes(peer_v, 1, 2).astype(BF16)

    cos2, sin2 = _rope_tables(n_ctx, n_lat)
    q_scale = GQA_HEAD_DIM ** -0.5 * math.log2(math.e)
    na_bias = _na_bias(na_rpb)

    for l in range(depth):
        mod = mods[l]
        p_na, p_gla, p_gqa = _project(xa, mod, (w_na[l], w_gla[l], w_gqa[l]), (BF16, F32, F32), n_ctx)

        y_na = _neighbourhood_attention(p_na, na_bias[l], n_ctx)
        o_f, o_b = _gla(p_gla, wg[l], bg[l])

        wqn, wkn = gqa_q_norm[l], gqa_k_norm[l]
        ta = jnp.concatenate([jnp.tile(cos2 * wqn[_EVEN_ODD] * q_scale, (1, 2)),
                              jnp.tile(cos2 * wkn[_EVEN_ODD], (1, 2))], -1)
        tb = jnp.concatenate([jnp.tile(sin2 * wqn[_ODD_EVEN] * q_scale, (1, 2)),
                              jnp.tile(sin2 * wkn[_ODD_EVEN], (1, 2))], -1)
        qt, kz, vt = _gqa_prep(p_gqa, ta, tb)
        y_gqa = _gqa_attention(qt, kz, vt, n_ctx)

        x1, h2t = _mix_out(xa, mod, y_na, o_f, o_b, p_gla, y_gqa, w_out_b[l],
                           gla_norm_w[l].reshape(1, GLA_DV), ln1_g[l].reshape(1, d), ln1_b[l].reshape(1, d), n_ctx)
        r2, e2, n1, e1 = _peer_route(h2t, wq_t[l], sub_keys[l])
        peer_t = _peer_experts(h2t, u_tab[l], vt_tab[l], r2, e2, n1, e1)
        xa = _peer_out(x1, mod, peer_t, ln2_g[l].reshape(1, d), ln2_b[l].reshape(1, d), n_ctx)

    return xa[n_ctx:][None]
```

```python
import functools
import math

import numpy as np
import jax
import jax.numpy as jnp
from jax import lax
from jax.experimental import pallas as pl
from jax.experimental.pallas import tpu as pltpu

F32 = jnp.float32
BF16 = jnp.bfloat16

D_MODEL = 2048
DEPTH = 4
GRID_W = 64
EPS = 1e-6

NA_HEADS = 8
NA_HEAD_DIM = 64
NA_WIN_H = 8
NA_WIN_W = 16
GLA_HEADS = 4
GLA_DK = 128
GLA_DV = 256
GLA_GATE_RANK = 16
GLA_GATE_NORM = 16.0
GLA_CHUNK = 64
GQA_HEADS = 8
GQA_KV_HEADS = 2
GQA_HEAD_DIM = 64
ROPE_BASE = 10000.0
PEER_HEADS = 8
PEER_NKEYS = 128
PEER_KEY_DIM = 256
PEER_TOPK = 16

NA_W = NA_HEADS * NA_HEAD_DIM
GLA_KW = GLA_HEADS * GLA_DK
GLA_VW = GLA_HEADS * GLA_DV
GQA_QW = GQA_HEADS * GQA_HEAD_DIM
GQA_KVW = GQA_KV_HEADS * GQA_HEAD_DIM
DEEPNORM_ALPHA = (2.0 * DEPTH) ** 0.25

LANES = 128
BF16_SUBLANES = 16
VMEM_LIMIT_BYTES = 58 * 1024 * 1024

NA_QROWS = 4
NA_BAND = NA_QROWS + NA_WIN_H
NA_QBLK = NA_QROWS * GRID_W
NA_KBLK = NA_BAND * GRID_W
GLA_BLK = 256
GQA_KEY_CHUNK = 256
GQA_LOOP_CHUNKS = 8
GQA_ONES_ROWS = 16
PEER_EXPERT_BLK = 1024
PEER_SUB = 512
NEG = -1e30


def _cparams(sem):
    return pltpu.CompilerParams(dimension_semantics=sem, vmem_limit_bytes=VMEM_LIMIT_BYTES)


def _pick(n, cands):
    for c in cands:
        if n % c == 0:
            return c
    raise ValueError(f"no block size in {cands} divides {n}")


def _dot(a, b):
    return jnp.dot(a, b, preferred_element_type=F32)


def _dot_nt(a, b):
    return lax.dot_general(a, b, (((1,), (1,)), ((), ())), preferred_element_type=F32)


def _dot_tn(a, b):
    return lax.dot_general(a, b, (((0,), (0,)), ((), ())), preferred_element_type=F32)


def _layer_norm(x):
    mu = jnp.mean(x, axis=-1, keepdims=True)
    xc = x - mu
    var = jnp.mean(xc * xc, axis=-1, keepdims=True)
    return xc * lax.rsqrt(var + EPS)


def _row_is_ctx(block_idx, tm, n_ctx):
    rows = block_idx * tm + lax.broadcasted_iota(jnp.int32, (tm, 1), 0)
    return rows < n_ctx


def _mod_row(mod_ref, idx, is_ctx):
    return jnp.where(is_ctx, mod_ref[0, idx:idx + 1, :], mod_ref[1, idx:idx + 1, :])


def _mod_kernel(c_ref, w_ref, b_ref, o_ref):
    w = w_ref[0]
    reps = w.shape[1] // LANES
    for m in range(2):
        cb = c_ref[m]
        s = cb / (1.0 + jnp.exp(-cb))
        o_ref[0, m:m + 1, :] = jnp.sum(w * jnp.tile(s, (1, reps)), axis=0, keepdims=True) + b_ref[0]


def _modulation(c2, w_mod, b_mod):
    depth, d, n = w_mod.shape
    tn = _pick(n, (512, 256, 128))
    cb = jnp.broadcast_to(c2[:, :, None], (2, d, LANES))
    return pl.pallas_call(
        _mod_kernel,
        grid=(depth, n // tn),
        in_specs=[pl.BlockSpec((2, d, LANES), lambda l, j: (0, 0, 0)),
                  pl.BlockSpec((1, d, tn), lambda l, j: (l, 0, j)),
                  pl.BlockSpec((1, 1, tn), lambda l, j: (l, 0, j))],
        out_specs=pl.BlockSpec((1, 2, tn), lambda l, j: (l, 0, j)),
        out_shape=jax.ShapeDtypeStruct((depth, 2, n), F32),
        compiler_params=_cparams(("arbitrary", "arbitrary")),
        name="modulation",
    )(cb, w_mod, b_mod.reshape(depth, 1, n))


def _proj_kernel(x_ref, mod_ref, *refs, n_ctx, tm):
    is_ctx = _row_is_ctx(pl.program_id(0), tm, n_ctx)
    xn = _layer_norm(x_ref[...])
    h = (xn * (1.0 + _mod_row(mod_ref, 1, is_ctx)) + _mod_row(mod_ref, 0, is_ctx)).astype(BF16)
    n_groups = len(refs) // 2
    for w_ref, o_ref in zip(refs[:n_groups], refs[n_groups:]):
        o_ref[...] = _dot(h, w_ref[...]).astype(o_ref.dtype)


def _project(xa, mod, weights, out_dtypes, n_ctx):
    tt, d = xa.shape
    tm = _pick(tt, (256, 128))
    w_specs = [pl.BlockSpec(w.shape, lambda i: (0, 0), pipeline_mode=pl.Buffered(1)) for w in weights]
    return pl.pallas_call(
        functools.partial(_proj_kernel, n_ctx=n_ctx, tm=tm),
        grid=(tt // tm,),
        in_specs=[pl.BlockSpec((tm, d), lambda i: (i, 0)),
                  pl.BlockSpec((2, 6, d), lambda i: (0, 0, 0))] + w_specs,
        out_specs=[pl.BlockSpec((tm, w.shape[1]), lambda i: (i, 0)) for w in weights],
        out_shape=[jax.ShapeDtypeStruct((tt, w.shape[1]), dt) for w, dt in zip(weights, out_dtypes)],
        compiler_params=_cparams(("arbitrary",)),
        name="ln_mod_project",
    )(xa, mod, *weights)


def _na_kernel(q_ref, k_ref, v_ref, bias_ref, o_ref, *, n_ctx, n_band_starts):
    qb = pl.program_id(1)
    scale = NA_HEAD_DIM ** -0.5
    lane = lax.broadcasted_iota(jnp.int32, (1, 2 * NA_HEAD_DIM), 1)
    q = q_ref[...]
    kc = k_ref[0:n_ctx, :]
    vc = v_ref[0:n_ctx, :]

    def head_q(h):
        return jnp.where((lane // NA_HEAD_DIM) == h, q, jnp.zeros_like(q))

    @pl.when(qb == 0)
    def _():
        outs = []
        for h in range(2):
            s = _dot_nt(head_q(h), kc) * scale
            m = jnp.max(s, axis=-1, keepdims=True)
            p = jnp.exp(s - m)
            l = jnp.sum(p, axis=-1, keepdims=True)
            outs.append(_dot(p.astype(BF16), vc) / l)
        o_ref[...] = jnp.where((lane // NA_HEAD_DIM) == 0, outs[0], outs[1]).astype(o_ref.dtype)

    @pl.when(qb > 0)
    def _():
        start_blk = jnp.clip(qb - 2, 0, n_band_starts - 1)
        start = pl.multiple_of(n_ctx + start_blk * NA_QBLK, NA_QBLK)
        kb = k_ref[pl.ds(start, NA_KBLK), :]
        vb = v_ref[pl.ds(start, NA_KBLK), :]
        outs = []
        for h in range(2):
            qh = head_q(h)
            sw = _dot_nt(qh, kb) * scale + bias_ref[0, h]
            sc = _dot_nt(qh, kc) * scale
            m = jnp.maximum(jnp.max(sw, axis=-1, keepdims=True), jnp.max(sc, axis=-1, keepdims=True))
            pw = jnp.exp(sw - m)
            pc = jnp.exp(sc - m)
            l = jnp.sum(pw, axis=-1, keepdims=True) + jnp.sum(pc, axis=-1, keepdims=True)
            outs.append((_dot(pw.astype(BF16), vb) + _dot(pc.astype(BF16), vc)) / l)
        o_ref[...] = jnp.where((lane // NA_HEAD_DIM) == 0, outs[0], outs[1]).astype(o_ref.dtype)


def _na_bias(rpb):
    a = np.arange(NA_QROWS)[:, None]
    j = np.arange(NA_BAND)[None, :]
    ws = [np.zeros_like(a), a, np.full_like(a, NA_BAND - NA_WIN_H)]
    off = [0, -NA_WIN_H // 2, -(NA_QROWS + NA_WIN_H // 2)]
    qc = np.arange(GRID_W)[:, None]
    kc = np.arange(GRID_W)[None, :]
    wcs = np.clip(qc - NA_WIN_W // 2, 0, GRID_W - NA_WIN_W)
    col_ok = (kc >= wcs) & (kc < wcs + NA_WIN_W)
    n_dr, n_dc = 2 * NA_WIN_H - 1, 2 * NA_WIN_W - 1
    dc = np.where(col_ok, kc - qc + NA_WIN_W - 1, n_dc)
    oh_c = (dc[:, :, None] == np.arange(n_dc + 1)).astype(np.float32)
    oh_r = []
    for p in range(3):
        row_ok = (j >= ws[p]) & (j < ws[p] + NA_WIN_H)
        dr = np.where(row_ok, j - a + off[p] + NA_WIN_H - 1, n_dr)
        oh_r.append((dr[:, :, None] == np.arange(n_dr + 1)).astype(np.float32))
    table = jnp.pad(rpb, ((0, 0), (0, 0), (0, 1), (0, 1)), constant_values=NEG)
    b = jnp.einsum('pajr,lhrc,qkc->lphaqjk', np.stack(oh_r), table, oh_c, precision=lax.Precision.HIGHEST)
    return b.reshape(rpb.shape[0], 3, rpb.shape[1], NA_QBLK, NA_KBLK)


def _neighbourhood_attention(p_na, bias, n_ctx):
    tt = p_na.shape[0]
    n_qb = tt // NA_QBLK
    n_lat_blk = n_qb - 1
    n_band_starts = n_lat_blk - NA_BAND // NA_QROWS + 1
    hp = NA_HEADS // 2

    def bias_idx(h, qb):
        pat = jnp.where(qb <= 1, 0, jnp.where(qb == n_qb - 1, 2, 1))
        return (pat, h, 0, 0)

    return pl.pallas_call(
        functools.partial(_na_kernel, n_ctx=n_ctx, n_band_starts=n_band_starts),
        grid=(hp, n_qb),
        in_specs=[pl.BlockSpec((NA_QBLK, 128), lambda h, qb: (qb, h)),
                  pl.BlockSpec((tt, 128), lambda h, qb: (0, hp + h)),
                  pl.BlockSpec((tt, 128), lambda h, qb: (0, 2 * hp + h)),
                  pl.BlockSpec((1, 2, NA_QBLK, NA_KBLK), bias_idx)],
        out_specs=pl.BlockSpec((NA_QBLK, 128), lambda h, qb: (qb, h)),
        out_shape=jax.ShapeDtypeStruct((tt, NA_W), BF16),
        compiler_params=_cparams(("arbitrary", "arbitrary")),
        name="neighbourhood_attention",
    )(p_na, p_na, p_na, bias)


def _log_sigmoid(x):
    return jnp.minimum(x, 0.0) - jnp.log(1.0 + jnp.exp(-jnp.abs(x)))


def _gla_chunk(q, k, v, lr, wg, bg, tri, causal, s_ref, last_row):
    x = jnp.dot(lr, wg, preferred_element_type=F32, precision=lax.Precision.HIGHEST) + bg
    la = _log_sigmoid(x) * (1.0 / GLA_GATE_NORM)
    b = jnp.dot(tri, la, preferred_element_type=F32, precision=lax.Precision.HIGHEST)
    b_last = b[last_row:last_row + 1, :]
    q_t = (q * (GLA_DK ** -0.5) * jnp.exp(b)).astype(BF16)
    k_t = (k * jnp.exp(-b)).astype(BF16)
    k_end = (k * jnp.exp(b_last - b)).astype(BF16)
    dec = jnp.exp(b_last)
    outs = []
    for h in range(GLA_HEADS):
        ks = slice(h * GLA_DK, (h + 1) * GLA_DK)
        vh = v[:, h * GLA_DV:(h + 1) * GLA_DV].astype(BF16)
        att = jnp.where(causal, _dot_nt(q_t[:, ks], k_t[:, ks]), 0.0)
        st = s_ref[h]
        o = _dot(att.astype(BF16), vh) + _dot_nt(q_t[:, ks], st.astype(BF16))
        s_ref[h] = st * dec[:, ks] + _dot_tn(vh, k_end[:, ks])
        outs.append(o)
    return jnp.concatenate(outs, axis=-1)


def _gla_kernel(qf_ref, kf_ref, vf_ref, lf_ref, qb_ref, kb_ref, vb_ref, lb_ref, wg_ref, bg_ref,
                of_ref, ob_ref, sf_ref, sb_ref):
    @pl.when(pl.program_id(0) == 0)
    def _():
        sf_ref[...] = jnp.zeros_like(sf_ref)
        sb_ref[...] = jnp.zeros_like(sb_ref)

    r = lax.broadcasted_iota(jnp.int32, (GLA_CHUNK, GLA_CHUNK), 0)
    c = lax.broadcasted_iota(jnp.int32, (GLA_CHUNK, GLA_CHUNK), 1)
    lower = r >= c
    upper = r <= c
    tri_f = lower.astype(F32)
    tri_b = upper.astype(F32)
    n_chunks = GLA_BLK // GLA_CHUNK
    for ci in range(n_chunks):
        rows = slice(ci * GLA_CHUNK, (ci + 1) * GLA_CHUNK)
        of_ref[rows, :] = _gla_chunk(qf_ref[rows, :], kf_ref[rows, :], vf_ref[rows, :], lf_ref[rows, :],
                                     wg_ref[0], bg_ref[0], tri_f, lower, sf_ref, GLA_CHUNK - 1)
        cb = n_chunks - 1 - ci
        rows = slice(cb * GLA_CHUNK, (cb + 1) * GLA_CHUNK)
        ob_ref[rows, :] = _gla_chunk(qb_ref[rows, :], kb_ref[rows, :], vb_ref[rows, :], lb_ref[rows, :],
                                     wg_ref[1], bg_ref[1], tri_b, upper, sb_ref, 0)


def _gla(p_gla, wg, bg):
    tt = p_gla.shape[0]
    nb = tt // GLA_BLK
    fwd = lambda i: i
    bwd = lambda i: jnp.where(i == 0, 0, nb - i)
    lr_blk = (2 * GLA_KW + 2 * GLA_VW) // 128

    def specs(rowmap):
        return [pl.BlockSpec((GLA_BLK, GLA_KW), lambda i: (rowmap(i), 0)),
                pl.BlockSpec((GLA_BLK, GLA_KW), lambda i: (rowmap(i), 1)),
                pl.BlockSpec((GLA_BLK, GLA_VW), lambda i: (rowmap(i), 1)),
                pl.BlockSpec((GLA_BLK, 128), lambda i: (rowmap(i), lr_blk))]

    return pl.pallas_call(
        _gla_kernel,
        grid=(nb,),
        in_specs=specs(fwd) + specs(bwd) + [
            pl.BlockSpec((2, 128, GLA_KW), lambda i: (0, 0, 0)),
            pl.BlockSpec((2, 1, GLA_KW), lambda i: (0, 0, 0))],
        out_specs=[pl.BlockSpec((GLA_BLK, GLA_VW), lambda i: (fwd(i), 0)),
                   pl.BlockSpec((GLA_BLK, GLA_VW), lambda i: (bwd(i), 0))],
        out_shape=[jax.ShapeDtypeStruct((tt, GLA_VW), F32)] * 2,
        scratch_shapes=[pltpu.VMEM((GLA_HEADS, GLA_DV, GLA_DK), F32)] * 2,
        compiler_params=_cparams(("arbitrary",)),
        name="gla_bidirectional",
    )(*([p_gla] * 8), wg, bg)


def _group_sum_sq(x, ones_bd):
    sq = x * x
    hi = sq.astype(BF16)
    lo = (sq - hi.astype(F32)).astype(BF16)
    return _dot(hi, ones_bd) + _dot(lo, ones_bd)


def _gqa_prep_kernel(p_ref, ta_ref, tb_ref, ones_ref, qt_ref, kz_ref, vt_ref):
    d = GQA_HEAD_DIM
    x = p_ref[...]
    q, qs = x[:, 0:GQA_QW], x[:, GQA_QW:2 * GQA_QW]
    o = 2 * GQA_QW
    k, ks, v = x[:, o:o + GQA_KVW], x[:, o + GQA_KVW:o + 2 * GQA_KVW], x[:, o + 2 * GQA_KVW:o + 3 * GQA_KVW]
    ta, tb = ta_ref[...], tb_ref[...]
    taq = jnp.tile(ta[:, 0:128], (1, GQA_QW // 128))
    tbq = jnp.tile(tb[:, 0:128], (1, GQA_QW // 128))
    rq = lax.rsqrt(_group_sum_sq(q, ones_ref[...]) * (1.0 / d) + EPS)
    qr = rq * (q * taq + qs * tbq)
    rk = lax.rsqrt(_group_sum_sq(k, ones_ref[0:GQA_KVW, 0:GQA_KVW]) * (1.0 / d) + EPS)
    kr = rk * (k * ta[:, 128:256] + ks * tb[:, 128:256])
    qt_ref[...] = qr.T.astype(BF16)
    lane = lax.broadcasted_iota(jnp.int32, (1, GQA_KVW), 1)
    for g in range(GQA_KV_HEADS):
        kz_ref[g] = jnp.where((lane // d) == g, kr, 0.0).astype(BF16)
    vt_ref[...] = v.T.astype(BF16)


def _gqa_prep(p_gqa, ta, tb):
    tt, n = p_gqa.shape
    tm = _pick(tt, (384, 256, 128))
    hid = np.arange(GQA_QW) // GQA_HEAD_DIM
    ones_bd = jnp.asarray(hid[:, None] == hid[None, :], BF16)
    return pl.pallas_call(
        _gqa_prep_kernel,
        grid=(tt // tm,),
        in_specs=[pl.BlockSpec((tm, n), lambda i: (i, 0)),
                  pl.BlockSpec((tm, 256), lambda i: (i, 0)),
                  pl.BlockSpec((tm, 256), lambda i: (i, 0)),
                  pl.BlockSpec((GQA_QW, GQA_QW), lambda i: (0, 0))],
        out_specs=[pl.BlockSpec((GQA_QW, tm), lambda i: (0, i)),
                   pl.BlockSpec((GQA_KV_HEADS, tm, GQA_KVW), lambda i: (0, i, 0)),
                   pl.BlockSpec((GQA_KVW, tm), lambda i: (0, i))],
        out_shape=[jax.ShapeDtypeStruct((GQA_QW, tt), BF16),
                   jax.ShapeDtypeStruct((GQA_KV_HEADS, tt, GQA_KVW), BF16),
                   jax.ShapeDtypeStruct((GQA_KVW, tt), BF16)],
        compiler_params=_cparams(("arbitrary",)),
        name="gqa_prep",
    )(p_gqa, ta, tb, ones_bd)


def _gqa_kernel(qt_ref, kz_ref, vt_ref, o_ref, s0_ref, s1_ref, acc_ref, *, n_ctx):
    qb = pl.program_id(1)
    d = GQA_HEAD_DIM
    group = GQA_HEADS // GQA_KV_HEADS
    kc = GQA_KEY_CHUNK
    tq = qt_ref.shape[1]
    q4 = jnp.concatenate([jnp.concatenate([qt_ref[g * d:(g + 1) * d, :]] * 2, axis=0) for g in range(group)],
                         axis=1)

    def scores(c, s_ref):
        start = pl.multiple_of(c * kc, kc)
        s = _dot(kz_ref[0, pl.ds(start, kc), :], q4)
        s_ref[...] = s
        return jnp.max(s, axis=0, keepdims=True)

    def accumulate(c, s_ref, m_chunk, m):
        m_new = jnp.maximum(m, m_chunk)
        p = jnp.exp2(s_ref[...] - m_new)
        acc_ref[...] = acc_ref[...] * jnp.exp2(m - m_new) + _dot(vt_ref[0, c], p.astype(BF16))
        return m_new

    def attend(n_chunks):
        acc_ref[...] = jnp.zeros_like(acc_ref)
        m = jnp.full((1, group * tq), -jnp.inf, F32)
        mc0 = scores(0, s0_ref)

        def body(j, carry):
            m, mc0 = carry
            for k in range(0, GQA_LOOP_CHUNKS, 2):
                c = GQA_LOOP_CHUNKS * j + k
                mc1 = scores(c + 1, s1_ref)
                m = accumulate(c, s0_ref, mc0, m)
                mc0 = scores(c + 2, s0_ref)
                m = accumulate(c + 1, s1_ref, mc1, m)
            return m, mc0

        m, mc0 = lax.fori_loop(0, (n_chunks - 1) // GQA_LOOP_CHUNKS, body, (m, mc0))
        accumulate(n_chunks - 1, s0_ref, mc0, m)
        out = acc_ref[0:d, :] / acc_ref[d:d + 1, :]
        o_ref[...] = jnp.concatenate([out[:, g * tq:(g + 1) * tq] for g in range(group)],
                                     axis=0).T.astype(o_ref.dtype)

    @pl.when(qb == 0)
    def _():
        attend(n_ctx // kc)

    @pl.when(qb > 0)
    def _():
        attend(kz_ref.shape[1] // kc)


def _gqa_attention(qt, kz, vt, n_ctx):
    tt = qt.shape[1]
    tq = n_ctx
    kc = GQA_KEY_CHUNK
    group = GQA_HEADS // GQA_KV_HEADS
    gw = group * GQA_HEAD_DIM
    assert tt % kc == 0 and n_ctx % kc == 0
    assert (tt // kc) % GQA_LOOP_CHUNKS == 1 and (n_ctx // kc) % GQA_LOOP_CHUNKS == 1
    vt3 = vt.reshape(GQA_KV_HEADS, GQA_HEAD_DIM, tt // kc, kc).transpose(0, 2, 1, 3)
    extra = jnp.zeros((GQA_KV_HEADS, tt // kc, GQA_ONES_ROWS, kc), BF16).at[:, :, 0, :].set(1.0)
    vt3 = jnp.concatenate([vt3, extra], axis=2)
    vrows = GQA_HEAD_DIM + GQA_ONES_ROWS
    return pl.pallas_call(
        functools.partial(_gqa_kernel, n_ctx=n_ctx),
        grid=(GQA_KV_HEADS, tt // tq),
        in_specs=[pl.BlockSpec((gw, tq), lambda g, i: (g, i)),
                  pl.BlockSpec((1, tt, GQA_KVW), lambda g, i: (g, 0, 0)),
                  pl.BlockSpec((1, tt // kc, vrows, kc), lambda g, i: (g, 0, 0, 0))],
        out_specs=pl.BlockSpec((tq, gw), lambda g, i: (i, g)),
        out_shape=jax.ShapeDtypeStruct((tt, GQA_QW), BF16),
        scratch_shapes=[pltpu.VMEM((kc, group * tq), F32), pltpu.VMEM((kc, group * tq), F32),
                        pltpu.VMEM((vrows, group * tq), F32)],
        compiler_params=_cparams(("arbitrary", "arbitrary")),
        name="gqa_attention",
    )(qt, kz, vt3)


def _mix_out_kernel(x_ref, mod_ref, yna_ref, of_ref, ob_ref, gate_ref, ygqa_ref, w_ref, nw_ref,
                    g_ref, b_ref, x1_ref, h2t_ref, *, n_ctx, tm):
    is_ctx = _row_is_ctx(pl.program_id(0), tm, n_ctx)
    o = of_ref[...] + ob_ref[...]
    nw = nw_ref[...]
    parts = []
    for h in range(GLA_HEADS):
        oh = o[:, h * GLA_DV:(h + 1) * GLA_DV]
        parts.append(oh * lax.rsqrt(jnp.mean(oh * oh, axis=-1, keepdims=True) + EPS) * nw)
    gate = gate_ref[...]
    y_gla = (jnp.concatenate(parts, axis=-1) * (gate / (1.0 + jnp.exp(-gate)))).astype(BF16)
    y = (_dot(yna_ref[...], w_ref[0:NA_W, :]) + _dot(y_gla, w_ref[NA_W:NA_W + GLA_VW, :])
         + _dot(ygqa_ref[...], w_ref[NA_W + GLA_VW:, :]))
    x1 = _layer_norm(DEEPNORM_ALPHA * x_ref[...] + _mod_row(mod_ref, 2, is_ctx) * y) * g_ref[...] + b_ref[...]
    x1_ref[...] = x1
    h2 = _layer_norm(x1) * (1.0 + _mod_row(mod_ref, 4, is_ctx)) + _mod_row(mod_ref, 3, is_ctx)
    h2t_ref[...] = h2.T.astype(BF16)


def _mix_out(xa, mod, y_na, o_f, o_b, p_gla, y_gqa, w_out, norm_w, ln_g, ln_b, n_ctx):
    tt, d = xa.shape
    tm = _pick(tt, (384, 256, 128))
    row = lambda i: (i, 0)
    const2 = lambda i: (0, 0)
    return pl.pallas_call(
        functools.partial(_mix_out_kernel, n_ctx=n_ctx, tm=tm),
        grid=(tt // tm,),
        in_specs=[pl.BlockSpec((tm, d), row),
                  pl.BlockSpec((2, 6, d), lambda i: (0, 0, 0)),
                  pl.BlockSpec((tm, NA_W), row),
                  pl.BlockSpec((tm, GLA_VW), row),
                  pl.BlockSpec((tm, GLA_VW), row),
                  pl.BlockSpec((tm, GLA_VW), lambda i: (i, 2)),
                  pl.BlockSpec((tm, GQA_QW), row),
                  pl.BlockSpec(w_out.shape, const2),
                  pl.BlockSpec((1, GLA_DV), const2),
                  pl.BlockSpec((1, d), const2),
                  pl.BlockSpec((1, d), const2)],
        out_specs=[pl.BlockSpec((tm, d), row), pl.BlockSpec((d, tm), lambda i: (0, i))],
        out_shape=[jax.ShapeDtypeStruct((tt, d), F32), jax.ShapeDtypeStruct((d, tt), BF16)],
        compiler_params=_cparams(("arbitrary",)),
        name="mixer_out_postln",
    )(xa, mod, y_na, o_f, o_b, p_gla, y_gqa, w_out, norm_w, ln_g, ln_b)


def _top16_exact(s, key_iota):
    cur = s
    rank = jnp.full(s.shape, float(PEER_TOPK), F32)
    vals = []
    for r in range(PEER_TOPK):
        m = jnp.max(cur, axis=0, keepdims=True)
        idx = jnp.min(jnp.where(cur == m, key_iota, float(PEER_NKEYS)), axis=0, keepdims=True)
        sel = key_iota == idx
        rank = jnp.where(sel, float(r), rank)
        cur = jnp.where(sel, -jnp.inf, cur)
        vals.append(m)
    return jnp.concatenate(vals, axis=0), rank


_CODE_SCALE = 2.0 ** 100
_CODE_STEP = 64.0


def _rank_code(r):
    return -_CODE_SCALE * (1.0 + (r + 1) / _CODE_STEP)


def _top16_fast(s):
    cur = s
    vals = []
    for r in range(PEER_TOPK):
        m = jnp.max(cur, axis=0, keepdims=True)
        cur = jnp.where(cur == m, _rank_code(r), cur)
        vals.append(m)
    return jnp.concatenate(vals, axis=0), cur


def _coded(cur):
    return cur <= _rank_code(0)


def _decode_rank(cur):
    rank = (cur * (-1.0 / _CODE_SCALE) - 1.0) * _CODE_STEP - 1.0
    return jnp.where(_coded(cur), rank, float(PEER_TOPK))


def _cand_layout():
    k = PEER_TOPK
    rows = [(0, rb) for rb in range(k)]
    for ra in range(1, 5):
        rows += [(ra, rb) for rb in range(8)]
    rows += [(ra, 0) for ra in range(8, k)]
    rows += [(ra, 1) for ra in range(8)]
    rows += [(ra, 0) for ra in range(8)]
    seen, valid = set(), []
    for pair in rows:
        valid.append(pair not in seen)
        seen.add(pair)
    needed = {(ra, rb) for ra in range(k) for rb in range(k) if (ra + 1) * (rb + 1) <= k}
    assert needed <= seen
    ra = np.array([p[0] for p in rows], np.float32)
    rb = np.array([p[1] for p in rows], np.float32)
    return ra, rb, np.array(valid)


def _select_pairs(a, b, c_ra, c_flat, c_ok, exact):
    k = PEER_TOPK
    tt = a.shape[1]
    blocks = [a[0:1, :] + b]
    for ra in range(1, 5):
        blocks.append(a[ra:ra + 1, :] + b[0:8, :])
    blocks.append(a[8:k, :] + b[0:1, :])
    blocks.append(a[0:8, :] + b[1:2, :])
    blocks.append(a[0:8, :] + b[0:1, :])
    cand = jnp.where(c_ok, jnp.concatenate(blocks, axis=0), -jnp.inf)
    m0 = a[0:1, :] + b[0:1, :]
    z = jnp.zeros((1, tt), F32)
    if exact:
        rank_iota = lax.broadcasted_iota(jnp.int32, (k, tt), 0).astype(F32)
        cnt = jnp.zeros((k, tt), F32)
        for _ in range(k):
            m = jnp.max(cand, axis=0, keepdims=True)
            idx = jnp.min(jnp.where(cand == m, c_flat, 1e9), axis=0, keepdims=True)
            sel = c_flat == idx
            ra_sel = jnp.sum(jnp.where(sel, c_ra, 0.0), axis=0, keepdims=True)
            cnt = cnt + jnp.where(rank_iota == ra_sel, 1.0, 0.0)
            z = z + jnp.exp(m - m0)
            cand = jnp.where(sel, -jnp.inf, cand)
        return cnt, z, None
    for _ in range(k):
        m = jnp.max(cand, axis=0, keepdims=True)
        z = z + jnp.exp(m - m0)
        cand = jnp.where(cand == m, -jnp.inf, cand)
    sel = jnp.where(jnp.logical_and(c_ok, cand == -jnp.inf), 1.0, 0.0)
    low = sel[56:64, :] + sel[64:72, :]
    row8 = lax.broadcasted_iota(jnp.int32, (8, tt), 0)
    sums = [jnp.sum(sel[0:16, :], axis=0, keepdims=True)]
    sums += [jnp.sum(sel[16 + 8 * i:24 + 8 * i, :], axis=0, keepdims=True) for i in range(4)]
    for r, v in enumerate(sums):
        low = jnp.where(row8 == r, v, low)
    cnt = jnp.concatenate([low, sel[48:56, :]], axis=0)
    return cnt, z, jnp.sum(sel, axis=0, keepdims=True)


def _route_head(s1, s2, key_iota, c_ra, c_flat, c_ok, exact):
    k = PEER_TOPK
    if exact:
        a, r1 = _top16_exact(s1, key_iota)
        b, r2 = _top16_exact(s2, key_iota)
    else:
        a, cur1 = _top16_fast(s1)
        b, cur2 = _top16_fast(s2)
        r2 = _decode_rank(cur2)
    cnt, z, n_sel = _select_pairs(a, b, c_ra, c_flat, c_ok, exact)
    n1 = jnp.zeros(s1.shape, F32)
    for r in range(k):
        hit = (r1 == float(r)) if exact else (s1 == a[r:r + 1, :])
        n1 = jnp.where(hit, cnt[r:r + 1, :], n1)
    outs = (r2, jnp.exp(s2 - b[0:1, :]) / z, n1, jnp.exp(s1 - a[0:1, :]))
    if exact:
        return outs, None
    n_coded = lambda cur: jnp.sum(jnp.where(_coded(cur), 1.0, 0.0), axis=0, keepdims=True)
    clean = jnp.logical_and(n_sel == float(k),
                            jnp.logical_and(n_coded(cur1) == float(k), n_coded(cur2) == float(k)))
    return outs, jnp.where(clean, 0.0, 1.0)


def _route_kernel(h_ref, wq_ref, sk_ref, cst_ref, r2_ref, e2_ref, n1_ref, e1_ref):
    tt = h_ref.shape[1]
    key_iota = lax.broadcasted_iota(jnp.int32, (PEER_NKEYS, tt), 0).astype(F32)
    cst = cst_ref[...]
    n_rows = cst.shape[0]
    c_ra = jnp.broadcast_to(cst[:, 0:1], (n_rows, tt))
    c_flat = jnp.broadcast_to(cst[:, 1:2], (n_rows, tt))
    c_ok = jnp.broadcast_to(cst[:, 2:3], (n_rows, tt)) > 0.5
    half = PEER_KEY_DIM // 2

    def head(h, qt_h, exact):
        s1 = _dot(sk_ref[h, 0], qt_h[0:half, :].astype(BF16))
        s2 = _dot(sk_ref[h, 1], qt_h[half:2 * half, :].astype(BF16))
        outs, bad = _route_head(s1, s2, key_iota, c_ra, c_flat, c_ok, exact)
        for ref, val in zip((r2_ref, e2_ref, n1_ref, e1_ref), outs):
            ref[h] = val.astype(ref.dtype)
        return bad

    qt = _dot(wq_ref[...], h_ref[...])
    tied = [jnp.max(head(h, qt[h * PEER_KEY_DIM:(h + 1) * PEER_KEY_DIM, :], False)) for h in range(PEER_HEADS)]
    for h in range(PEER_HEADS):
        @pl.when(tied[h] > 0.0)
        def _(h=h):
            head(h, _dot(wq_ref[h * PEER_KEY_DIM:(h + 1) * PEER_KEY_DIM, :], h_ref[...]), True)


def _peer_route(h2t, wq_t, sub_keys):
    d, tt = h2t.shape
    tb = _pick(tt, (256, 128))
    ra, rb, valid = _cand_layout()
    cst = np.zeros((ra.shape[0], LANES), np.float32)
    flat = np.where(valid, ra * PEER_TOPK + rb, 1000.0 + np.arange(ra.shape[0]))
    cst[:, 0], cst[:, 1], cst[:, 2] = ra, flat, valid
    shp = (PEER_HEADS, PEER_NKEYS, tt)
    out_spec = pl.BlockSpec((PEER_HEADS, PEER_NKEYS, tb), lambda i: (0, 0, i))
    return pl.pallas_call(
        _route_kernel,
        grid=(tt // tb,),
        in_specs=[pl.BlockSpec((d, tb), lambda i: (0, i)),
                  pl.BlockSpec(wq_t.shape, lambda i: (0, 0)),
                  pl.BlockSpec(sub_keys.shape, lambda i: (0, 0, 0, 0)),
                  pl.BlockSpec(cst.shape, lambda i: (0, 0))],
        out_specs=[out_spec] * 4,
        out_shape=[jax.ShapeDtypeStruct(shp, dt) for dt in (BF16, BF16, F32, F32)],
        compiler_params=_cparams(("arbitrary",)),
        name="peer_route",
    )(h2t, wq_t, sub_keys, jnp.asarray(cst))


def _gelu_tanh(x):
    c = math.sqrt(2.0 / math.pi)
    hx = 0.5 * x
    return hx * jnp.tanh(x * (c + (c * 0.044715) * (x * x))) + hx


def _peer_kernel(h_ref, u_ref, vt_ref, r2_ref, e2_ref, n1_ref, e1_ref, o_ref):
    e = pl.program_id(1)
    eb = u_ref.shape[0]
    n_i = eb // PEER_NKEYS

    @pl.when(e == 0)
    def _():
        o_ref[...] = jnp.zeros_like(o_ref)

    n_sub = eb // PEER_SUB
    sub_rows = lambda s: slice(s * PEER_SUB, (s + 1) * PEER_SUB)
    pre = _dot(u_ref[sub_rows(0), :].astype(BF16), h_ref[...])
    for s in range(n_sub):
        rows = sub_rows(s)
        cur = pre
        if s + 1 < n_sub:
            pre = _dot(u_ref[sub_rows(s + 1), :].astype(BF16), h_ref[...])
        act = _gelu_tanh(cur)
        parts = []
        for ii in range(PEER_SUB // PEER_NKEYS):
            i = e * n_i + s * (PEER_SUB // PEER_NKEYS) + ii
            tb = act.shape[1]
            tiles = PEER_NKEYS // BF16_SUBLANES
            g = None
            for h in range(PEER_HEADS):
                n1 = jnp.broadcast_to(n1_ref[h, pl.ds(i, 1), :], (BF16_SUBLANES, tb)).astype(BF16)
                e1 = jnp.broadcast_to(e1_ref[h, pl.ds(i, 1), :], (BF16_SUBLANES, tb)).astype(BF16)
                r2t = r2_ref[h].reshape(tiles, BF16_SUBLANES, tb)
                e2t = e2_ref[h].reshape(tiles, BF16_SUBLANES, tb)
                t = jnp.where(r2t < n1[None], e2t, jnp.zeros_like(e2t)) * e1[None]
                g = t if g is None else g + t
            a16 = act[ii * PEER_NKEYS:(ii + 1) * PEER_NKEYS, :].astype(BF16).reshape(tiles, BF16_SUBLANES, tb)
            parts.append((a16 * g).reshape(PEER_NKEYS, tb))
        o_ref[...] += _dot(vt_ref[:, rows], jnp.concatenate(parts, axis=0))


def _peer_experts(h2t, u_tab, vt_tab, r2, e2, n1, e1):
    d, tt = h2t.shape
    n_exp = u_tab.shape[0]
    tb = _pick(tt, (768, 512, 256))
    eb = PEER_EXPERT_BLK
    route_spec = pl.BlockSpec((PEER_HEADS, PEER_NKEYS, tb), lambda i, e: (0, 0, i),
                              pipeline_mode=pl.Buffered(1))
    return pl.pallas_call(
        _peer_kernel,
        grid=(tt // tb, n_exp // eb),
        in_specs=[pl.BlockSpec((d, tb), lambda i, e: (0, i), pipeline_mode=pl.Buffered(1)),
                  pl.BlockSpec((eb, d), lambda i, e: (e, 0)),
                  pl.BlockSpec((d, eb), lambda i, e: (0, e)),
                  route_spec, route_spec, route_spec, route_spec],
        out_specs=pl.BlockSpec((d, tb), lambda i, e: (0, i)),
        out_shape=jax.ShapeDtypeStruct((d, tt), F32),
        compiler_params=_cparams(("arbitrary", "arbitrary")),
        name="peer_experts",
    )(h2t, u_tab, vt_tab, r2, e2, n1, e1)


def _peer_out_kernel(x_ref, mod_ref, pt_ref, g_ref, b_ref, o_ref, *, n_ctx, tm):
    is_ctx = _row_is_ctx(pl.program_id(0), tm, n_ctx)
    y = pt_ref[...].T
    o_ref[...] = (_layer_norm(DEEPNORM_ALPHA * x_ref[...] + _mod_row(mod_ref, 5, is_ctx) * y)
                  * g_ref[...] + b_ref[...])


def _peer_out(x1, mod, peer_t, ln_g, ln_b, n_ctx):
    tt, d = x1.shape
    tm = _pick(tt, (384, 256, 128))
    return pl.pallas_call(
        functools.partial(_peer_out_kernel, n_ctx=n_ctx, tm=tm),
        grid=(tt // tm,),
        in_specs=[pl.BlockSpec((tm, d), lambda i: (i, 0)),
                  pl.BlockSpec((2, 6, d), lambda i: (0, 0, 0)),
                  pl.BlockSpec((d, tm), lambda i: (0, i)),
                  pl.BlockSpec((1, d), lambda i: (0, 0)),
                  pl.BlockSpec((1, d), lambda i: (0, 0))],
        out_specs=pl.BlockSpec((tm, d), lambda i: (i, 0)),
        out_shape=jax.ShapeDtypeStruct((tt, d), F32),
        compiler_params=_cparams(("arbitrary",)),
        name="peer_out_postln",
    )(x1, mod, peer_t, ln_g, ln_b)


def _rope_tables(n_ctx, n_lat):
    half = GQA_HEAD_DIM // 2
    inv = ROPE_BASE ** (-jnp.arange(0, half, 2, dtype=F32) / half)
    t = jnp.arange(n_lat)
    row = (t // GRID_W).astype(F32)
    col = (t % GRID_W).astype(F32)
    ang = jnp.concatenate([row[:, None] * inv, col[:, None] * inv], axis=-1)
    cos, sin = jnp.cos(ang), jnp.sin(ang)
    cos2 = jnp.concatenate([jnp.ones((n_ctx, 2 * half), F32), jnp.concatenate([cos, cos], -1)], 0)
    sin2 = jnp.concatenate([jnp.zeros((n_ctx, 2 * half), F32), jnp.concatenate([-sin, sin], -1)], 0)
    return cos2, sin2


_EVEN_ODD = np.concatenate([np.arange(0, GQA_HEAD_DIM, 2), np.arange(1, GQA_HEAD_DIM, 2)])
_ODD_EVEN = np.concatenate([np.arange(1, GQA_HEAD_DIM, 2), np.arange(0, GQA_HEAD_DIM, 2)])


def _head_cols(perm, n_heads):
    return np.concatenate([h * GQA_HEAD_DIM + perm for h in range(n_heads)])


def kernel(x, c, ctx, c_ctx, w_mod, b_mod, w_in, na_rpb, gla_gate_w, gla_gate_b, gla_norm_w,
           gqa_q_norm, gqa_k_norm, w_out, ln1_g, ln1_b, peer_wq, peer_subkeys, peer_u, peer_v,
           ln2_g, ln2_b):
    batch, n_lat, d = x.shape
    assert batch == 1 and d == D_MODEL
    n_ctx = ctx.shape[1]
    assert n_lat % NA_QBLK == 0 and n_lat // GRID_W >= NA_BAND + NA_QROWS
    depth = w_mod.shape[0]

    xa = jnp.concatenate([ctx[0], x[0]], axis=0)
    mods = _modulation(jnp.stack([c_ctx, c[0]]), w_mod, b_mod).reshape(depth, 2, 6, d)

    o = np.cumsum([0, NA_W, NA_W, NA_W, GLA_KW, GLA_KW, GLA_VW, GLA_VW, 2 * GLA_GATE_RANK,
                   GQA_QW, GQA_KVW, GQA_KVW])
    w_na = w_in[:, :, o[0]:o[3]].astype(BF16)
    w_gla = jnp.concatenate([w_in[:, :, o[3]:o[8]],
                             jnp.zeros((depth, d, 128 - 2 * GLA_GATE_RANK), F32)], -1).astype(BF16)
    wq_c, wk_c, wv_c = w_in[:, :, o[8]:o[9]], w_in[:, :, o[9]:o[10]], w_in[:, :, o[10]:o[11]]
    w_gqa = jnp.concatenate([wq_c[:, :, _head_cols(_EVEN_ODD, GQA_HEADS)],
                             wq_c[:, :, _head_cols(_ODD_EVEN, GQA_HEADS)],
                             wk_c[:, :, _head_cols(_EVEN_ODD, GQA_KV_HEADS)],
                             wk_c[:, :, _head_cols(_ODD_EVEN, GQA_KV_HEADS)],
                             wv_c], -1).astype(BF16)
    wg = jnp.zeros((depth, 2, 128, GLA_KW), F32)
    wg = wg.at[:, 0, 0:GLA_GATE_RANK].set(gla_gate_w[:, 0])
    wg = wg.at[:, 1, GLA_GATE_RANK:2 * GLA_GATE_RANK].set(gla_gate_w[:, 1])
    bg = gla_gate_b.reshape(depth, 2, 1, GLA_KW)
    w_out_b = w_out.astype(BF16)
    wq_t = jnp.swapaxes(peer_wq, 1, 2).astype(BF16)
    sub_keys = peer_subkeys.astype(BF16)
    u_tab = peer_u
    vt_tab = jnp.swapaxes(peer_v, 1, 2).astype(BF16)

    cos2, sin2 = _rope_tables(n_ctx, n_lat)
    q_scale = GQA_HEAD_DIM ** -0.5 * math.log2(math.e)
    na_bias = _na_bias(na_rpb)

    for l in range(depth):
        mod = mods[l]
        p_na, p_gla, p_gqa = _project(xa, mod, (w_na[l], w_gla[l], w_gqa[l]), (BF16, F32, F32), n_ctx)

        y_na = _neighbourhood_attention(p_na, na_bias[l], n_ctx)
        o_f, o_b = _gla(p_gla, wg[l], bg[l])

        wqn, wkn = gqa_q_norm[l], gqa_k_norm[l]
        ta = jnp.concatenate([jnp.tile(cos2 * wqn[_EVEN_ODD] * q_scale, (1, 2)),
                              jnp.tile(cos2 * wkn[_EVEN_ODD], (1, 2))], -1)
        tb = jnp.concatenate([jnp.tile(sin2 * wqn[_ODD_EVEN] * q_scale, (1, 2)),
                              jnp.tile(sin2 * wkn[_ODD_EVEN], (1, 2))], -1)
        qt, kz, vt = _gqa_prep(p_gqa, ta, tb)
        y_gqa = _gqa_attention(qt, kz, vt, n_ctx)

        x1, h2t = _mix_out(xa, mod, y_na, o_f, o_b, p_gla, y_gqa, w_out_b[l],
                           gla_norm_w[l].reshape(1, GLA_DV), ln1_g[l].reshape(1, d), ln1_b[l].reshape(1, d), n_ctx)
        r2, e2, n1, e1 = _peer_route(h2t, wq_t[l], sub_keys[l])
        peer_t = _peer_experts(h2t, u_tab[l], vt_tab[l], r2, e2, n1, e1)
        xa = _peer_out(x1, mod, peer_t, ln2_g[l].reshape(1, d), ln2_b[l].reshape(1, d), n_ctx)

    return xa[n_ctx:][None]
```

```python
import functools
import math

import numpy as np
import jax
import jax.numpy as jnp
from jax import lax
from jax.experimental import pallas as pl
from jax.experimental.pallas import tpu as pltpu

F32 = jnp.float32
BF16 = jnp.bfloat16

D_MODEL = 2048
DEPTH = 4
GRID_W = 64
EPS = 1e-6

NA_HEADS = 8
NA_HEAD_DIM = 64
NA_WIN_H = 8
NA_WIN_W = 16
GLA_HEADS = 4
GLA_DK = 128
GLA_DV = 256
GLA_GATE_RANK = 16
GLA_GATE_NORM = 16.0
GLA_CHUNK = 64
GQA_HEADS = 8
GQA_KV_HEADS = 2
GQA_HEAD_DIM = 64
ROPE_BASE = 10000.0
PEER_HEADS = 8
PEER_NKEYS = 128
PEER_KEY_DIM = 256
PEER_TOPK = 16

NA_W = NA_HEADS * NA_HEAD_DIM
GLA_KW = GLA_HEADS * GLA_DK
GLA_VW = GLA_HEADS * GLA_DV
GQA_QW = GQA_HEADS * GQA_HEAD_DIM
GQA_KVW = GQA_KV_HEADS * GQA_HEAD_DIM
DEEPNORM_ALPHA = (2.0 * DEPTH) ** 0.25

LANES = 128
BF16_SUBLANES = 16
VMEM_LIMIT_BYTES = 58 * 1024 * 1024

NA_QROWS = 4
NA_BAND = NA_QROWS + NA_WIN_H
NA_QBLK = NA_QROWS * GRID_W
NA_KBLK = NA_BAND * GRID_W
NA_STEP_HEADS = 4
GLA_BLK = 256
GQA_KEY_CHUNK = 256
GQA_LOOP_CHUNKS = 8
GQA_ONES_ROWS = 16
PEER_EXPERT_BLK = 1024
PEER_SUB = 512
NEG = -1e30


def _cparams(sem):
    return pltpu.CompilerParams(dimension_semantics=sem, vmem_limit_bytes=VMEM_LIMIT_BYTES)


def _pick(n, cands):
    for c in cands:
        if n % c == 0:
            return c
    raise ValueError(f"no block size in {cands} divides {n}")


def _dot(a, b):
    return jnp.dot(a, b, preferred_element_type=F32)


def _dot_nt(a, b):
    return lax.dot_general(a, b, (((1,), (1,)), ((), ())), preferred_element_type=F32)


def _dot_tn(a, b):
    return lax.dot_general(a, b, (((0,), (0,)), ((), ())), preferred_element_type=F32)


def _layer_norm(x):
    mu = jnp.mean(x, axis=-1, keepdims=True)
    xc = x - mu
    var = jnp.mean(xc * xc, axis=-1, keepdims=True)
    return xc * lax.rsqrt(var + EPS)


def _row_is_ctx(block_idx, tm, n_ctx):
    rows = block_idx * tm + lax.broadcasted_iota(jnp.int32, (tm, 1), 0)
    return rows < n_ctx


def _mod_row(mod_ref, idx, is_ctx):
    return jnp.where(is_ctx, mod_ref[0, idx:idx + 1, :], mod_ref[1, idx:idx + 1, :])


def _mod_kernel(c_ref, w_ref, b_ref, o_ref):
    w = w_ref[0]
    reps = w.shape[1] // LANES
    for m in range(2):
        cb = c_ref[m]
        s = cb / (1.0 + jnp.exp(-cb))
        o_ref[0, m:m + 1, :] = jnp.sum(w * jnp.tile(s, (1, reps)), axis=0, keepdims=True) + b_ref[0]


def _modulation(c2, w_mod, b_mod):
    depth, d, n = w_mod.shape
    tn = _pick(n, (512, 256, 128))
    cb = jnp.broadcast_to(c2[:, :, None], (2, d, LANES))
    return pl.pallas_call(
        _mod_kernel,
        grid=(depth, n // tn),
        in_specs=[pl.BlockSpec((2, d, LANES), lambda l, j: (0, 0, 0)),
                  pl.BlockSpec((1, d, tn), lambda l, j: (l, 0, j)),
                  pl.BlockSpec((1, 1, tn), lambda l, j: (l, 0, j))],
        out_specs=pl.BlockSpec((1, 2, tn), lambda l, j: (l, 0, j)),
        out_shape=jax.ShapeDtypeStruct((depth, 2, n), F32),
        compiler_params=_cparams(("arbitrary", "arbitrary")),
        name="modulation",
    )(cb, w_mod, b_mod.reshape(depth, 1, n))


def _proj_kernel(x_ref, mod_ref, *refs, n_ctx, tm):
    is_ctx = _row_is_ctx(pl.program_id(0), tm, n_ctx)
    xn = _layer_norm(x_ref[...])
    h = (xn * (1.0 + _mod_row(mod_ref, 1, is_ctx)) + _mod_row(mod_ref, 0, is_ctx)).astype(BF16)
    n_groups = len(refs) // 2
    for w_ref, o_ref in zip(refs[:n_groups], refs[n_groups:]):
        o_ref[...] = _dot(h, w_ref[...]).astype(o_ref.dtype)


def _project(xa, mod, weights, out_dtypes, n_ctx):
    tt, d = xa.shape
    tm = _pick(tt, (256, 128))
    w_specs = [pl.BlockSpec(w.shape, lambda i: (0, 0), pipeline_mode=pl.Buffered(1)) for w in weights]
    return pl.pallas_call(
        functools.partial(_proj_kernel, n_ctx=n_ctx, tm=tm),
        grid=(tt // tm,),
        in_specs=[pl.BlockSpec((tm, d), lambda i: (i, 0)),
                  pl.BlockSpec((2, 6, d), lambda i: (0, 0, 0))] + w_specs,
        out_specs=[pl.BlockSpec((tm, w.shape[1]), lambda i: (i, 0)) for w in weights],
        out_shape=[jax.ShapeDtypeStruct((tt, w.shape[1]), dt) for w, dt in zip(weights, out_dtypes)],
        compiler_params=_cparams(("arbitrary",)),
        name="ln_mod_project",
    )(xa, mod, *weights)


def _na_kernel(q_ref, k_ref, v_ref, bias_ref, o_ref, *, n_ctx, n_band_starts):
    qb = pl.program_id(1)
    scale = NA_HEAD_DIM ** -0.5
    lane = lax.broadcasted_iota(jnp.int32, (1, LANES), 1)
    heads_per_tile = LANES // NA_HEAD_DIM

    def attend(pair, band_start):
        cols = slice(pair * LANES, (pair + 1) * LANES)
        q = q_ref[:, cols]
        kc, vc = k_ref[0:n_ctx, cols], v_ref[0:n_ctx, cols]
        if band_start is not None:
            kb, vb = k_ref[pl.ds(band_start, NA_KBLK), cols], v_ref[pl.ds(band_start, NA_KBLK), cols]
        outs = []
        for h in range(heads_per_tile):
            qh = jnp.where((lane // NA_HEAD_DIM) == h, q, jnp.zeros_like(q))
            sc = _dot_nt(qh, kc) * scale
            m = jnp.max(sc, axis=-1, keepdims=True)
            if band_start is not None:
                sw = _dot_nt(qh, kb) * scale + bias_ref[0, pair * heads_per_tile + h]
                m = jnp.maximum(m, jnp.max(sw, axis=-1, keepdims=True))
            pc = jnp.exp(sc - m)
            l = jnp.sum(pc, axis=-1, keepdims=True)
            o = _dot(pc.astype(BF16), vc)
            if band_start is not None:
                pw = jnp.exp(sw - m)
                l = l + jnp.sum(pw, axis=-1, keepdims=True)
                o = o + _dot(pw.astype(BF16), vb)
            outs.append(o / l)
        o_ref[:, cols] = jnp.where((lane // NA_HEAD_DIM) == 0, outs[0], outs[1]).astype(o_ref.dtype)

    @pl.when(qb == 0)
    def _():
        for pair in range(NA_STEP_HEADS // heads_per_tile):
            attend(pair, None)

    @pl.when(qb > 0)
    def _():
        start_blk = jnp.clip(qb - 2, 0, n_band_starts - 1)
        start = pl.multiple_of(n_ctx + start_blk * NA_QBLK, NA_QBLK)
        for pair in range(NA_STEP_HEADS // heads_per_tile):
            attend(pair, start)


def _na_bias(rpb):
    a = np.arange(NA_QROWS)[:, None]
    j = np.arange(NA_BAND)[None, :]
    ws = [np.zeros_like(a), a, np.full_like(a, NA_BAND - NA_WIN_H)]
    off = [0, -NA_WIN_H // 2, -(NA_QROWS + NA_WIN_H // 2)]
    qc = np.arange(GRID_W)[:, None]
    kc = np.arange(GRID_W)[None, :]
    wcs = np.clip(qc - NA_WIN_W // 2, 0, GRID_W - NA_WIN_W)
    col_ok = (kc >= wcs) & (kc < wcs + NA_WIN_W)
    depth, heads = rpb.shape[:2]
    n_dr, n_dc = 2 * NA_WIN_H - 1, 2 * NA_WIN_W - 1
    period = 2 * GRID_W + 1
    t = np.arange(period)
    dc = np.where(t <= GRID_W, t, t - period) + NA_WIN_W - 1
    u = jnp.where((dc >= 0) & (dc < n_dc), rpb[..., np.clip(dc, 0, n_dc - 1)], NEG)
    tiles = jnp.tile(u, (1, 1, 1, GRID_W))[..., :GRID_W * 2 * GRID_W]
    tiles = tiles.reshape(depth, heads, n_dr, GRID_W, 2 * GRID_W)[..., :GRID_W]
    tiles = jnp.where(col_ok, tiles, NEG)
    tiles = jnp.concatenate([tiles, jnp.full((depth, heads, 1, GRID_W, GRID_W), NEG, F32)], axis=2)
    dr = []
    for p in range(3):
        row_ok = (j >= ws[p]) & (j < ws[p] + NA_WIN_H)
        dr.append(np.where(row_ok, j - a + off[p] + NA_WIN_H - 1, n_dr))
    b = tiles[:, :, np.stack(dr)]
    return b.transpose(0, 2, 1, 3, 5, 4, 6).reshape(depth, 3, heads, NA_QBLK, NA_KBLK)


def _neighbourhood_attention(p_na, bias, n_ctx):
    tt = p_na.shape[0]
    n_qb = tt // NA_QBLK
    n_lat_blk = n_qb - 1
    n_band_starts = n_lat_blk - NA_BAND // NA_QROWS + 1
    hp = NA_HEADS // NA_STEP_HEADS
    w = NA_STEP_HEADS * NA_HEAD_DIM

    def bias_idx(h, qb):
        pat = jnp.where(qb <= 1, 0, jnp.where(qb == n_qb - 1, 2, 1))
        return (pat, h, 0, 0)

    return pl.pallas_call(
        functools.partial(_na_kernel, n_ctx=n_ctx, n_band_starts=n_band_starts),
        grid=(hp, n_qb),
        in_specs=[pl.BlockSpec((NA_QBLK, w), lambda h, qb: (qb, h)),
                  pl.BlockSpec((tt, w), lambda h, qb: (0, hp + h)),
                  pl.BlockSpec((tt, w), lambda h, qb: (0, 2 * hp + h)),
                  pl.BlockSpec((1, NA_STEP_HEADS, NA_QBLK, NA_KBLK), bias_idx)],
        out_specs=pl.BlockSpec((NA_QBLK, w), lambda h, qb: (qb, h)),
        out_shape=jax.ShapeDtypeStruct((tt, NA_W), BF16),
        compiler_params=_cparams(("arbitrary", "arbitrary")),
        name="neighbourhood_attention",
    )(p_na, p_na, p_na, bias)


def _log_sigmoid(x):
    return jnp.minimum(x, 0.0) - jnp.log(1.0 + jnp.exp(-jnp.abs(x)))


def _gla_chunk(q, k, v, lr, wg, bg, tri, causal, s_ref, last_row):
    x = jnp.dot(lr, wg, preferred_element_type=F32, precision=lax.Precision.HIGHEST) + bg
    la = _log_sigmoid(x) * (1.0 / GLA_GATE_NORM)
    b = jnp.dot(tri, la, preferred_element_type=F32, precision=lax.Precision.HIGHEST)
    b_last = b[last_row:last_row + 1, :]
    q_t = (q * (GLA_DK ** -0.5) * jnp.exp(b)).astype(BF16)
    k_t = (k * jnp.exp(-b)).astype(BF16)
    k_end = (k * jnp.exp(b_last - b)).astype(BF16)
    dec = jnp.exp(b_last)
    outs = []
    for h in range(GLA_HEADS):
        ks = slice(h * GLA_DK, (h + 1) * GLA_DK)
        vh = v[:, h * GLA_DV:(h + 1) * GLA_DV].astype(BF16)
        att = jnp.where(causal, _dot_nt(q_t[:, ks], k_t[:, ks]), 0.0)
        st = s_ref[h]
        o = _dot(att.astype(BF16), vh) + _dot_nt(q_t[:, ks], st.astype(BF16))
        s_ref[h] = st * dec[:, ks] + _dot_tn(vh, k_end[:, ks])
        outs.append(o)
    return jnp.concatenate(outs, axis=-1)


def _gla_kernel(qf_ref, kf_ref, vf_ref, lf_ref, qb_ref, kb_ref, vb_ref, lb_ref, wg_ref, bg_ref,
                of_ref, ob_ref, sf_ref, sb_ref):
    @pl.when(pl.program_id(0) == 0)
    def _():
        sf_ref[...] = jnp.zeros_like(sf_ref)
        sb_ref[...] = jnp.zeros_like(sb_ref)

    r = lax.broadcasted_iota(jnp.int32, (GLA_CHUNK, GLA_CHUNK), 0)
    c = lax.broadcasted_iota(jnp.int32, (GLA_CHUNK, GLA_CHUNK), 1)
    lower = r >= c
    upper = r <= c
    tri_f = lower.astype(F32)
    tri_b = upper.astype(F32)
    n_chunks = GLA_BLK // GLA_CHUNK
    for ci in range(n_chunks):
        rows = slice(ci * GLA_CHUNK, (ci + 1) * GLA_CHUNK)
        of_ref[rows, :] = _gla_chunk(qf_ref[rows, :], kf_ref[rows, :], vf_ref[rows, :], lf_ref[rows, :],
                                     wg_ref[0], bg_ref[0], tri_f, lower, sf_ref, GLA_CHUNK - 1)
        cb = n_chunks - 1 - ci
        rows = slice(cb * GLA_CHUNK, (cb + 1) * GLA_CHUNK)
        ob_ref[rows, :] = _gla_chunk(qb_ref[rows, :], kb_ref[rows, :], vb_ref[rows, :], lb_ref[rows, :],
                                     wg_ref[1], bg_ref[1], tri_b, upper, sb_ref, 0)


def _gla(p_gla, wg, bg):
    tt = p_gla.shape[0]
    nb = tt // GLA_BLK
    fwd = lambda i: i
    bwd = lambda i: jnp.where(i == 0, 0, nb - i)
    lr_blk = (2 * GLA_KW + 2 * GLA_VW) // 128

    def specs(rowmap):
        return [pl.BlockSpec((GLA_BLK, GLA_KW), lambda i: (rowmap(i), 0)),
                pl.BlockSpec((GLA_BLK, GLA_KW), lambda i: (rowmap(i), 1)),
                pl.BlockSpec((GLA_BLK, GLA_VW), lambda i: (rowmap(i), 1)),
                pl.BlockSpec((GLA_BLK, 128), lambda i: (rowmap(i), lr_blk))]

    return pl.pallas_call(
        _gla_kernel,
        grid=(nb,),
        in_specs=specs(fwd) + specs(bwd) + [
            pl.BlockSpec((2, 128, GLA_KW), lambda i: (0, 0, 0)),
            pl.BlockSpec((2, 1, GLA_KW), lambda i: (0, 0, 0))],
        out_specs=[pl.BlockSpec((GLA_BLK, GLA_VW), lambda i: (fwd(i), 0)),
                   pl.BlockSpec((GLA_BLK, GLA_VW), lambda i: (bwd(i), 0))],
        out_shape=[jax.ShapeDtypeStruct((tt, GLA_VW), F32)] * 2,
        scratch_shapes=[pltpu.VMEM((GLA_HEADS, GLA_DV, GLA_DK), F32)] * 2,
        compiler_params=_cparams(("arbitrary",)),
        name="gla_bidirectional",
    )(*([p_gla] * 8), wg, bg)


def _group_sum_sq(x, ones_bd):
    sq = x * x
    hi = sq.astype(BF16)
    lo = (sq - hi.astype(F32)).astype(BF16)
    return _dot(hi, ones_bd) + _dot(lo, ones_bd)


def _gqa_prep_kernel(p_ref, ta_ref, tb_ref, ones_ref, qt_ref, kz_ref, vt_ref):
    d = GQA_HEAD_DIM
    x = p_ref[...]
    q, qs = x[:, 0:GQA_QW], x[:, GQA_QW:2 * GQA_QW]
    o = 2 * GQA_QW
    k, ks, v = x[:, o:o + GQA_KVW], x[:, o + GQA_KVW:o + 2 * GQA_KVW], x[:, o + 2 * GQA_KVW:o + 3 * GQA_KVW]
    ta, tb = ta_ref[...], tb_ref[...]
    taq = jnp.tile(ta[:, 0:128], (1, GQA_QW // 128))
    tbq = jnp.tile(tb[:, 0:128], (1, GQA_QW // 128))
    rq = lax.rsqrt(_group_sum_sq(q, ones_ref[...]) * (1.0 / d) + EPS)
    qr = rq * (q * taq + qs * tbq)
    rk = lax.rsqrt(_group_sum_sq(k, ones_ref[0:GQA_KVW, 0:GQA_KVW]) * (1.0 / d) + EPS)
    kr = rk * (k * ta[:, 128:256] + ks * tb[:, 128:256])
    qt_ref[...] = qr.T.astype(BF16)
    lane = lax.broadcasted_iota(jnp.int32, (1, GQA_KVW), 1)
    for g in range(GQA_KV_HEADS):
        kz_ref[g] = jnp.where((lane // d) == g, kr, 0.0).astype(BF16)
    vt_ref[...] = v.T.astype(BF16)


def _gqa_prep(p_gqa, ta, tb):
    tt, n = p_gqa.shape
    tm = _pick(tt, (384, 256, 128))
    hid = np.arange(GQA_QW) // GQA_HEAD_DIM
    ones_bd = jnp.asarray(hid[:, None] == hid[None, :], BF16)
    return pl.pallas_call(
        _gqa_prep_kernel,
        grid=(tt // tm,),
        in_specs=[pl.BlockSpec((tm, n), lambda i: (i, 0)),
                  pl.BlockSpec((tm, 256), lambda i: (i, 0)),
                  pl.BlockSpec((tm, 256), lambda i: (i, 0)),
                  pl.BlockSpec((GQA_QW, GQA_QW), lambda i: (0, 0))],
        out_specs=[pl.BlockSpec((GQA_QW, tm), lambda i: (0, i)),
                   pl.BlockSpec((GQA_KV_HEADS, tm, GQA_KVW), lambda i: (0, i, 0)),
                   pl.BlockSpec((GQA_KVW, tm), lambda i: (0, i))],
        out_shape=[jax.ShapeDtypeStruct((GQA_QW, tt), BF16),
                   jax.ShapeDtypeStruct((GQA_KV_HEADS, tt, GQA_KVW), BF16),
                   jax.ShapeDtypeStruct((GQA_KVW, tt), BF16)],
        compiler_params=_cparams(("arbitrary",)),
        name="gqa_prep",
    )(p_gqa, ta, tb, ones_bd)


def _gqa_kernel(qt_ref, kz_ref, vt_ref, o_ref, s0_ref, s1_ref, acc_ref, *, n_ctx):
    qb = pl.program_id(1)
    d = GQA_HEAD_DIM
    group = GQA_HEADS // GQA_KV_HEADS
    kc = GQA_KEY_CHUNK
    tq = qt_ref.shape[1]
    q4 = jnp.concatenate([jnp.concatenate([qt_ref[g * d:(g + 1) * d, :]] * 2, axis=0) for g in range(group)],
                         axis=1)

    def scores(c, s_ref):
        start = pl.multiple_of(c * kc, kc)
        s = _dot(kz_ref[0, pl.ds(start, kc), :], q4)
        s_ref[...] = s
        return jnp.max(s, axis=0, keepdims=True)

    def accumulate(c, s_ref, m_chunk, m):
        m_new = jnp.maximum(m, m_chunk)
        p = jnp.exp2(s_ref[...] - m_new)
        acc_ref[...] = acc_ref[...] * jnp.exp2(m - m_new) + _dot(vt_ref[0, c], p.astype(BF16))
        return m_new

    def attend(n_chunks):
        acc_ref[...] = jnp.zeros_like(acc_ref)
        m = jnp.full((1, group * tq), -jnp.inf, F32)
        mc0 = scores(0, s0_ref)

        def body(j, carry):
            m, mc0 = carry
            for k in range(0, GQA_LOOP_CHUNKS, 2):
                c = GQA_LOOP_CHUNKS * j + k
                mc1 = scores(c + 1, s1_ref)
                m = accumulate(c, s0_ref, mc0, m)
                mc0 = scores(c + 2, s0_ref)
                m = accumulate(c + 1, s1_ref, mc1, m)
            return m, mc0

        m, mc0 = lax.fori_loop(0, (n_chunks - 1) // GQA_LOOP_CHUNKS, body, (m, mc0))
        accumulate(n_chunks - 1, s0_ref, mc0, m)
        out = acc_ref[0:d, :] / acc_ref[d:d + 1, :]
        o_ref[...] = jnp.concatenate([out[:, g * tq:(g + 1) * tq] for g in range(group)],
                                     axis=0).T.astype(o_ref.dtype)

    @pl.when(qb == 0)
    def _():
        attend(n_ctx // kc)

    @pl.when(qb > 0)
    def _():
        attend(kz_ref.shape[1] // kc)


def _gqa_attention(qt, kz, vt, n_ctx):
    tt = qt.shape[1]
    tq = n_ctx
    kc = GQA_KEY_CHUNK
    group = GQA_HEADS // GQA_KV_HEADS
    gw = group * GQA_HEAD_DIM
    assert tt % kc == 0 and n_ctx % kc == 0
    assert (tt // kc) % GQA_LOOP_CHUNKS == 1 and (n_ctx // kc) % GQA_LOOP_CHUNKS == 1
    vt3 = vt.reshape(GQA_KV_HEADS, GQA_HEAD_DIM, tt // kc, kc).transpose(0, 2, 1, 3)
    extra = jnp.zeros((GQA_KV_HEADS, tt // kc, GQA_ONES_ROWS, kc), BF16).at[:, :, 0, :].set(1.0)
    vt3 = jnp.concatenate([vt3, extra], axis=2)
    vrows = GQA_HEAD_DIM + GQA_ONES_ROWS
    return pl.pallas_call(
        functools.partial(_gqa_kernel, n_ctx=n_ctx),
        grid=(GQA_KV_HEADS, tt // tq),
        in_specs=[pl.BlockSpec((gw, tq), lambda g, i: (g, i)),
                  pl.BlockSpec((1, tt, GQA_KVW), lambda g, i: (g, 0, 0)),
                  pl.BlockSpec((1, tt // kc, vrows, kc), lambda g, i: (g, 0, 0, 0))],
        out_specs=pl.BlockSpec((tq, gw), lambda g, i: (i, g)),
        out_shape=jax.ShapeDtypeStruct((tt, GQA_QW), BF16),
        scratch_shapes=[pltpu.VMEM((kc, group * tq), F32), pltpu.VMEM((kc, group * tq), F32),
                        pltpu.VMEM((vrows, group * tq), F32)],
        compiler_params=_cparams(("arbitrary", "arbitrary")),
        name="gqa_attention",
    )(qt, kz, vt3)


def _mix_out_kernel(x_ref, mod_ref, yna_ref, of_ref, ob_ref, gate_ref, ygqa_ref, w_ref, nw_ref,
                    g_ref, b_ref, x1_ref, h2t_ref, *, n_ctx, tm):
    is_ctx = _row_is_ctx(pl.program_id(0), tm, n_ctx)
    o = of_ref[...] + ob_ref[...]
    nw = nw_ref[...]
    parts = []
    for h in range(GLA_HEADS):
        oh = o[:, h * GLA_DV:(h + 1) * GLA_DV]
        parts.append(oh * lax.rsqrt(jnp.mean(oh * oh, axis=-1, keepdims=True) + EPS) * nw)
    gate = gate_ref[...]
    y_gla = (jnp.concatenate(parts, axis=-1) * (gate / (1.0 + jnp.exp(-gate)))).astype(BF16)
    y = (_dot(yna_ref[...], w_ref[0:NA_W, :]) + _dot(y_gla, w_ref[NA_W:NA_W + GLA_VW, :])
         + _dot(ygqa_ref[...], w_ref[NA_W + GLA_VW:, :]))
    x1 = _layer_norm(DEEPNORM_ALPHA * x_ref[...] + _mod_row(mod_ref, 2, is_ctx) * y) * g_ref[...] + b_ref[...]
    x1_ref[...] = x1
    h2 = _layer_norm(x1) * (1.0 + _mod_row(mod_ref, 4, is_ctx)) + _mod_row(mod_ref, 3, is_ctx)
    h2t_ref[...] = h2.T.astype(BF16)


def _mix_out(xa, mod, y_na, o_f, o_b, p_gla, y_gqa, w_out, norm_w, ln_g, ln_b, n_ctx):
    tt, d = xa.shape
    tm = _pick(tt, (384, 256, 128))
    row = lambda i: (i, 0)
    const2 = lambda i: (0, 0)
    return pl.pallas_call(
        functools.partial(_mix_out_kernel, n_ctx=n_ctx, tm=tm),
        grid=(tt // tm,),
        in_specs=[pl.BlockSpec((tm, d), row),
                  pl.BlockSpec((2, 6, d), lambda i: (0, 0, 0)),
                  pl.BlockSpec((tm, NA_W), row),
                  pl.BlockSpec((tm, GLA_VW), row),
                  pl.BlockSpec((tm, GLA_VW), row),
                  pl.BlockSpec((tm, GLA_VW), lambda i: (i, 2)),
                  pl.BlockSpec((tm, GQA_QW), row),
                  pl.BlockSpec(w_out.shape, const2),
                  pl.BlockSpec((1, GLA_DV), const2),
                  pl.BlockSpec((1, d), const2),
                  pl.BlockSpec((1, d), const2)],
        out_specs=[pl.BlockSpec((tm, d), row), pl.BlockSpec((d, tm), lambda i: (0, i))],
        out_shape=[jax.ShapeDtypeStruct((tt, d), F32), jax.ShapeDtypeStruct((d, tt), BF16)],
        compiler_params=_cparams(("arbitrary",)),
        name="mixer_out_postln",
    )(xa, mod, y_na, o_f, o_b, p_gla, y_gqa, w_out, norm_w, ln_g, ln_b)


def _top16_exact(s, key_iota):
    cur = s
    rank = jnp.full(s.shape, float(PEER_TOPK), F32)
    vals = []
    for r in range(PEER_TOPK):
        m = jnp.max(cur, axis=0, keepdims=True)
        idx = jnp.min(jnp.where(cur == m, key_iota, float(PEER_NKEYS)), axis=0, keepdims=True)
        sel = key_iota == idx
        rank = jnp.where(sel, float(r), rank)
        cur = jnp.where(sel, -jnp.inf, cur)
        vals.append(m)
    return jnp.concatenate(vals, axis=0), rank


_CODE_SCALE = 2.0 ** 100
_CODE_STEP = 64.0


def _rank_code(r):
    return -_CODE_SCALE * (1.0 + (r + 1) / _CODE_STEP)


def _top16_fast(s):
    cur = s
    vals = []
    for r in range(PEER_TOPK):
        m = jnp.max(cur, axis=0, keepdims=True)
        cur = jnp.where(cur == m, _rank_code(r), cur)
        vals.append(m)
    return jnp.concatenate(vals, axis=0), cur


def _coded(cur):
    return cur <= _rank_code(0)


def _decode_rank(cur):
    rank = (cur * (-1.0 / _CODE_SCALE) - 1.0) * _CODE_STEP - 1.0
    return jnp.where(_coded(cur), rank, float(PEER_TOPK))


def _cand_layout():
    k = PEER_TOPK
    rows = [(0, rb) for rb in range(k)]
    for ra in range(1, 5):
        rows += [(ra, rb) for rb in range(8)]
    rows += [(ra, 0) for ra in range(8, k)]
    rows += [(ra, 1) for ra in range(8)]
    rows += [(ra, 0) for ra in range(8)]
    seen, valid = set(), []
    for pair in rows:
        valid.append(pair not in seen)
        seen.add(pair)
    needed = {(ra, rb) for ra in range(k) for rb in range(k) if (ra + 1) * (rb + 1) <= k}
    assert needed <= seen
    ra = np.array([p[0] for p in rows], np.float32)
    rb = np.array([p[1] for p in rows], np.float32)
    return ra, rb, np.array(valid)


def _select_pairs(a, b, c_ra, c_flat, c_ok, exact):
    k = PEER_TOPK
    tt = a.shape[1]
    blocks = [a[0:1, :] + b]
    for ra in range(1, 5):
        blocks.append(a[ra:ra + 1, :] + b[0:8, :])
    blocks.append(a[8:k, :] + b[0:1, :])
    blocks.append(a[0:8, :] + b[1:2, :])
    blocks.append(a[0:8, :] + b[0:1, :])
    cand = jnp.where(c_ok, jnp.concatenate(blocks, axis=0), -jnp.inf)
    m0 = a[0:1, :] + b[0:1, :]
    z = jnp.zeros((1, tt), F32)
    if exact:
        rank_iota = lax.broadcasted_iota(jnp.int32, (k, tt), 0).astype(F32)
        cnt = jnp.zeros((k, tt), F32)
        for _ in range(k):
            m = jnp.max(cand, axis=0, keepdims=True)
            idx = jnp.min(jnp.where(cand == m, c_flat, 1e9), axis=0, keepdims=True)
            sel = c_flat == idx
            ra_sel = jnp.sum(jnp.where(sel, c_ra, 0.0), axis=0, keepdims=True)
            cnt = cnt + jnp.where(rank_iota == ra_sel, 1.0, 0.0)
            z = z + jnp.exp(m - m0)
            cand = jnp.where(sel, -jnp.inf, cand)
        return cnt, z, None
    for _ in range(k):
        m = jnp.max(cand, axis=0, keepdims=True)
        z = z + jnp.exp(m - m0)
        cand = jnp.where(cand == m, -jnp.inf, cand)
    sel = jnp.where(jnp.logical_and(c_ok, cand == -jnp.inf), 1.0, 0.0)
    low = sel[56:64, :] + sel[64:72, :]
    row8 = lax.broadcasted_iota(jnp.int32, (8, tt), 0)
    sums = [jnp.sum(sel[0:16, :], axis=0, keepdims=True)]
    sums += [jnp.sum(sel[16 + 8 * i:24 + 8 * i, :], axis=0, keepdims=True) for i in range(4)]
    for r, v in enumerate(sums):
        low = jnp.where(row8 == r, v, low)
    cnt = jnp.concatenate([low, sel[48:56, :]], axis=0)
    return cnt, z, jnp.sum(sel, axis=0, keepdims=True)


def _route_head(s1, s2, key_iota, c_ra, c_flat, c_ok, exact):
    k = PEER_TOPK
    if exact:
        a, r1 = _top16_exact(s1, key_iota)
        b, r2 = _top16_exact(s2, key_iota)
    else:
        a, cur1 = _top16_fast(s1)
        b, cur2 = _top16_fast(s2)
        r2 = _decode_rank(cur2)
    cnt, z, n_sel = _select_pairs(a, b, c_ra, c_flat, c_ok, exact)
    n1 = jnp.zeros(s1.shape, F32)
    for r in range(k):
        hit = (r1 == float(r)) if exact else (s1 == a[r:r + 1, :])
        n1 = jnp.where(hit, cnt[r:r + 1, :], n1)
    outs = (r2, jnp.exp(s2 - b[0:1, :]) / z, n1, jnp.exp(s1 - a[0:1, :]))
    if exact:
        return outs, None
    n_coded = lambda cur: jnp.sum(jnp.where(_coded(cur), 1.0, 0.0), axis=0, keepdims=True)
    clean = jnp.logical_and(n_sel == float(k),
                            jnp.logical_and(n_coded(cur1) == float(k), n_coded(cur2) == float(k)))
    return outs, jnp.where(clean, 0.0, 1.0)


def _route_kernel(h_ref, wq_ref, sk_ref, cst_ref, r2_ref, e2_ref, n1_ref, e1_ref):
    tt = h_ref.shape[1]
    key_iota = lax.broadcasted_iota(jnp.int32, (PEER_NKEYS, tt), 0).astype(F32)
    cst = cst_ref[...]
    n_rows = cst.shape[0]
    c_ra = jnp.broadcast_to(cst[:, 0:1], (n_rows, tt))
    c_flat = jnp.broadcast_to(cst[:, 1:2], (n_rows, tt))
    c_ok = jnp.broadcast_to(cst[:, 2:3], (n_rows, tt)) > 0.5
    half = PEER_KEY_DIM // 2

    def head(h, qt_h, exact):
        s1 = _dot(sk_ref[h, 0], qt_h[0:half, :].astype(BF16))
        s2 = _dot(sk_ref[h, 1], qt_h[half:2 * half, :].astype(BF16))
        outs, bad = _route_head(s1, s2, key_iota, c_ra, c_flat, c_ok, exact)
        for ref, val in zip((r2_ref, e2_ref, n1_ref, e1_ref), outs):
            ref[h] = val.astype(ref.dtype)
        return bad

    qt = _dot(wq_ref[...], h_ref[...])
    tied = [jnp.max(head(h, qt[h * PEER_KEY_DIM:(h + 1) * PEER_KEY_DIM, :], False)) for h in range(PEER_HEADS)]
    for h in range(PEER_HEADS):
        @pl.when(tied[h] > 0.0)
        def _(h=h):
            head(h, _dot(wq_ref[h * PEER_KEY_DIM:(h + 1) * PEER_KEY_DIM, :], h_ref[...]), True)


def _peer_route(h2t, wq_t, sub_keys):
    d, tt = h2t.shape
    tb = _pick(tt, (256, 128))
    ra, rb, valid = _cand_layout()
    cst = np.zeros((ra.shape[0], LANES), np.float32)
    flat = np.where(valid, ra * PEER_TOPK + rb, 1000.0 + np.arange(ra.shape[0]))
    cst[:, 0], cst[:, 1], cst[:, 2] = ra, flat, valid
    shp = (PEER_HEADS, PEER_NKEYS, tt)
    out_spec = pl.BlockSpec((PEER_HEADS, PEER_NKEYS, tb), lambda i: (0, 0, i))
    return pl.pallas_call(
        _route_kernel,
        grid=(tt // tb,),
        in_specs=[pl.BlockSpec((d, tb), lambda i: (0, i)),
                  pl.BlockSpec(wq_t.shape, lambda i: (0, 0)),
                  pl.BlockSpec(sub_keys.shape, lambda i: (0, 0, 0, 0)),
                  pl.BlockSpec(cst.shape, lambda i: (0, 0))],
        out_specs=[out_spec] * 4,
        out_shape=[jax.ShapeDtypeStruct(shp, dt) for dt in (BF16, BF16, F32, F32)],
        compiler_params=_cparams(("arbitrary",)),
        name="peer_route",
    )(h2t, wq_t, sub_keys, jnp.asarray(cst))


def _gelu_tanh(x):
    c = math.sqrt(2.0 / math.pi)
    hx = 0.5 * x
    return hx * jnp.tanh(x * (c + (c * 0.044715) * (x * x))) + hx


def _peer_kernel(h_ref, u_ref, vt_ref, r2_ref, e2_ref, n1_ref, e1_ref, o_ref):
    e = pl.program_id(1)
    eb = u_ref.shape[0]
    n_i = eb // PEER_NKEYS

    @pl.when(e == 0)
    def _():
        o_ref[...] = jnp.zeros_like(o_ref)

    n_sub = eb // PEER_SUB
    sub_rows = lambda s: slice(s * PEER_SUB, (s + 1) * PEER_SUB)
    pre = _dot(u_ref[sub_rows(0), :].astype(BF16), h_ref[...])
    for s in range(n_sub):
        rows = sub_rows(s)
        cur = pre
        if s + 1 < n_sub:
            pre = _dot(u_ref[sub_rows(s + 1), :].astype(BF16), h_ref[...])
        act = _gelu_tanh(cur)
        parts = []
        for ii in range(PEER_SUB // PEER_NKEYS):
            i = e * n_i + s * (PEER_SUB // PEER_NKEYS) + ii
            tb = act.shape[1]
            tiles = PEER_NKEYS // BF16_SUBLANES
            g = None
            for h in range(PEER_HEADS):
                n1 = jnp.broadcast_to(n1_ref[h, pl.ds(i, 1), :], (BF16_SUBLANES, tb)).astype(BF16)
                e1 = jnp.broadcast_to(e1_ref[h, pl.ds(i, 1), :], (BF16_SUBLANES, tb)).astype(BF16)
                r2t = r2_ref[h].reshape(tiles, BF16_SUBLANES, tb)
                e2t = e2_ref[h].reshape(tiles, BF16_SUBLANES, tb)
                t = jnp.where(r2t < n1[None], e2t, jnp.zeros_like(e2t)) * e1[None]
                g = t if g is None else g + t
            a16 = act[ii * PEER_NKEYS:(ii + 1) * PEER_NKEYS, :].astype(BF16).reshape(tiles, BF16_SUBLANES, tb)
            parts.append((a16 * g).reshape(PEER_NKEYS, tb))
        o_ref[...] += _dot(vt_ref[:, rows], jnp.concatenate(parts, axis=0))


def _peer_experts(h2t, u_tab, vt_tab, r2, e2, n1, e1):
    d, tt = h2t.shape
    n_exp = u_tab.shape[0]
    tb = _pick(tt, (768, 512, 256))
    eb = PEER_EXPERT_BLK
    route_spec = pl.BlockSpec((PEER_HEADS, PEER_NKEYS, tb), lambda i, e: (0, 0, i),
                              pipeline_mode=pl.Buffered(1))
    return pl.pallas_call(
        _peer_kernel,
        grid=(tt // tb, n_exp // eb),
        in_specs=[pl.BlockSpec((d, tb), lambda i, e: (0, i), pipeline_mode=pl.Buffered(1)),
                  pl.BlockSpec((eb, d), lambda i, e: (e, 0)),
                  pl.BlockSpec((d, eb), lambda i, e: (0, e)),
                  route_spec, route_spec, route_spec, route_spec],
        out_specs=pl.BlockSpec((d, tb), lambda i, e: (0, i)),
        out_shape=jax.ShapeDtypeStruct((d, tt), F32),
        compiler_params=_cparams(("arbitrary", "arbitrary")),
        name="peer_experts",
    )(h2t, u_tab, vt_tab, r2, e2, n1, e1)


def _peer_out_kernel(x_ref, mod_ref, pt_ref, g_ref, b_ref, o_ref, *, n_ctx, tm):
    is_ctx = _row_is_ctx(pl.program_id(0), tm, n_ctx)
    y = pt_ref[...].T
    o_ref[...] = (_layer_norm(DEEPNORM_ALPHA * x_ref[...] + _mod_row(mod_ref, 5, is_ctx) * y)
                  * g_ref[...] + b_ref[...])


def _peer_out(x1, mod, peer_t, ln_g, ln_b, n_ctx):
    tt, d = x1.shape
    tm = _pick(tt, (384, 256, 128))
    return pl.pallas_call(
        functools.partial(_peer_out_kernel, n_ctx=n_ctx, tm=tm),
        grid=(tt // tm,),
        in_specs=[pl.BlockSpec((tm, d), lambda i: (i, 0)),
                  pl.BlockSpec((2, 6, d), lambda i: (0, 0, 0)),
                  pl.BlockSpec((d, tm), lambda i: (0, i)),
                  pl.BlockSpec((1, d), lambda i: (0, 0)),
                  pl.BlockSpec((1, d), lambda i: (0, 0))],
        out_specs=pl.BlockSpec((tm, d), lambda i: (i, 0)),
        out_shape=jax.ShapeDtypeStruct((tt, d), F32),
        compiler_params=_cparams(("arbitrary",)),
        name="peer_out_postln",
    )(x1, mod, peer_t, ln_g, ln_b)


def _rope_tables(n_ctx, n_lat):
    half = GQA_HEAD_DIM // 2
    inv = ROPE_BASE ** (-jnp.arange(0, half, 2, dtype=F32) / half)
    t = jnp.arange(n_lat)
    row = (t // GRID_W).astype(F32)
    col = (t % GRID_W).astype(F32)
    ang = jnp.concatenate([row[:, None] * inv, col[:, None] * inv], axis=-1)
    cos, sin = jnp.cos(ang), jnp.sin(ang)
    cos2 = jnp.concatenate([jnp.ones((n_ctx, 2 * half), F32), jnp.concatenate([cos, cos], -1)], 0)
    sin2 = jnp.concatenate([jnp.zeros((n_ctx, 2 * half), F32), jnp.concatenate([-sin, sin], -1)], 0)
    return cos2, sin2


_EVEN_ODD = np.concatenate([np.arange(0, GQA_HEAD_DIM, 2), np.arange(1, GQA_HEAD_DIM, 2)])
_ODD_EVEN = np.concatenate([np.arange(1, GQA_HEAD_DIM, 2), np.arange(0, GQA_HEAD_DIM, 2)])


def _head_cols(perm, n_heads):
    return np.concatenate([h * GQA_HEAD_DIM + perm for h in range(n_heads)])


def kernel(x, c, ctx, c_ctx, w_mod, b_mod, w_in, na_rpb, gla_gate_w, gla_gate_b, gla_norm_w,
           gqa_q_norm, gqa_k_norm, w_out, ln1_g, ln1_b, peer_wq, peer_subkeys, peer_u, peer_v,
           ln2_g, ln2_b):
    batch, n_lat, d = x.shape
    assert batch == 1 and d == D_MODEL
    n_ctx = ctx.shape[1]
    assert n_lat % NA_QBLK == 0 and n_lat // GRID_W >= NA_BAND + NA_QROWS
    depth = w_mod.shape[0]

    xa = jnp.concatenate([ctx[0], x[0]], axis=0)
    mods = _modulation(jnp.stack([c_ctx, c[0]]), w_mod, b_mod).reshape(depth, 2, 6, d)

    o = np.cumsum([0, NA_W, NA_W, NA_W, GLA_KW, GLA_KW, GLA_VW, GLA_VW, 2 * GLA_GATE_RANK,
                   GQA_QW, GQA_KVW, GQA_KVW])
    w_na = w_in[:, :, o[0]:o[3]].astype(BF16)
    w_gla = jnp.concatenate([w_in[:, :, o[3]:o[8]],
                             jnp.zeros((depth, d, 128 - 2 * GLA_GATE_RANK), F32)], -1).astype(BF16)
    wq_c, wk_c, wv_c = w_in[:, :, o[8]:o[9]], w_in[:, :, o[9]:o[10]], w_in[:, :, o[10]:o[11]]
    w_gqa = jnp.concatenate([wq_c[:, :, _head_cols(_EVEN_ODD, GQA_HEADS)],
                             wq_c[:, :, _head_cols(_ODD_EVEN, GQA_HEADS)],
                             wk_c[:, :, _head_cols(_EVEN_ODD, GQA_KV_HEADS)],
                             wk_c[:, :, _head_cols(_ODD_EVEN, GQA_KV_HEADS)],
                             wv_c], -1).astype(BF16)
    wg = jnp.zeros((depth, 2, 128, GLA_KW), F32)
    wg = wg.at[:, 0, 0:GLA_GATE_RANK].set(gla_gate_w[:, 0])
    wg = wg.at[:, 1, GLA_GATE_RANK:2 * GLA_GATE_RANK].set(gla_gate_w[:, 1])
    bg = gla_gate_b.reshape(depth, 2, 1, GLA_KW)
    w_out_b = w_out.astype(BF16)
    wq_t = jnp.swapaxes(peer_wq, 1, 2).astype(BF16)
    sub_keys = peer_subkeys.astype(BF16)
    u_tab = peer_u
    vt_tab = jnp.swapaxes(peer_v, 1, 2).astype(BF16)

    cos2, sin2 = _rope_tables(n_ctx, n_lat)
    q_scale = GQA_HEAD_DIM ** -0.5 * math.log2(math.e)
    na_bias = _na_bias(na_rpb)

    for l in range(depth):
        mod = mods[l]
        p_na, p_gla, p_gqa = _project(xa, mod, (w_na[l], w_gla[l], w_gqa[l]), (BF16, F32, F32), n_ctx)

        y_na = _neighbourhood_attention(p_na, na_bias[l], n_ctx)
        o_f, o_b = _gla(p_gla, wg[l], bg[l])

        wqn, wkn = gqa_q_norm[l], gqa_k_norm[l]
        ta = jnp.concatenate([jnp.tile(cos2 * wqn[_EVEN_ODD] * q_scale, (1, 2)),
                              jnp.tile(cos2 * wkn[_EVEN_ODD], (1, 2))], -1)
        tb = jnp.concatenate([jnp.tile(sin2 * wqn[_ODD_EVEN] * q_scale, (1, 2)),
                              jnp.tile(sin2 * wkn[_ODD_EVEN], (1, 2))], -1)
        qt, kz, vt = _gqa_prep(p_gqa, ta, tb)
        y_gqa = _gqa_attention(qt, kz, vt, n_ctx)

        x1, h2t = _mix_out(xa, mod, y_na, o_f, o_b, p_gla, y_gqa, w_out_b[l],
                           gla_norm_w[l].reshape(1, GLA_DV), ln1_g[l].reshape(1, d), ln1_b[l].reshape(1, d), n_ctx)
        r2, e2, n1, e1 = _peer_route(h2t, wq_t[l], sub_keys[l])
        peer_t = _peer_experts(h2t, u_tab[l], vt_tab[l], r2, e2, n1, e1)
        xa = _peer_out(x1, mod, peer_t, ln2_g[l].reshape(1, d), ln2_b[l].reshape(1, d), n_ctx)

    return xa[n_ctx:][None]
```

```python
import functools
import math

import numpy as np
import jax
import jax.numpy as jnp
from jax import lax
from jax.experimental import pallas as pl
from jax.experimental.pallas import tpu as pltpu

F32 = jnp.float32
BF16 = jnp.bfloat16

D_MODEL = 2048
DEPTH = 4
GRID_W = 64
EPS = 1e-6

NA_HEADS = 8
NA_HEAD_DIM = 64
NA_WIN_H = 8
NA_WIN_W = 16
GLA_HEADS = 4
GLA_DK = 128
GLA_DV = 256
GLA_GATE_RANK = 16
GLA_GATE_NORM = 16.0
GLA_CHUNK = 64
GQA_HEADS = 8
GQA_KV_HEADS = 2
GQA_HEAD_DIM = 64
ROPE_BASE = 10000.0
PEER_HEADS = 8
PEER_NKEYS = 128
PEER_KEY_DIM = 256
PEER_TOPK = 16

NA_W = NA_HEADS * NA_HEAD_DIM
GLA_KW = GLA_HEADS * GLA_DK
GLA_VW = GLA_HEADS * GLA_DV
GQA_QW = GQA_HEADS * GQA_HEAD_DIM
GQA_KVW = GQA_KV_HEADS * GQA_HEAD_DIM
DEEPNORM_ALPHA = (2.0 * DEPTH) ** 0.25

LANES = 128
BF16_SUBLANES = 16
VMEM_LIMIT_BYTES = 58 * 1024 * 1024

NA_QROWS = 4
NA_BAND = NA_QROWS + NA_WIN_H
NA_QBLK = NA_QROWS * GRID_W
NA_KBLK = NA_BAND * GRID_W
NA_STEP_HEADS = 4
GLA_BLK = 256
GQA_KEY_CHUNK = 256
GQA_LOOP_CHUNKS = 8
GQA_ONES_ROWS = 16
PEER_EXPERT_BLK = 1024
PEER_SUB = 512
NEG = -1e30


def _cparams(sem):
    return pltpu.CompilerParams(dimension_semantics=sem, vmem_limit_bytes=VMEM_LIMIT_BYTES)


def _pick(n, cands):
    for c in cands:
        if n % c == 0:
            return c
    raise ValueError(f"no block size in {cands} divides {n}")


def _dot(a, b):
    return jnp.dot(a, b, preferred_element_type=F32)


def _dot_nt(a, b):
    return lax.dot_general(a, b, (((1,), (1,)), ((), ())), preferred_element_type=F32)


def _dot_tn(a, b):
    return lax.dot_general(a, b, (((0,), (0,)), ((), ())), preferred_element_type=F32)


def _layer_norm(x):
    mu = jnp.mean(x, axis=-1, keepdims=True)
    xc = x - mu
    var = jnp.mean(xc * xc, axis=-1, keepdims=True)
    return xc * lax.rsqrt(var + EPS)


def _row_is_ctx(block_idx, tm, n_ctx):
    rows = block_idx * tm + lax.broadcasted_iota(jnp.int32, (tm, 1), 0)
    return rows < n_ctx


def _mod_row(mod_ref, idx, is_ctx):
    return jnp.where(is_ctx, mod_ref[0, idx:idx + 1, :], mod_ref[1, idx:idx + 1, :])


def _mod_kernel(c_ref, w_ref, b_ref, o_ref):
    w = w_ref[0]
    reps = w.shape[1] // LANES
    for m in range(2):
        cb = c_ref[m]
        s = cb / (1.0 + jnp.exp(-cb))
        o_ref[0, m:m + 1, :] = jnp.sum(w * jnp.tile(s, (1, reps)), axis=0, keepdims=True) + b_ref[0]


def _modulation(c2, w_mod, b_mod):
    depth, d, n = w_mod.shape
    tn = _pick(n, (512, 256, 128))
    cb = jnp.broadcast_to(c2[:, :, None], (2, d, LANES))
    return pl.pallas_call(
        _mod_kernel,
        grid=(depth, n // tn),
        in_specs=[pl.BlockSpec((2, d, LANES), lambda l, j: (0, 0, 0)),
                  pl.BlockSpec((1, d, tn), lambda l, j: (l, 0, j)),
                  pl.BlockSpec((1, 1, tn), lambda l, j: (l, 0, j))],
        out_specs=pl.BlockSpec((1, 2, tn), lambda l, j: (l, 0, j)),
        out_shape=jax.ShapeDtypeStruct((depth, 2, n), F32),
        compiler_params=_cparams(("arbitrary", "arbitrary")),
        name="modulation",
    )(cb, w_mod, b_mod.reshape(depth, 1, n))


def _proj_kernel(x_ref, mod_ref, *refs, n_ctx, tm):
    is_ctx = _row_is_ctx(pl.program_id(0), tm, n_ctx)
    xn = _layer_norm(x_ref[...])
    h = (xn * (1.0 + _mod_row(mod_ref, 1, is_ctx)) + _mod_row(mod_ref, 0, is_ctx)).astype(BF16)
    n_groups = len(refs) // 2
    for w_ref, o_ref in zip(refs[:n_groups], refs[n_groups:]):
        o_ref[...] = _dot(h, w_ref[...]).astype(o_ref.dtype)


def _project(xa, mod, weights, out_dtypes, n_ctx):
    tt, d = xa.shape
    tm = _pick(tt, (256, 128))
    w_specs = [pl.BlockSpec(w.shape, lambda i: (0, 0), pipeline_mode=pl.Buffered(1)) for w in weights]
    return pl.pallas_call(
        functools.partial(_proj_kernel, n_ctx=n_ctx, tm=tm),
        grid=(tt // tm,),
        in_specs=[pl.BlockSpec((tm, d), lambda i: (i, 0)),
                  pl.BlockSpec((2, 6, d), lambda i: (0, 0, 0))] + w_specs,
        out_specs=[pl.BlockSpec((tm, w.shape[1]), lambda i: (i, 0)) for w in weights],
        out_shape=[jax.ShapeDtypeStruct((tt, w.shape[1]), dt) for w, dt in zip(weights, out_dtypes)],
        compiler_params=_cparams(("arbitrary",)),
        name="ln_mod_project",
    )(xa, mod, *weights)


def _na_kernel(q_ref, k_ref, v_ref, bias_ref, o_ref, *, n_ctx, n_band_starts):
    qb = pl.program_id(1)
    scale = NA_HEAD_DIM ** -0.5
    lane = lax.broadcasted_iota(jnp.int32, (1, LANES), 1)
    heads_per_tile = LANES // NA_HEAD_DIM

    def attend(pair, band_start):
        cols = slice(pair * LANES, (pair + 1) * LANES)
        q = q_ref[:, cols]
        kc, vc = k_ref[0:n_ctx, cols], v_ref[0:n_ctx, cols]
        if band_start is not None:
            kb, vb = k_ref[pl.ds(band_start, NA_KBLK), cols], v_ref[pl.ds(band_start, NA_KBLK), cols]
        outs = []
        for h in range(heads_per_tile):
            qh = jnp.where((lane // NA_HEAD_DIM) == h, q, jnp.zeros_like(q))
            sc = _dot_nt(qh, kc) * scale
            m = jnp.max(sc, axis=-1, keepdims=True)
            if band_start is not None:
                sw = _dot_nt(qh, kb) * scale + bias_ref[0, pair * heads_per_tile + h]
                m = jnp.maximum(m, jnp.max(sw, axis=-1, keepdims=True))
            pc = jnp.exp(sc - m)
            l = jnp.sum(pc, axis=-1, keepdims=True)
            o = _dot(pc.astype(BF16), vc)
            if band_start is not None:
                pw = jnp.exp(sw - m)
                l = l + jnp.sum(pw, axis=-1, keepdims=True)
                o = o + _dot(pw.astype(BF16), vb)
            outs.append(o / l)
        o_ref[:, cols] = jnp.where((lane // NA_HEAD_DIM) == 0, outs[0], outs[1]).astype(o_ref.dtype)

    @pl.when(qb == 0)
    def _():
        for pair in range(NA_STEP_HEADS // heads_per_tile):
            attend(pair, None)

    @pl.when(qb > 0)
    def _():
        start_blk = jnp.clip(qb - 2, 0, n_band_starts - 1)
        start = pl.multiple_of(n_ctx + start_blk * NA_QBLK, NA_QBLK)
        for pair in range(NA_STEP_HEADS // heads_per_tile):
            attend(pair, start)


def _na_bias(rpb):
    a = np.arange(NA_QROWS)[:, None]
    j = np.arange(NA_BAND)[None, :]
    ws = [np.zeros_like(a), a, np.full_like(a, NA_BAND - NA_WIN_H)]
    off = [0, -NA_WIN_H // 2, -(NA_QROWS + NA_WIN_H // 2)]
    qc = np.arange(GRID_W)[:, None]
    kc = np.arange(GRID_W)[None, :]
    wcs = np.clip(qc - NA_WIN_W // 2, 0, GRID_W - NA_WIN_W)
    col_ok = (kc >= wcs) & (kc < wcs + NA_WIN_W)
    depth, heads = rpb.shape[:2]
    n_dc = 2 * NA_WIN_W - 1
    width = NA_KBLK + LANES
    period = width + 1
    t = np.arange(period)
    z = np.where(t < NA_KBLK, t, t - period) + NA_WIN_W - 1
    jj, dc = z // GRID_W, z % GRID_W
    in_window = (z >= 0) & (jj < NA_BAND) & (dc < n_dc)
    jj, dc = np.clip(jj, 0, NA_BAND - 1), np.clip(dc, 0, n_dc - 1)
    dr, ok = [], []
    for p in range(3):
        row_ok = (j >= ws[p]) & (j < ws[p] + NA_WIN_H)
        d_pa = np.clip(j - a + off[p] + NA_WIN_H - 1, 0, 2 * NA_WIN_H - 2)
        dr.append(d_pa[:, jj])
        ok.append(row_ok[:, jj] & in_window)
    dr, ok = np.stack(dr), np.stack(ok)
    v = jnp.where(ok, rpb[:, :, dr, dc], NEG)
    v = v.transpose(0, 2, 1, 3, 4)
    b = jnp.tile(v, (1, 1, 1, 1, GRID_W))[..., :GRID_W * width]
    b = b.reshape(depth, 3, heads, NA_QROWS, GRID_W, width)[..., :NA_KBLK]
    b = jnp.where(np.tile(col_ok, (1, NA_BAND)), b, NEG)
    return b.reshape(depth, 3, heads, NA_QBLK, NA_KBLK)


def _neighbourhood_attention(p_na, bias, n_ctx):
    tt = p_na.shape[0]
    n_qb = tt // NA_QBLK
    n_lat_blk = n_qb - 1
    n_band_starts = n_lat_blk - NA_BAND // NA_QROWS + 1
    hp = NA_HEADS // NA_STEP_HEADS
    w = NA_STEP_HEADS * NA_HEAD_DIM

    def bias_idx(h, qb):
        pat = jnp.where(qb <= 1, 0, jnp.where(qb == n_qb - 1, 2, 1))
        return (pat, h, 0, 0)

    return pl.pallas_call(
        functools.partial(_na_kernel, n_ctx=n_ctx, n_band_starts=n_band_starts),
        grid=(hp, n_qb),
        in_specs=[pl.BlockSpec((NA_QBLK, w), lambda h, qb: (qb, h)),
                  pl.BlockSpec((tt, w), lambda h, qb: (0, hp + h)),
                  pl.BlockSpec((tt, w), lambda h, qb: (0, 2 * hp + h)),
                  pl.BlockSpec((1, NA_STEP_HEADS, NA_QBLK, NA_KBLK), bias_idx)],
        out_specs=pl.BlockSpec((NA_QBLK, w), lambda h, qb: (qb, h)),
        out_shape=jax.ShapeDtypeStruct((tt, NA_W), BF16),
        compiler_params=_cparams(("arbitrary", "arbitrary")),
        name="neighbourhood_attention",
    )(p_na, p_na, p_na, bias)


def _log_sigmoid(x):
    return jnp.minimum(x, 0.0) - jnp.log(1.0 + jnp.exp(-jnp.abs(x)))


def _gla_chunk(q, k, v, lr, wg, bg, tri, causal, s_ref, last_row):
    x = jnp.dot(lr, wg, preferred_element_type=F32, precision=lax.Precision.HIGHEST) + bg
    la = _log_sigmoid(x) * (1.0 / GLA_GATE_NORM)
    b = jnp.dot(tri, la, preferred_element_type=F32, precision=lax.Precision.HIGHEST)
    b_last = b[last_row:last_row + 1, :]
    q_t = (q * (GLA_DK ** -0.5) * jnp.exp(b)).astype(BF16)
    k_t = (k * jnp.exp(-b)).astype(BF16)
    k_end = (k * jnp.exp(b_last - b)).astype(BF16)
    dec = jnp.exp(b_last)
    outs = []
    for h in range(GLA_HEADS):
        ks = slice(h * GLA_DK, (h + 1) * GLA_DK)
        vh = v[:, h * GLA_DV:(h + 1) * GLA_DV].astype(BF16)
        att = jnp.where(causal, _dot_nt(q_t[:, ks], k_t[:, ks]), 0.0)
        st = s_ref[h]
        o = _dot(att.astype(BF16), vh) + _dot_nt(q_t[:, ks], st.astype(BF16))
        s_ref[h] = st * dec[:, ks] + _dot_tn(vh, k_end[:, ks])
        outs.append(o)
    return jnp.concatenate(outs, axis=-1)


def _gla_kernel(qf_ref, kf_ref, vf_ref, lf_ref, qb_ref, kb_ref, vb_ref, lb_ref, wg_ref, bg_ref,
                of_ref, ob_ref, sf_ref, sb_ref):
    @pl.when(pl.program_id(0) == 0)
    def _():
        sf_ref[...] = jnp.zeros_like(sf_ref)
        sb_ref[...] = jnp.zeros_like(sb_ref)

    r = lax.broadcasted_iota(jnp.int32, (GLA_CHUNK, GLA_CHUNK), 0)
    c = lax.broadcasted_iota(jnp.int32, (GLA_CHUNK, GLA_CHUNK), 1)
    lower = r >= c
    upper = r <= c
    tri_f = lower.astype(F32)
    tri_b = upper.astype(F32)
    n_chunks = GLA_BLK // GLA_CHUNK
    for ci in range(n_chunks):
        rows = slice(ci * GLA_CHUNK, (ci + 1) * GLA_CHUNK)
        of_ref[rows, :] = _gla_chunk(qf_ref[rows, :], kf_ref[rows, :], vf_ref[rows, :], lf_ref[rows, :],
                                     wg_ref[0], bg_ref[0], tri_f, lower, sf_ref, GLA_CHUNK - 1)
        cb = n_chunks - 1 - ci
        rows = slice(cb * GLA_CHUNK, (cb + 1) * GLA_CHUNK)
        ob_ref[rows, :] = _gla_chunk(qb_ref[rows, :], kb_ref[rows, :], vb_ref[rows, :], lb_ref[rows, :],
                                     wg_ref[1], bg_ref[1], tri_b, upper, sb_ref, 0)


def _gla(p_gla, wg, bg):
    tt = p_gla.shape[0]
    nb = tt // GLA_BLK
    fwd = lambda i: i
    bwd = lambda i: jnp.where(i == 0, 0, nb - i)
    lr_blk = (2 * GLA_KW + 2 * GLA_VW) // 128

    def specs(rowmap):
        return [pl.BlockSpec((GLA_BLK, GLA_KW), lambda i: (rowmap(i), 0)),
                pl.BlockSpec((GLA_BLK, GLA_KW), lambda i: (rowmap(i), 1)),
                pl.BlockSpec((GLA_BLK, GLA_VW), lambda i: (rowmap(i), 1)),
                pl.BlockSpec((GLA_BLK, 128), lambda i: (rowmap(i), lr_blk))]

    return pl.pallas_call(
        _gla_kernel,
        grid=(nb,),
        in_specs=specs(fwd) + specs(bwd) + [
            pl.BlockSpec((2, 128, GLA_KW), lambda i: (0, 0, 0)),
            pl.BlockSpec((2, 1, GLA_KW), lambda i: (0, 0, 0))],
        out_specs=[pl.BlockSpec((GLA_BLK, GLA_VW), lambda i: (fwd(i), 0)),
                   pl.BlockSpec((GLA_BLK, GLA_VW), lambda i: (bwd(i), 0))],
        out_shape=[jax.ShapeDtypeStruct((tt, GLA_VW), F32)] * 2,
        scratch_shapes=[pltpu.VMEM((GLA_HEADS, GLA_DV, GLA_DK), F32)] * 2,
        compiler_params=_cparams(("arbitrary",)),
        name="gla_bidirectional",
    )(*([p_gla] * 8), wg, bg)


def _group_sum_sq(x, ones_bd):
    sq = x * x
    hi = sq.astype(BF16)
    lo = (sq - hi.astype(F32)).astype(BF16)
    return _dot(hi, ones_bd) + _dot(lo, ones_bd)


def _gqa_prep_kernel(p_ref, ta_ref, tb_ref, ones_ref, qt_ref, kz_ref, vt_ref):
    d = GQA_HEAD_DIM
    x = p_ref[...]
    q, qs = x[:, 0:GQA_QW], x[:, GQA_QW:2 * GQA_QW]
    o = 2 * GQA_QW
    k, ks, v = x[:, o:o + GQA_KVW], x[:, o + GQA_KVW:o + 2 * GQA_KVW], x[:, o + 2 * GQA_KVW:o + 3 * GQA_KVW]
    ta, tb = ta_ref[...], tb_ref[...]
    taq = jnp.tile(ta[:, 0:128], (1, GQA_QW // 128))
    tbq = jnp.tile(tb[:, 0:128], (1, GQA_QW // 128))
    rq = lax.rsqrt(_group_sum_sq(q, ones_ref[...]) * (1.0 / d) + EPS)
    qr = rq * (q * taq + qs * tbq)
    rk = lax.rsqrt(_group_sum_sq(k, ones_ref[0:GQA_KVW, 0:GQA_KVW]) * (1.0 / d) + EPS)
    kr = rk * (k * ta[:, 128:256] + ks * tb[:, 128:256])
    qt_ref[...] = qr.T.astype(BF16)
    lane = lax.broadcasted_iota(jnp.int32, (1, GQA_KVW), 1)
    for g in range(GQA_KV_HEADS):
        kz_ref[g] = jnp.where((lane // d) == g, kr, 0.0).astype(BF16)
    vt_ref[...] = v.T.astype(BF16)


def _gqa_prep(p_gqa, ta, tb):
    tt, n = p_gqa.shape
    tm = _pick(tt, (384, 256, 128))
    hid = np.arange(GQA_QW) // GQA_HEAD_DIM
    ones_bd = jnp.asarray(hid[:, None] == hid[None, :], BF16)
    return pl.pallas_call(
        _gqa_prep_kernel,
        grid=(tt // tm,),
        in_specs=[pl.BlockSpec((tm, n), lambda i: (i, 0)),
                  pl.BlockSpec((tm, 256), lambda i: (i, 0)),
                  pl.BlockSpec((tm, 256), lambda i: (i, 0)),
                  pl.BlockSpec((GQA_QW, GQA_QW), lambda i: (0, 0))],
        out_specs=[pl.BlockSpec((GQA_QW, tm), lambda i: (0, i)),
                   pl.BlockSpec((GQA_KV_HEADS, tm, GQA_KVW), lambda i: (0, i, 0)),
                   pl.BlockSpec((GQA_KVW, tm), lambda i: (0, i))],
        out_shape=[jax.ShapeDtypeStruct((GQA_QW, tt), BF16),
                   jax.ShapeDtypeStruct((GQA_KV_HEADS, tt, GQA_KVW), BF16),
                   jax.ShapeDtypeStruct((GQA_KVW, tt), BF16)],
        compiler_params=_cparams(("arbitrary",)),
        name="gqa_prep",
    )(p_gqa, ta, tb, ones_bd)


def _gqa_kernel(qt_ref, kz_ref, vt_ref, o_ref, s0_ref, s1_ref, acc_ref, *, n_ctx):
    qb = pl.program_id(1)
    d = GQA_HEAD_DIM
    group = GQA_HEADS // GQA_KV_HEADS
    kc = GQA_KEY_CHUNK
    tq = qt_ref.shape[1]
    q4 = jnp.concatenate([jnp.concatenate([qt_ref[g * d:(g + 1) * d, :]] * 2, axis=0) for g in range(group)],
                         axis=1)

    def scores(c, s_ref):
        start = pl.multiple_of(c * kc, kc)
        s = _dot(kz_ref[0, pl.ds(start, kc), :], q4)
        s_ref[...] = s
        return jnp.max(s, axis=0, keepdims=True)

    def accumulate(c, s_ref, m_chunk, m):
        m_new = jnp.maximum(m, m_chunk)
        p = jnp.exp2(s_ref[...] - m_new)
        acc_ref[...] = acc_ref[...] * jnp.exp2(m - m_new) + _dot(vt_ref[0, c], p.astype(BF16))
        return m_new

    def attend(n_chunks):
        acc_ref[...] = jnp.zeros_like(acc_ref)
        m = jnp.full((1, group * tq), -jnp.inf, F32)
        mc0 = scores(0, s0_ref)

        def body(j, carry):
            m, mc0 = carry
            for k in range(0, GQA_LOOP_CHUNKS, 2):
                c = GQA_LOOP_CHUNKS * j + k
                mc1 = scores(c + 1, s1_ref)
                m = accumulate(c, s0_ref, mc0, m)
                mc0 = scores(c + 2, s0_ref)
                m = accumulate(c + 1, s1_ref, mc1, m)
            return m, mc0

        m, mc0 = lax.fori_loop(0, (n_chunks - 1) // GQA_LOOP_CHUNKS, body, (m, mc0))
        accumulate(n_chunks - 1, s0_ref, mc0, m)
        out = acc_ref[0:d, :] / acc_ref[d:d + 1, :]
        o_ref[...] = jnp.concatenate([out[:, g * tq:(g + 1) * tq] for g in range(group)],
                                     axis=0).T.astype(o_ref.dtype)

    @pl.when(qb == 0)
    def _():
        attend(n_ctx // kc)

    @pl.when(qb > 0)
    def _():
        attend(kz_ref.shape[1] // kc)


def _gqa_attention(qt, kz, vt, n_ctx):
    tt = qt.shape[1]
    tq = n_ctx
    kc = GQA_KEY_CHUNK
    group = GQA_HEADS // GQA_KV_HEADS
    gw = group * GQA_HEAD_DIM
    assert tt % kc == 0 and n_ctx % kc == 0
    assert (tt // kc) % GQA_LOOP_CHUNKS == 1 and (n_ctx // kc) % GQA_LOOP_CHUNKS == 1
    vt3 = vt.reshape(GQA_KV_HEADS, GQA_HEAD_DIM, tt // kc, kc).transpose(0, 2, 1, 3)
    extra = jnp.zeros((GQA_KV_HEADS, tt // kc, GQA_ONES_ROWS, kc), BF16).at[:, :, 0, :].set(1.0)
    vt3 = jnp.concatenate([vt3, extra], axis=2)
    vrows = GQA_HEAD_DIM + GQA_ONES_ROWS
    return pl.pallas_call(
        functools.partial(_gqa_kernel, n_ctx=n_ctx),
        grid=(GQA_KV_HEADS, tt // tq),
        in_specs=[pl.BlockSpec((gw, tq), lambda g, i: (g, i)),
                  pl.BlockSpec((1, tt, GQA_KVW), lambda g, i: (g, 0, 0)),
                  pl.BlockSpec((1, tt // kc, vrows, kc), lambda g, i: (g, 0, 0, 0))],
        out_specs=pl.BlockSpec((tq, gw), lambda g, i: (i, g)),
        out_shape=jax.ShapeDtypeStruct((tt, GQA_QW), BF16),
        scratch_shapes=[pltpu.VMEM((kc, group * tq), F32), pltpu.VMEM((kc, group * tq), F32),
                        pltpu.VMEM((vrows, group * tq), F32)],
        compiler_params=_cparams(("arbitrary", "arbitrary")),
        name="gqa_attention",
    )(qt, kz, vt3)


def _mix_out_kernel(x_ref, mod_ref, yna_ref, of_ref, ob_ref, gate_ref, ygqa_ref, w_ref, nw_ref,
                    g_ref, b_ref, x1_ref, h2t_ref, *, n_ctx, tm):
    is_ctx = _row_is_ctx(pl.program_id(0), tm, n_ctx)
    o = of_ref[...] + ob_ref[...]
    nw = nw_ref[...]
    parts = []
    for h in range(GLA_HEADS):
        oh = o[:, h * GLA_DV:(h + 1) * GLA_DV]
        parts.append(oh * lax.rsqrt(jnp.mean(oh * oh, axis=-1, keepdims=True) + EPS) * nw)
    gate = gate_ref[...]
    y_gla = (jnp.concatenate(parts, axis=-1) * (gate / (1.0 + jnp.exp(-gate)))).astype(BF16)
    y = (_dot(yna_ref[...], w_ref[0:NA_W, :]) + _dot(y_gla, w_ref[NA_W:NA_W + GLA_VW, :])
         + _dot(ygqa_ref[...], w_ref[NA_W + GLA_VW:, :]))
    x1 = _layer_norm(DEEPNORM_ALPHA * x_ref[...] + _mod_row(mod_ref, 2, is_ctx) * y) * g_ref[...] + b_ref[...]
    x1_ref[...] = x1
    h2 = _layer_norm(x1) * (1.0 + _mod_row(mod_ref, 4, is_ctx)) + _mod_row(mod_ref, 3, is_ctx)
    h2t_ref[...] = h2.T.astype(BF16)


def _mix_out(xa, mod, y_na, o_f, o_b, p_gla, y_gqa, w_out, norm_w, ln_g, ln_b, n_ctx):
    tt, d = xa.shape
    tm = _pick(tt, (384, 256, 128))
    row = lambda i: (i, 0)
    const2 = lambda i: (0, 0)
    return pl.pallas_call(
        functools.partial(_mix_out_kernel, n_ctx=n_ctx, tm=tm),
        grid=(tt // tm,),
        in_specs=[pl.BlockSpec((tm, d), row),
                  pl.BlockSpec((2, 6, d), lambda i: (0, 0, 0)),
                  pl.BlockSpec((tm, NA_W), row),
                  pl.BlockSpec((tm, GLA_VW), row),
                  pl.BlockSpec((tm, GLA_VW), row),
                  pl.BlockSpec((tm, GLA_VW), lambda i: (i, 2)),
                  pl.BlockSpec((tm, GQA_QW), row),
                  pl.BlockSpec(w_out.shape, const2),
                  pl.BlockSpec((1, GLA_DV), const2),
                  pl.BlockSpec((1, d), const2),
                  pl.BlockSpec((1, d), const2)],
        out_specs=[pl.BlockSpec((tm, d), row), pl.BlockSpec((d, tm), lambda i: (0, i))],
        out_shape=[jax.ShapeDtypeStruct((tt, d), F32), jax.ShapeDtypeStruct((d, tt), BF16)],
        compiler_params=_cparams(("arbitrary",)),
        name="mixer_out_postln",
    )(xa, mod, y_na, o_f, o_b, p_gla, y_gqa, w_out, norm_w, ln_g, ln_b)


def _top16_exact(s, key_iota):
    cur = s
    rank = jnp.full(s.shape, float(PEER_TOPK), F32)
    vals = []
    for r in range(PEER_TOPK):
        m = jnp.max(cur, axis=0, keepdims=True)
        idx = jnp.min(jnp.where(cur == m, key_iota, float(PEER_NKEYS)), axis=0, keepdims=True)
        sel = key_iota == idx
        rank = jnp.where(sel, float(r), rank)
        cur = jnp.where(sel, -jnp.inf, cur)
        vals.append(m)
    return jnp.concatenate(vals, axis=0), rank


_CODE_SCALE = 2.0 ** 100
_CODE_STEP = 64.0


def _rank_code(r):
    return -_CODE_SCALE * (1.0 + (r + 1) / _CODE_STEP)


def _top16_fast(s):
    cur = s
    vals = []
    for r in range(PEER_TOPK):
        m = jnp.max(cur, axis=0, keepdims=True)
        cur = jnp.where(cur == m, _rank_code(r), cur)
        vals.append(m)
    return jnp.concatenate(vals, axis=0), cur


def _coded(cur):
    return cur <= _rank_code(0)


def _decode_rank(cur):
    rank = (cur * (-1.0 / _CODE_SCALE) - 1.0) * _CODE_STEP - 1.0
    return jnp.where(_coded(cur), rank, float(PEER_TOPK))


def _cand_layout():
    k = PEER_TOPK
    rows = [(0, rb) for rb in range(k)]
    for ra in range(1, 5):
        rows += [(ra, rb) for rb in range(8)]
    rows += [(ra, 0) for ra in range(8, k)]
    rows += [(ra, 1) for ra in range(8)]
    rows += [(ra, 0) for ra in range(8)]
    seen, valid = set(), []
    for pair in rows:
        valid.append(pair not in seen)
        seen.add(pair)
    needed = {(ra, rb) for ra in range(k) for rb in range(k) if (ra + 1) * (rb + 1) <= k}
    assert needed <= seen
    ra = np.array([p[0] for p in rows], np.float32)
    rb = np.array([p[1] for p in rows], np.float32)
    return ra, rb, np.array(valid)


def _select_pairs(a, b, c_ra, c_flat, c_ok, exact):
    k = PEER_TOPK
    tt = a.shape[1]
    blocks = [a[0:1, :] + b]
    for ra in range(1, 5):
        blocks.append(a[ra:ra + 1, :] + b[0:8, :])
    blocks.append(a[8:k, :] + b[0:1, :])
    blocks.append(a[0:8, :] + b[1:2, :])
    blocks.append(a[0:8, :] + b[0:1, :])
    cand = jnp.where(c_ok, jnp.concatenate(blocks, axis=0), -jnp.inf)
    m0 = a[0:1, :] + b[0:1, :]
    z = jnp.zeros((1, tt), F32)
    if exact:
        rank_iota = lax.broadcasted_iota(jnp.int32, (k, tt), 0).astype(F32)
        cnt = jnp.zeros((k, tt), F32)
        for _ in range(k):
            m = jnp.max(cand, axis=0, keepdims=True)
            idx = jnp.min(jnp.where(cand == m, c_flat, 1e9), axis=0, keepdims=True)
            sel = c_flat == idx
            ra_sel = jnp.sum(jnp.where(sel, c_ra, 0.0), axis=0, keepdims=True)
            cnt = cnt + jnp.where(rank_iota == ra_sel, 1.0, 0.0)
            z = z + jnp.exp(m - m0)
            cand = jnp.where(sel, -jnp.inf, cand)
        return cnt, z, None
    for _ in range(k):
        m = jnp.max(cand, axis=0, keepdims=True)
        z = z + jnp.exp(m - m0)
        cand = jnp.where(cand == m, -jnp.inf, cand)
    sel = jnp.where(jnp.logical_and(c_ok, cand == -jnp.inf), 1.0, 0.0)
    low = sel[56:64, :] + sel[64:72, :]
    row8 = lax.broadcasted_iota(jnp.int32, (8, tt), 0)
    sums = [jnp.sum(sel[0:16, :], axis=0, keepdims=True)]
    sums += [jnp.sum(sel[16 + 8 * i:24 + 8 * i, :], axis=0, keepdims=True) for i in range(4)]
    for r, v in enumerate(sums):
        low = jnp.where(row8 == r, v, low)
    cnt = jnp.concatenate([low, sel[48:56, :]], axis=0)
    return cnt, z, jnp.sum(sel, axis=0, keepdims=True)


def _route_head(s1, s2, key_iota, c_ra, c_flat, c_ok, exact):
    k = PEER_TOPK
    if exact:
        a, r1 = _top16_exact(s1, key_iota)
        b, r2 = _top16_exact(s2, key_iota)
    else:
        a, cur1 = _top16_fast(s1)
        b, cur2 = _top16_fast(s2)
        r2 = _decode_rank(cur2)
    cnt, z, n_sel = _select_pairs(a, b, c_ra, c_flat, c_ok, exact)
    n1 = jnp.zeros(s1.shape, F32)
    for r in range(k):
        hit = (r1 == float(r)) if exact else (s1 == a[r:r + 1, :])
        n1 = jnp.where(hit, cnt[r:r + 1, :], n1)
    outs = (r2, jnp.exp(s2 - b[0:1, :]) / z, n1, jnp.exp(s1 - a[0:1, :]))
    if exact:
        return outs, None
    n_coded = lambda cur: jnp.sum(jnp.where(_coded(cur), 1.0, 0.0), axis=0, keepdims=True)
    clean = jnp.logical_and(n_sel == float(k),
                            jnp.logical_and(n_coded(cur1) == float(k), n_coded(cur2) == float(k)))
    return outs, jnp.where(clean, 0.0, 1.0)


def _route_kernel(h_ref, wq_ref, sk_ref, cst_ref, r2_ref, e2_ref, n1_ref, e1_ref):
    tt = h_ref.shape[1]
    key_iota = lax.broadcasted_iota(jnp.int32, (PEER_NKEYS, tt), 0).astype(F32)
    cst = cst_ref[...]
    n_rows = cst.shape[0]
    c_ra = jnp.broadcast_to(cst[:, 0:1], (n_rows, tt))
    c_flat = jnp.broadcast_to(cst[:, 1:2], (n_rows, tt))
    c_ok = jnp.broadcast_to(cst[:, 2:3], (n_rows, tt)) > 0.5
    half = PEER_KEY_DIM // 2

    def head(h, qt_h, exact):
        s1 = _dot(sk_ref[h, 0], qt_h[0:half, :].astype(BF16))
        s2 = _dot(sk_ref[h, 1], qt_h[half:2 * half, :].astype(BF16))
        outs, bad = _route_head(s1, s2, key_iota, c_ra, c_flat, c_ok, exact)
        for ref, val in zip((r2_ref, e2_ref, n1_ref, e1_ref), outs):
            ref[h] = val.astype(ref.dtype)
        return bad

    qt = _dot(wq_ref[...], h_ref[...])
    tied = [jnp.max(head(h, qt[h * PEER_KEY_DIM:(h + 1) * PEER_KEY_DIM, :], False)) for h in range(PEER_HEADS)]
    for h in range(PEER_HEADS):
        @pl.when(tied[h] > 0.0)
        def _(h=h):
            head(h, _dot(wq_ref[h * PEER_KEY_DIM:(h + 1) * PEER_KEY_DIM, :], h_ref[...]), True)


def _peer_route(h2t, wq_t, sub_keys):
    d, tt = h2t.shape
    tb = _pick(tt, (256, 128))
    ra, rb, valid = _cand_layout()
    cst = np.zeros((ra.shape[0], LANES), np.float32)
    flat = np.where(valid, ra * PEER_TOPK + rb, 1000.0 + np.arange(ra.shape[0]))
    cst[:, 0], cst[:, 1], cst[:, 2] = ra, flat, valid
    shp = (PEER_HEADS, PEER_NKEYS, tt)
    out_spec = pl.BlockSpec((PEER_HEADS, PEER_NKEYS, tb), lambda i: (0, 0, i))
    return pl.pallas_call(
        _route_kernel,
        grid=(tt // tb,),
        in_specs=[pl.BlockSpec((d, tb), lambda i: (0, i)),
                  pl.BlockSpec(wq_t.shape, lambda i: (0, 0)),
                  pl.BlockSpec(sub_keys.shape, lambda i: (0, 0, 0, 0)),
                  pl.BlockSpec(cst.shape, lambda i: (0, 0))],
        out_specs=[out_spec] * 4,
        out_shape=[jax.ShapeDtypeStruct(shp, dt) for dt in (BF16, BF16, F32, F32)],
        compiler_params=_cparams(("arbitrary",)),
        name="peer_route",
    )(h2t, wq_t, sub_keys, jnp.asarray(cst))


def _gelu_tanh(x):
    c = math.sqrt(2.0 / math.pi)
    hx = 0.5 * x
    return hx * jnp.tanh(x * (c + (c * 0.044715) * (x * x))) + hx


def _peer_kernel(h_ref, u_ref, vt_ref, r2_ref, e2_ref, n1_ref, e1_ref, o_ref):
    e = pl.program_id(1)
    eb = u_ref.shape[0]
    n_i = eb // PEER_NKEYS

    @pl.when(e == 0)
    def _():
        o_ref[...] = jnp.zeros_like(o_ref)

    n_sub = eb // PEER_SUB
    sub_rows = lambda s: slice(s * PEER_SUB, (s + 1) * PEER_SUB)
    pre = _dot(u_ref[sub_rows(0), :].astype(BF16), h_ref[...])
    for s in range(n_sub):
        rows = sub_rows(s)
        cur = pre
        if s + 1 < n_sub:
            pre = _dot(u_ref[sub_rows(s + 1), :].astype(BF16), h_ref[...])
        act = _gelu_tanh(cur)
        parts = []
        for ii in range(PEER_SUB // PEER_NKEYS):
            i = e * n_i + s * (PEER_SUB // PEER_NKEYS) + ii
            tb = act.shape[1]
            tiles = PEER_NKEYS // BF16_SUBLANES
            g = None
            for h in range(PEER_HEADS):
                n1 = jnp.broadcast_to(n1_ref[h, pl.ds(i, 1), :], (BF16_SUBLANES, tb)).astype(BF16)
                e1 = jnp.broadcast_to(e1_ref[h, pl.ds(i, 1), :], (BF16_SUBLANES, tb)).astype(BF16)
                r2t = r2_ref[h].reshape(tiles, BF16_SUBLANES, tb)
                e2t = e2_ref[h].reshape(tiles, BF16_SUBLANES, tb)
                t = jnp.where(r2t < n1[None], e2t, jnp.zeros_like(e2t)) * e1[None]
                g = t if g is None else g + t
            a16 = act[ii * PEER_NKEYS:(ii + 1) * PEER_NKEYS, :].astype(BF16).reshape(tiles, BF16_SUBLANES, tb)
            parts.append((a16 * g).reshape(PEER_NKEYS, tb))
        o_ref[...] += _dot(vt_ref[:, rows], jnp.concatenate(parts, axis=0))


def _peer_experts(h2t, u_tab, vt_tab, r2, e2, n1, e1):
    d, tt = h2t.shape
    n_exp = u_tab.shape[0]
    tb = _pick(tt, (768, 512, 256))
    eb = PEER_EXPERT_BLK
    route_spec = pl.BlockSpec((PEER_HEADS, PEER_NKEYS, tb), lambda i, e: (0, 0, i),
                              pipeline_mode=pl.Buffered(1))
    return pl.pallas_call(
        _peer_kernel,
        grid=(tt // tb, n_exp // eb),
        in_specs=[pl.BlockSpec((d, tb), lambda i, e: (0, i), pipeline_mode=pl.Buffered(1)),
                  pl.BlockSpec((eb, d), lambda i, e: (e, 0)),
                  pl.BlockSpec((d, eb), lambda i, e: (0, e)),
                  route_spec, route_spec, route_spec, route_spec],
        out_specs=pl.BlockSpec((d, tb), lambda i, e: (0, i)),
        out_shape=jax.ShapeDtypeStruct((d, tt), F32),
        compiler_params=_cparams(("arbitrary", "arbitrary")),
        name="peer_experts",
    )(h2t, u_tab, vt_tab, r2, e2, n1, e1)


def _peer_out_kernel(x_ref, mod_ref, pt_ref, g_ref, b_ref, o_ref, *, n_ctx, tm):
    is_ctx = _row_is_ctx(pl.program_id(0), tm, n_ctx)
    y = pt_ref[...].T
    o_ref[...] = (_layer_norm(DEEPNORM_ALPHA * x_ref[...] + _mod_row(mod_ref, 5, is_ctx) * y)
                  * g_ref[...] + b_ref[...])


def _peer_out(x1, mod, peer_t, ln_g, ln_b, n_ctx):
    tt, d = x1.shape
    tm = _pick(tt, (384, 256, 128))
    return pl.pallas_call(
        functools.partial(_peer_out_kernel, n_ctx=n_ctx, tm=tm),
        grid=(tt // tm,),
        in_specs=[pl.BlockSpec((tm, d), lambda i: (i, 0)),
                  pl.BlockSpec((2, 6, d), lambda i: (0, 0, 0)),
                  pl.BlockSpec((d, tm), lambda i: (0, i)),
                  pl.BlockSpec((1, d), lambda i: (0, 0)),
                  pl.BlockSpec((1, d), lambda i: (0, 0))],
        out_specs=pl.BlockSpec((tm, d), lambda i: (i, 0)),
        out_shape=jax.ShapeDtypeStruct((tt, d), F32),
        compiler_params=_cparams(("arbitrary",)),
        name="peer_out_postln",
    )(x1, mod, peer_t, ln_g, ln_b)


def _rope_tables(n_ctx, n_lat):
    half = GQA_HEAD_DIM // 2
    inv = ROPE_BASE ** (-jnp.arange(0, half, 2, dtype=F32) / half)
    t = jnp.arange(n_lat)
    row = (t // GRID_W).astype(F32)
    col = (t % GRID_W).astype(F32)
    ang = jnp.concatenate([row[:, None] * inv, col[:, None] * inv], axis=-1)
    cos, sin = jnp.cos(ang), jnp.sin(ang)
    cos2 = jnp.concatenate([jnp.ones((n_ctx, 2 * half), F32), jnp.concatenate([cos, cos], -1)], 0)
    sin2 = jnp.concatenate([jnp.zeros((n_ctx, 2 * half), F32), jnp.concatenate([-sin, sin], -1)], 0)
    return cos2, sin2


_EVEN_ODD = np.concatenate([np.arange(0, GQA_HEAD_DIM, 2), np.arange(1, GQA_HEAD_DIM, 2)])
_ODD_EVEN = np.concatenate([np.arange(1, GQA_HEAD_DIM, 2), np.arange(0, GQA_HEAD_DIM, 2)])


def _head_cols(perm, n_heads):
    return np.concatenate([h * GQA_HEAD_DIM + perm for h in range(n_heads)])


def kernel(x, c, ctx, c_ctx, w_mod, b_mod, w_in, na_rpb, gla_gate_w, gla_gate_b, gla_norm_w,
           gqa_q_norm, gqa_k_norm, w_out, ln1_g, ln1_b, peer_wq, peer_subkeys, peer_u, peer_v,
           ln2_g, ln2_b):
    batch, n_lat, d = x.shape
    assert batch == 1 and d == D_MODEL
    n_ctx = ctx.shape[1]
    assert n_lat % NA_QBLK == 0 and n_lat // GRID_W >= NA_BAND + NA_QROWS
    depth = w_mod.shape[0]

    xa = jnp.concatenate([ctx[0], x[0]], axis=0)
    mods = _modulation(jnp.stack([c_ctx, c[0]]), w_mod, b_mod).reshape(depth, 2, 6, d)

    o = np.cumsum([0, NA_W, NA_W, NA_W, GLA_KW, GLA_KW, GLA_VW, GLA_VW, 2 * GLA_GATE_RANK,
                   GQA_QW, GQA_KVW, GQA_KVW])
    w_na = w_in[:, :, o[0]:o[3]].astype(BF16)
    w_gla = w_in[:, :, o[3]:o[8] + (128 - 2 * GLA_GATE_RANK)].astype(BF16)
    wq_c, wk_c, wv_c = w_in[:, :, o[8]:o[9]], w_in[:, :, o[9]:o[10]], w_in[:, :, o[10]:o[11]]
    w_gqa = jnp.concatenate([wq_c[:, :, _head_cols(_EVEN_ODD, GQA_HEADS)],
                             wq_c[:, :, _head_cols(_ODD_EVEN, GQA_HEADS)],
                             wk_c[:, :, _head_cols(_EVEN_ODD, GQA_KV_HEADS)],
                             wk_c[:, :, _head_cols(_ODD_EVEN, GQA_KV_HEADS)],
                             wv_c], -1).astype(BF16)
    wg = jnp.zeros((depth, 2, 128, GLA_KW), F32)
    wg = wg.at[:, 0, 0:GLA_GATE_RANK].set(gla_gate_w[:, 0])
    wg = wg.at[:, 1, GLA_GATE_RANK:2 * GLA_GATE_RANK].set(gla_gate_w[:, 1])
    bg = gla_gate_b.reshape(depth, 2, 1, GLA_KW)
    w_out_b = w_out.astype(BF16)
    wq_t = jnp.swapaxes(peer_wq, 1, 2).astype(BF16)
    sub_keys = peer_subkeys.astype(BF16)
    u_tab = peer_u
    vt_tab = jnp.swapaxes(peer_v, 1, 2).astype(BF16)

    cos2, sin2 = _rope_tables(n_ctx, n_lat)
    q_scale = GQA_HEAD_DIM ** -0.5 * math.log2(math.e)
    na_bias = _na_bias(na_rpb)

    for l in range(depth):
        mod = mods[l]
        p_na, p_gla, p_gqa = _project(xa, mod, (w_na[l], w_gla[l], w_gqa[l]), (BF16, F32, F32), n_ctx)

        y_na = _neighbourhood_attention(p_na, na_bias[l], n_ctx)
        o_f, o_b = _gla(p_gla, wg[l], bg[l])

        wqn, wkn = gqa_q_norm[l], gqa_k_norm[l]
        ta = jnp.concatenate([jnp.tile(cos2 * wqn[_EVEN_ODD] * q_scale, (1, 2)),
                              jnp.tile(cos2 * wkn[_EVEN_ODD], (1, 2))], -1)
        tb = jnp.concatenate([jnp.tile(sin2 * wqn[_ODD_EVEN] * q_scale, (1, 2)),
                              jnp.tile(sin2 * wkn[_ODD_EVEN], (1, 2))], -1)
        qt, kz, vt = _gqa_prep(p_gqa, ta, tb)
        y_gqa = _gqa_attention(qt, kz, vt, n_ctx)

        x1, h2t = _mix_out(xa, mod, y_na, o_f, o_b, p_gla, y_gqa, w_out_b[l],
                           gla_norm_w[l].reshape(1, GLA_DV), ln1_g[l].reshape(1, d), ln1_b[l].reshape(1, d), n_ctx)
        r2, e2, n1, e1 = _peer_route(h2t, wq_t[l], sub_keys[l])
        peer_t = _peer_experts(h2t, u_tab[l], vt_tab[l], r2, e2, n1, e1)
        xa = _peer_out(x1, mod, peer_t, ln2_g[l].reshape(1, d), ln2_b[l].reshape(1, d), n_ctx)

    return xa[n_ctx:][None]
```

```python
import functools
import math

import numpy as np
import jax
import jax.numpy as jnp
from jax import lax
from jax.experimental import pallas as pl
from jax.experimental.pallas import tpu as pltpu

F32 = jnp.float32
BF16 = jnp.bfloat16

D_MODEL = 2048
DEPTH = 4
GRID_W = 64
EPS = 1e-6

NA_HEADS = 8
NA_HEAD_DIM = 64
NA_WIN_H = 8
NA_WIN_W = 16
GLA_HEADS = 4
GLA_DK = 128
GLA_DV = 256
GLA_GATE_RANK = 16
GLA_GATE_NORM = 16.0
GLA_CHUNK = 64
GQA_HEADS = 8
GQA_KV_HEADS = 2
GQA_HEAD_DIM = 64
ROPE_BASE = 10000.0
PEER_HEADS = 8
PEER_NKEYS = 128
PEER_KEY_DIM = 256
PEER_TOPK = 16

NA_W = NA_HEADS * NA_HEAD_DIM
GLA_KW = GLA_HEADS * GLA_DK
GLA_VW = GLA_HEADS * GLA_DV
GQA_QW = GQA_HEADS * GQA_HEAD_DIM
GQA_KVW = GQA_KV_HEADS * GQA_HEAD_DIM
DEEPNORM_ALPHA = (2.0 * DEPTH) ** 0.25

LANES = 128
BF16_SUBLANES = 16
VMEM_LIMIT_BYTES = 58 * 1024 * 1024

NA_QROWS = 4
NA_BAND = NA_QROWS + NA_WIN_H
NA_QBLK = NA_QROWS * GRID_W
NA_KBLK = NA_BAND * GRID_W
NA_STEP_HEADS = 4
GLA_BLK = 256
GQA_KEY_CHUNK = 256
GQA_LOOP_CHUNKS = 8
GQA_ONES_ROWS = 16
PEER_EXPERT_BLK = 1024
PEER_SUB = 512
NEG = -1e30


def _cparams(sem):
    return pltpu.CompilerParams(dimension_semantics=sem, vmem_limit_bytes=VMEM_LIMIT_BYTES)


def _pick(n, cands):
    for c in cands:
        if n % c == 0:
            return c
    raise ValueError(f"no block size in {cands} divides {n}")


def _dot(a, b):
    return jnp.dot(a, b, preferred_element_type=F32)


def _dot_nt(a, b):
    return lax.dot_general(a, b, (((1,), (1,)), ((), ())), preferred_element_type=F32)


def _dot_tn(a, b):
    return lax.dot_general(a, b, (((0,), (0,)), ((), ())), preferred_element_type=F32)


def _layer_norm(x):
    mu = jnp.mean(x, axis=-1, keepdims=True)
    xc = x - mu
    var = jnp.mean(xc * xc, axis=-1, keepdims=True)
    return xc * lax.rsqrt(var + EPS)


def _row_is_ctx(block_idx, tm, n_ctx):
    rows = block_idx * tm + lax.broadcasted_iota(jnp.int32, (tm, 1), 0)
    return rows < n_ctx


def _mod_row(mod_ref, idx, is_ctx):
    return jnp.where(is_ctx, mod_ref[0, idx:idx + 1, :], mod_ref[1, idx:idx + 1, :])


def _mod_kernel(c_ref, w_ref, b_ref, o_ref):
    w = w_ref[0]
    reps = w.shape[1] // LANES
    for m in range(2):
        cb = c_ref[m]
        s = cb / (1.0 + jnp.exp(-cb))
        o_ref[0, m:m + 1, :] = jnp.sum(w * jnp.tile(s, (1, reps)), axis=0, keepdims=True) + b_ref[0]


def _modulation(c2, w_mod, b_mod):
    depth, d, n = w_mod.shape
    tn = _pick(n, (512, 256, 128))
    cb = jnp.broadcast_to(c2[:, :, None], (2, d, LANES))
    return pl.pallas_call(
        _mod_kernel,
        grid=(depth, n // tn),
        in_specs=[pl.BlockSpec((2, d, LANES), lambda l, j: (0, 0, 0)),
                  pl.BlockSpec((1, d, tn), lambda l, j: (l, 0, j)),
                  pl.BlockSpec((1, 1, tn), lambda l, j: (l, 0, j))],
        out_specs=pl.BlockSpec((1, 2, tn), lambda l, j: (l, 0, j)),
        out_shape=jax.ShapeDtypeStruct((depth, 2, n), F32),
        compiler_params=_cparams(("arbitrary", "arbitrary")),
        name="modulation",
    )(cb, w_mod, b_mod.reshape(depth, 1, n))


def _proj_kernel(x_ref, mod_ref, *refs, n_ctx, tm):
    is_ctx = _row_is_ctx(pl.program_id(0), tm, n_ctx)
    xn = _layer_norm(x_ref[...])
    h = (xn * (1.0 + _mod_row(mod_ref, 1, is_ctx)) + _mod_row(mod_ref, 0, is_ctx)).astype(BF16)
    n_groups = len(refs) // 2
    for w_ref, o_ref in zip(refs[:n_groups], refs[n_groups:]):
        o_ref[...] = _dot(h, w_ref[...]).astype(o_ref.dtype)


def _layer_block(stacked, layer, **kw):
    zeros = (0,) * (stacked.ndim - 1)
    return pl.BlockSpec((None,) + stacked.shape[1:], lambda *_: (layer,) + zeros, **kw)


def _project(xa, mod, weights, out_dtypes, n_ctx, layer):
    tt, d = xa.shape
    tm = _pick(tt, (256, 128))
    w_specs = [_layer_block(w, layer, pipeline_mode=pl.Buffered(1)) for w in weights]
    return pl.pallas_call(
        functools.partial(_proj_kernel, n_ctx=n_ctx, tm=tm),
        grid=(tt // tm,),
        in_specs=[pl.BlockSpec((tm, d), lambda i: (i, 0)),
                  pl.BlockSpec((2, 6, d), lambda i: (0, 0, 0))] + w_specs,
        out_specs=[pl.BlockSpec((tm, w.shape[2]), lambda i: (i, 0)) for w in weights],
        out_shape=[jax.ShapeDtypeStruct((tt, w.shape[2]), dt) for w, dt in zip(weights, out_dtypes)],
        compiler_params=_cparams(("arbitrary",)),
        name="ln_mod_project",
    )(xa, mod, *weights)


def _na_kernel(q_ref, k_ref, v_ref, bias_ref, o_ref, *, n_ctx, n_band_starts):
    qb = pl.program_id(1)
    scale = NA_HEAD_DIM ** -0.5
    lane = lax.broadcasted_iota(jnp.int32, (1, LANES), 1)
    heads_per_tile = LANES // NA_HEAD_DIM

    def attend(pair, band_start):
        cols = slice(pair * LANES, (pair + 1) * LANES)
        q = q_ref[:, cols]
        kc, vc = k_ref[0:n_ctx, cols], v_ref[0:n_ctx, cols]
        if band_start is not None:
            kb, vb = k_ref[pl.ds(band_start, NA_KBLK), cols], v_ref[pl.ds(band_start, NA_KBLK), cols]
        outs = []
        for h in range(heads_per_tile):
            qh = jnp.where((lane // NA_HEAD_DIM) == h, q, jnp.zeros_like(q))
            sc = _dot_nt(qh, kc) * scale
            m = jnp.max(sc, axis=-1, keepdims=True)
            if band_start is not None:
                sw = _dot_nt(qh, kb) * scale + bias_ref[0, pair * heads_per_tile + h]
                m = jnp.maximum(m, jnp.max(sw, axis=-1, keepdims=True))
            pc = jnp.exp(sc - m)
            l = jnp.sum(pc, axis=-1, keepdims=True)
            o = _dot(pc.astype(BF16), vc)
            if band_start is not None:
                pw = jnp.exp(sw - m)
                l = l + jnp.sum(pw, axis=-1, keepdims=True)
                o = o + _dot(pw.astype(BF16), vb)
            outs.append(o / l)
        o_ref[:, cols] = jnp.where((lane // NA_HEAD_DIM) == 0, outs[0], outs[1]).astype(o_ref.dtype)

    @pl.when(qb == 0)
    def _():
        for pair in range(NA_STEP_HEADS // heads_per_tile):
            attend(pair, None)

    @pl.when(qb > 0)
    def _():
        start_blk = jnp.clip(qb - 2, 0, n_band_starts - 1)
        start = pl.multiple_of(n_ctx + start_blk * NA_QBLK, NA_QBLK)
        for pair in range(NA_STEP_HEADS // heads_per_tile):
            attend(pair, start)


def _na_bias(rpb):
    a = np.arange(NA_QROWS)[:, None]
    j = np.arange(NA_BAND)[None, :]
    ws = [np.zeros_like(a), a, np.full_like(a, NA_BAND - NA_WIN_H)]
    off = [0, -NA_WIN_H // 2, -(NA_QROWS + NA_WIN_H // 2)]
    qc = np.arange(GRID_W)[:, None]
    kc = np.arange(GRID_W)[None, :]
    wcs = np.clip(qc - NA_WIN_W // 2, 0, GRID_W - NA_WIN_W)
    col_ok = (kc >= wcs) & (kc < wcs + NA_WIN_W)
    depth, heads = rpb.shape[:2]
    n_dc = 2 * NA_WIN_W - 1
    width = NA_KBLK + LANES
    period = width + 1
    t = np.arange(period)
    z = np.where(t < NA_KBLK, t, t - period) + NA_WIN_W - 1
    jj, dc = z // GRID_W, z % GRID_W
    in_window = (z >= 0) & (jj < NA_BAND) & (dc < n_dc)
    jj, dc = np.clip(jj, 0, NA_BAND - 1), np.clip(dc, 0, n_dc - 1)
    dr, ok = [], []
    for p in range(3):
        row_ok = (j >= ws[p]) & (j < ws[p] + NA_WIN_H)
        d_pa = np.clip(j - a + off[p] + NA_WIN_H - 1, 0, 2 * NA_WIN_H - 2)
        dr.append(d_pa[:, jj])
        ok.append(row_ok[:, jj] & in_window)
    dr, ok = np.stack(dr), np.stack(ok)
    v = jnp.where(ok, rpb[:, :, dr, dc], NEG)
    v = v.transpose(0, 2, 1, 3, 4)
    b = jnp.tile(v, (1, 1, 1, 1, GRID_W))[..., :GRID_W * width]
    b = b.reshape(depth, 3, heads, NA_QROWS, GRID_W, width)[..., :NA_KBLK]
    b = jnp.where(np.tile(col_ok, (1, NA_BAND)), b, NEG)
    return b.reshape(depth, 3, heads, NA_QBLK, NA_KBLK)


def _neighbourhood_attention(p_na, bias, n_ctx, layer):
    tt = p_na.shape[0]
    n_qb = tt // NA_QBLK
    n_lat_blk = n_qb - 1
    n_band_starts = n_lat_blk - NA_BAND // NA_QROWS + 1
    hp = NA_HEADS // NA_STEP_HEADS
    w = NA_STEP_HEADS * NA_HEAD_DIM

    def bias_idx(h, qb):
        pat = jnp.where(qb <= 1, 0, jnp.where(qb == n_qb - 1, 2, 1))
        return (layer, pat, h, 0, 0)

    return pl.pallas_call(
        functools.partial(_na_kernel, n_ctx=n_ctx, n_band_starts=n_band_starts),
        grid=(hp, n_qb),
        in_specs=[pl.BlockSpec((NA_QBLK, w), lambda h, qb: (qb, h)),
                  pl.BlockSpec((tt, w), lambda h, qb: (0, hp + h)),
                  pl.BlockSpec((tt, w), lambda h, qb: (0, 2 * hp + h)),
                  pl.BlockSpec((None, 1, NA_STEP_HEADS, NA_QBLK, NA_KBLK), bias_idx)],
        out_specs=pl.BlockSpec((NA_QBLK, w), lambda h, qb: (qb, h)),
        out_shape=jax.ShapeDtypeStruct((tt, NA_W), BF16),
        compiler_params=_cparams(("arbitrary", "arbitrary")),
        name="neighbourhood_attention",
    )(p_na, p_na, p_na, bias)


def _log_sigmoid(x):
    return jnp.minimum(x, 0.0) - jnp.log(1.0 + jnp.exp(-jnp.abs(x)))


def _gla_chunk(q, k, v, lr, wg, bg, tri, causal, s_ref, last_row):
    x = jnp.dot(lr, wg, preferred_element_type=F32, precision=lax.Precision.HIGHEST) + bg
    la = _log_sigmoid(x) * (1.0 / GLA_GATE_NORM)
    b = jnp.dot(tri, la, preferred_element_type=F32, precision=lax.Precision.HIGHEST)
    b_last = b[last_row:last_row + 1, :]
    q_t = (q * (GLA_DK ** -0.5) * jnp.exp(b)).astype(BF16)
    k_t = (k * jnp.exp(-b)).astype(BF16)
    k_end = (k * jnp.exp(b_last - b)).astype(BF16)
    dec = jnp.exp(b_last)
    outs = []
    for h in range(GLA_HEADS):
        ks = slice(h * GLA_DK, (h + 1) * GLA_DK)
        vh = v[:, h * GLA_DV:(h + 1) * GLA_DV].astype(BF16)
        att = jnp.where(causal, _dot_nt(q_t[:, ks], k_t[:, ks]), 0.0)
        st = s_ref[h]
        o = _dot(att.astype(BF16), vh) + _dot_nt(q_t[:, ks], st.astype(BF16))
        s_ref[h] = st * dec[:, ks] + _dot_tn(vh, k_end[:, ks])
        outs.append(o)
    return jnp.concatenate(outs, axis=-1)


def _gla_kernel(qf_ref, kf_ref, vf_ref, lf_ref, qb_ref, kb_ref, vb_ref, lb_ref, wg_ref, bg_ref,
                of_ref, ob_ref, sf_ref, sb_ref):
    @pl.when(pl.program_id(0) == 0)
    def _():
        sf_ref[...] = jnp.zeros_like(sf_ref)
        sb_ref[...] = jnp.zeros_like(sb_ref)

    r = lax.broadcasted_iota(jnp.int32, (GLA_CHUNK, GLA_CHUNK), 0)
    c = lax.broadcasted_iota(jnp.int32, (GLA_CHUNK, GLA_CHUNK), 1)
    lower = r >= c
    upper = r <= c
    tri_f = lower.astype(F32)
    tri_b = upper.astype(F32)
    n_chunks = GLA_BLK // GLA_CHUNK
    for ci in range(n_chunks):
        rows = slice(ci * GLA_CHUNK, (ci + 1) * GLA_CHUNK)
        of_ref[rows, :] = _gla_chunk(qf_ref[rows, :], kf_ref[rows, :], vf_ref[rows, :], lf_ref[rows, :],
                                     wg_ref[0], bg_ref[0], tri_f, lower, sf_ref, GLA_CHUNK - 1)
        cb = n_chunks - 1 - ci
        rows = slice(cb * GLA_CHUNK, (cb + 1) * GLA_CHUNK)
        ob_ref[rows, :] = _gla_chunk(qb_ref[rows, :], kb_ref[rows, :], vb_ref[rows, :], lb_ref[rows, :],
                                     wg_ref[1], bg_ref[1], tri_b, upper, sb_ref, 0)


def _gla(p_gla, wg, bg):
    tt = p_gla.shape[0]
    nb = tt // GLA_BLK
    fwd = lambda i: i
    bwd = lambda i: jnp.where(i == 0, 0, nb - i)
    lr_blk = (2 * GLA_KW + 2 * GLA_VW) // 128

    def specs(rowmap):
        return [pl.BlockSpec((GLA_BLK, GLA_KW), lambda i: (rowmap(i), 0)),
                pl.BlockSpec((GLA_BLK, GLA_KW), lambda i: (rowmap(i), 1)),
                pl.BlockSpec((GLA_BLK, GLA_VW), lambda i: (rowmap(i), 1)),
                pl.BlockSpec((GLA_BLK, 128), lambda i: (rowmap(i), lr_blk))]

    return pl.pallas_call(
        _gla_kernel,
        grid=(nb,),
        in_specs=specs(fwd) + specs(bwd) + [
            pl.BlockSpec((2, 128, GLA_KW), lambda i: (0, 0, 0)),
            pl.BlockSpec((2, 1, GLA_KW), lambda i: (0, 0, 0))],
        out_specs=[pl.BlockSpec((GLA_BLK, GLA_VW), lambda i: (fwd(i), 0)),
                   pl.BlockSpec((GLA_BLK, GLA_VW), lambda i: (bwd(i), 0))],
        out_shape=[jax.ShapeDtypeStruct((tt, GLA_VW), F32)] * 2,
        scratch_shapes=[pltpu.VMEM((GLA_HEADS, GLA_DV, GLA_DK), F32)] * 2,
        compiler_params=_cparams(("arbitrary",)),
        name="gla_bidirectional",
    )(*([p_gla] * 8), wg, bg)


def _group_sum_sq(x, ones_bd):
    sq = x * x
    hi = sq.astype(BF16)
    lo = (sq - hi.astype(F32)).astype(BF16)
    return _dot(hi, ones_bd) + _dot(lo, ones_bd)


def _gqa_prep_kernel(p_ref, ta_ref, tb_ref, ones_ref, qt_ref, kz_ref, vt_ref):
    d = GQA_HEAD_DIM
    x = p_ref[...]
    q, qs = x[:, 0:GQA_QW], x[:, GQA_QW:2 * GQA_QW]
    o = 2 * GQA_QW
    k, ks, v = x[:, o:o + GQA_KVW], x[:, o + GQA_KVW:o + 2 * GQA_KVW], x[:, o + 2 * GQA_KVW:o + 3 * GQA_KVW]
    ta, tb = ta_ref[...], tb_ref[...]
    taq = jnp.tile(ta[:, 0:128], (1, GQA_QW // 128))
    tbq = jnp.tile(tb[:, 0:128], (1, GQA_QW // 128))
    rq = lax.rsqrt(_group_sum_sq(q, ones_ref[...]) * (1.0 / d) + EPS)
    qr = rq * (q * taq + qs * tbq)
    rk = lax.rsqrt(_group_sum_sq(k, ones_ref[0:GQA_KVW, 0:GQA_KVW]) * (1.0 / d) + EPS)
    kr = rk * (k * ta[:, 128:256] + ks * tb[:, 128:256])
    qt_ref[...] = qr.T.astype(BF16)
    lane = lax.broadcasted_iota(jnp.int32, (1, GQA_KVW), 1)
    for g in range(GQA_KV_HEADS):
        kz_ref[g] = jnp.where((lane // d) == g, kr, 0.0).astype(BF16)
    vt_ref[...] = v.T.astype(BF16)


def _gqa_prep(p_gqa, ta, tb):
    tt, n = p_gqa.shape
    tm = _pick(tt, (384, 256, 128))
    hid = np.arange(GQA_QW) // GQA_HEAD_DIM
    ones_bd = jnp.asarray(hid[:, None] == hid[None, :], BF16)
    return pl.pallas_call(
        _gqa_prep_kernel,
        grid=(tt // tm,),
        in_specs=[pl.BlockSpec((tm, n), lambda i: (i, 0)),
                  pl.BlockSpec((tm, 256), lambda i: (i, 0)),
                  pl.BlockSpec((tm, 256), lambda i: (i, 0)),
                  pl.BlockSpec((GQA_QW, GQA_QW), lambda i: (0, 0))],
        out_specs=[pl.BlockSpec((GQA_QW, tm), lambda i: (0, i)),
                   pl.BlockSpec((GQA_KV_HEADS, tm, GQA_KVW), lambda i: (0, i, 0)),
                   pl.BlockSpec((GQA_KVW, tm), lambda i: (0, i))],
        out_shape=[jax.ShapeDtypeStruct((GQA_QW, tt), BF16),
                   jax.ShapeDtypeStruct((GQA_KV_HEADS, tt, GQA_KVW), BF16),
                   jax.ShapeDtypeStruct((GQA_KVW, tt), BF16)],
        compiler_params=_cparams(("arbitrary",)),
        name="gqa_prep",
    )(p_gqa, ta, tb, ones_bd)


def _gqa_kernel(qt_ref, kz_ref, vt_ref, o_ref, s0_ref, s1_ref, acc_ref, *, n_ctx):
    qb = pl.program_id(1)
    d = GQA_HEAD_DIM
    group = GQA_HEADS // GQA_KV_HEADS
    kc = GQA_KEY_CHUNK
    tq = qt_ref.shape[1]
    q4 = jnp.concatenate([jnp.concatenate([qt_ref[g * d:(g + 1) * d, :]] * 2, axis=0) for g in range(group)],
                         axis=1)

    def scores(c, s_ref):
        start = pl.multiple_of(c * kc, kc)
        s = _dot(kz_ref[0, pl.ds(start, kc), :], q4)
        s_ref[...] = s
        return jnp.max(s, axis=0, keepdims=True)

    def accumulate(c, s_ref, m_chunk, m):
        m_new = jnp.maximum(m, m_chunk)
        p = jnp.exp2(s_ref[...] - m_new)
        acc_ref[...] = acc_ref[...] * jnp.exp2(m - m_new) + _dot(vt_ref[0, c], p.astype(BF16))
        return m_new

    def attend(n_chunks):
        acc_ref[...] = jnp.zeros_like(acc_ref)
        m = jnp.full((1, group * tq), -jnp.inf, F32)
        mc0 = scores(0, s0_ref)

        def body(j, carry):
            m, mc0 = carry
            for k in range(0, GQA_LOOP_CHUNKS, 2):
                c = GQA_LOOP_CHUNKS * j + k
                mc1 = scores(c + 1, s1_ref)
                m = accumulate(c, s0_ref, mc0, m)
                mc0 = scores(c + 2, s0_ref)
                m = accumulate(c + 1, s1_ref, mc1, m)
            return m, mc0

        m, mc0 = lax.fori_loop(0, (n_chunks - 1) // GQA_LOOP_CHUNKS, body, (m, mc0))
        accumulate(n_chunks - 1, s0_ref, mc0, m)
        out = acc_ref[0:d, :] / acc_ref[d:d + 1, :]
        o_ref[...] = jnp.concatenate([out[:, g * tq:(g + 1) * tq] for g in range(group)],
                                     axis=0).T.astype(o_ref.dtype)

    @pl.when(qb == 0)
    def _():
        attend(n_ctx // kc)

    @pl.when(qb > 0)
    def _():
        attend(kz_ref.shape[1] // kc)


def _gqa_attention(qt, kz, vt, n_ctx):
    tt = qt.shape[1]
    tq = n_ctx
    kc = GQA_KEY_CHUNK
    group = GQA_HEADS // GQA_KV_HEADS
    gw = group * GQA_HEAD_DIM
    assert tt % kc == 0 and n_ctx % kc == 0
    assert (tt // kc) % GQA_LOOP_CHUNKS == 1 and (n_ctx // kc) % GQA_LOOP_CHUNKS == 1
    vt3 = vt.reshape(GQA_KV_HEADS, GQA_HEAD_DIM, tt // kc, kc).transpose(0, 2, 1, 3)
    extra = jnp.zeros((GQA_KV_HEADS, tt // kc, GQA_ONES_ROWS, kc), BF16).at[:, :, 0, :].set(1.0)
    vt3 = jnp.concatenate([vt3, extra], axis=2)
    vrows = GQA_HEAD_DIM + GQA_ONES_ROWS
    return pl.pallas_call(
        functools.partial(_gqa_kernel, n_ctx=n_ctx),
        grid=(GQA_KV_HEADS, tt // tq),
        in_specs=[pl.BlockSpec((gw, tq), lambda g, i: (g, i)),
                  pl.BlockSpec((1, tt, GQA_KVW), lambda g, i: (g, 0, 0)),
                  pl.BlockSpec((1, tt // kc, vrows, kc), lambda g, i: (g, 0, 0, 0))],
        out_specs=pl.BlockSpec((tq, gw), lambda g, i: (i, g)),
        out_shape=jax.ShapeDtypeStruct((tt, GQA_QW), BF16),
        scratch_shapes=[pltpu.VMEM((kc, group * tq), F32), pltpu.VMEM((kc, group * tq), F32),
                        pltpu.VMEM((vrows, group * tq), F32)],
        compiler_params=_cparams(("arbitrary", "arbitrary")),
        name="gqa_attention",
    )(qt, kz, vt3)


def _mix_out_kernel(x_ref, mod_ref, yna_ref, of_ref, ob_ref, gate_ref, ygqa_ref, w_ref, nw_ref,
                    g_ref, b_ref, x1_ref, h2t_ref, *, n_ctx, tm):
    is_ctx = _row_is_ctx(pl.program_id(0), tm, n_ctx)
    o = of_ref[...] + ob_ref[...]
    nw = nw_ref[...]
    parts = []
    for h in range(GLA_HEADS):
        oh = o[:, h * GLA_DV:(h + 1) * GLA_DV]
        parts.append(oh * lax.rsqrt(jnp.mean(oh * oh, axis=-1, keepdims=True) + EPS) * nw)
    gate = gate_ref[...]
    y_gla = (jnp.concatenate(parts, axis=-1) * (gate / (1.0 + jnp.exp(-gate)))).astype(BF16)
    y = (_dot(yna_ref[...], w_ref[0:NA_W, :]) + _dot(y_gla, w_ref[NA_W:NA_W + GLA_VW, :])
         + _dot(ygqa_ref[...], w_ref[NA_W + GLA_VW:, :]))
    x1 = _layer_norm(DEEPNORM_ALPHA * x_ref[...] + _mod_row(mod_ref, 2, is_ctx) * y) * g_ref[...] + b_ref[...]
    x1_ref[...] = x1
    h2 = _layer_norm(x1) * (1.0 + _mod_row(mod_ref, 4, is_ctx)) + _mod_row(mod_ref, 3, is_ctx)
    h2t_ref[...] = h2.T.astype(BF16)


def _mix_out(xa, mod, y_na, o_f, o_b, p_gla, y_gqa, w_out, norm_w, ln_g, ln_b, n_ctx, layer):
    tt, d = xa.shape
    tm = _pick(tt, (384, 256, 128))
    row = lambda i: (i, 0)
    const2 = lambda i: (0, 0)
    return pl.pallas_call(
        functools.partial(_mix_out_kernel, n_ctx=n_ctx, tm=tm),
        grid=(tt // tm,),
        in_specs=[pl.BlockSpec((tm, d), row),
                  pl.BlockSpec((2, 6, d), lambda i: (0, 0, 0)),
                  pl.BlockSpec((tm, NA_W), row),
                  pl.BlockSpec((tm, GLA_VW), row),
                  pl.BlockSpec((tm, GLA_VW), row),
                  pl.BlockSpec((tm, GLA_VW), lambda i: (i, 2)),
                  pl.BlockSpec((tm, GQA_QW), row),
                  _layer_block(w_out, layer),
                  pl.BlockSpec((1, GLA_DV), const2),
                  pl.BlockSpec((1, d), const2),
                  pl.BlockSpec((1, d), const2)],
        out_specs=[pl.BlockSpec((tm, d), row), pl.BlockSpec((d, tm), lambda i: (0, i))],
        out_shape=[jax.ShapeDtypeStruct((tt, d), F32), jax.ShapeDtypeStruct((d, tt), BF16)],
        compiler_params=_cparams(("arbitrary",)),
        name="mixer_out_postln",
    )(xa, mod, y_na, o_f, o_b, p_gla, y_gqa, w_out, norm_w, ln_g, ln_b)


def _top16_exact(s, key_iota):
    cur = s
    rank = jnp.full(s.shape, float(PEER_TOPK), F32)
    vals = []
    for r in range(PEER_TOPK):
        m = jnp.max(cur, axis=0, keepdims=True)
        idx = jnp.min(jnp.where(cur == m, key_iota, float(PEER_NKEYS)), axis=0, keepdims=True)
        sel = key_iota == idx
        rank = jnp.where(sel, float(r), rank)
        cur = jnp.where(sel, -jnp.inf, cur)
        vals.append(m)
    return jnp.concatenate(vals, axis=0), rank


_CODE_SCALE = 2.0 ** 100
_CODE_STEP = 64.0


def _rank_code(r):
    return -_CODE_SCALE * (1.0 + (r + 1) / _CODE_STEP)


def _top16_fast(s):
    cur = s
    vals = []
    for r in range(PEER_TOPK):
        m = jnp.max(cur, axis=0, keepdims=True)
        cur = jnp.where(cur == m, _rank_code(r), cur)
        vals.append(m)
    return jnp.concatenate(vals, axis=0), cur


def _coded(cur):
    return cur <= _rank_code(0)


def _decode_rank(cur):
    rank = (cur * (-1.0 / _CODE_SCALE) - 1.0) * _CODE_STEP - 1.0
    return jnp.where(_coded(cur), rank, float(PEER_TOPK))


def _cand_layout():
    k = PEER_TOPK
    rows = [(0, rb) for rb in range(k)]
    for ra in range(1, 5):
        rows += [(ra, rb) for rb in range(8)]
    rows += [(ra, 0) for ra in range(8, k)]
    rows += [(ra, 1) for ra in range(8)]
    rows += [(ra, 0) for ra in range(8)]
    seen, valid = set(), []
    for pair in rows:
        valid.append(pair not in seen)
        seen.add(pair)
    needed = {(ra, rb) for ra in range(k) for rb in range(k) if (ra + 1) * (rb + 1) <= k}
    assert needed <= seen
    ra = np.array([p[0] for p in rows], np.float32)
    rb = np.array([p[1] for p in rows], np.float32)
    return ra, rb, np.array(valid)


def _select_pairs(a, b, c_ra, c_flat, c_ok, exact):
    k = PEER_TOPK
    tt = a.shape[1]
    blocks = [a[0:1, :] + b]
    for ra in range(1, 5):
        blocks.append(a[ra:ra + 1, :] + b[0:8, :])
    blocks.append(a[8:k, :] + b[0:1, :])
    blocks.append(a[0:8, :] + b[1:2, :])
    blocks.append(a[0:8, :] + b[0:1, :])
    cand = jnp.where(c_ok, jnp.concatenate(blocks, axis=0), -jnp.inf)
    m0 = a[0:1, :] + b[0:1, :]
    z = jnp.zeros((1, tt), F32)
    if exact:
        rank_iota = lax.broadcasted_iota(jnp.int32, (k, tt), 0).astype(F32)
        cnt = jnp.zeros((k, tt), F32)
        for _ in range(k):
            m = jnp.max(cand, axis=0, keepdims=True)
            idx = jnp.min(jnp.where(cand == m, c_flat, 1e9), axis=0, keepdims=True)
            sel = c_flat == idx
            ra_sel = jnp.sum(jnp.where(sel, c_ra, 0.0), axis=0, keepdims=True)
            cnt = cnt + jnp.where(rank_iota == ra_sel, 1.0, 0.0)
            z = z + jnp.exp(m - m0)
            cand = jnp.where(sel, -jnp.inf, cand)
        return cnt, z, None
    for _ in range(k):
        m = jnp.max(cand, axis=0, keepdims=True)
        z = z + jnp.exp(m - m0)
        cand = jnp.where(cand == m, -jnp.inf, cand)
    sel = jnp.where(jnp.logical_and(c_ok, cand == -jnp.inf), 1.0, 0.0)
    low = sel[56:64, :] + sel[64:72, :]
    row8 = lax.broadcasted_iota(jnp.int32, (8, tt), 0)
    sums = [jnp.sum(sel[0:16, :], axis=0, keepdims=True)]
    sums += [jnp.sum(sel[16 + 8 * i:24 + 8 * i, :], axis=0, keepdims=True) for i in range(4)]
    for r, v in enumerate(sums):
        low = jnp.where(row8 == r, v, low)
    cnt = jnp.concatenate([low, sel[48:56, :]], axis=0)
    return cnt, z, jnp.sum(sel, axis=0, keepdims=True)


def _route_head(s1, s2, key_iota, c_ra, c_flat, c_ok, exact):
    k = PEER_TOPK
    if exact:
        a, r1 = _top16_exact(s1, key_iota)
        b, r2 = _top16_exact(s2, key_iota)
    else:
        a, cur1 = _top16_fast(s1)
        b, cur2 = _top16_fast(s2)
        r2 = _decode_rank(cur2)
    cnt, z, n_sel = _select_pairs(a, b, c_ra, c_flat, c_ok, exact)
    n1 = jnp.zeros(s1.shape, F32)
    for r in range(k):
        hit = (r1 == float(r)) if exact else (s1 == a[r:r + 1, :])
        n1 = jnp.where(hit, cnt[r:r + 1, :], n1)
    outs = (r2, jnp.exp(s2 - b[0:1, :]) / z, n1, jnp.exp(s1 - a[0:1, :]))
    if exact:
        return outs, None
    n_coded = lambda cur: jnp.sum(jnp.where(_coded(cur), 1.0, 0.0), axis=0, keepdims=True)
    clean = jnp.logical_and(n_sel == float(k),
                            jnp.logical_and(n_coded(cur1) == float(k), n_coded(cur2) == float(k)))
    return outs, jnp.where(clean, 0.0, 1.0)


def _route_kernel(h_ref, wq_ref, sk_ref, cst_ref, r2_ref, e2_ref, n1_ref, e1_ref):
    tt = h_ref.shape[1]
    key_iota = lax.broadcasted_iota(jnp.int32, (PEER_NKEYS, tt), 0).astype(F32)
    cst = cst_ref[...]
    n_rows = cst.shape[0]
    c_ra = jnp.broadcast_to(cst[:, 0:1], (n_rows, tt))
    c_flat = jnp.broadcast_to(cst[:, 1:2], (n_rows, tt))
    c_ok = jnp.broadcast_to(cst[:, 2:3], (n_rows, tt)) > 0.5
    half = PEER_KEY_DIM // 2

    def head(h, qt_h, exact):
        s1 = _dot(sk_ref[h, 0], qt_h[0:half, :].astype(BF16))
        s2 = _dot(sk_ref[h, 1], qt_h[half:2 * half, :].astype(BF16))
        outs, bad = _route_head(s1, s2, key_iota, c_ra, c_flat, c_ok, exact)
        for ref, val in zip((r2_ref, e2_ref, n1_ref, e1_ref), outs):
            ref[h] = val.astype(ref.dtype)
        return bad

    qt = _dot(wq_ref[...], h_ref[...])
    tied = [jnp.max(head(h, qt[h * PEER_KEY_DIM:(h + 1) * PEER_KEY_DIM, :], False)) for h in range(PEER_HEADS)]
    for h in range(PEER_HEADS):
        @pl.when(tied[h] > 0.0)
        def _(h=h):
            head(h, _dot(wq_ref[h * PEER_KEY_DIM:(h + 1) * PEER_KEY_DIM, :], h_ref[...]), True)


def _peer_route(h2t, wq_t, sub_keys, layer):
    d, tt = h2t.shape
    tb = _pick(tt, (256, 128))
    ra, rb, valid = _cand_layout()
    cst = np.zeros((ra.shape[0], LANES), np.float32)
    flat = np.where(valid, ra * PEER_TOPK + rb, 1000.0 + np.arange(ra.shape[0]))
    cst[:, 0], cst[:, 1], cst[:, 2] = ra, flat, valid
    shp = (PEER_HEADS, PEER_NKEYS, tt)
    out_spec = pl.BlockSpec((PEER_HEADS, PEER_NKEYS, tb), lambda i: (0, 0, i))
    return pl.pallas_call(
        _route_kernel,
        grid=(tt // tb,),
        in_specs=[pl.BlockSpec((d, tb), lambda i: (0, i)),
                  _layer_block(wq_t, layer),
                  _layer_block(sub_keys, layer),
                  pl.BlockSpec(cst.shape, lambda i: (0, 0))],
        out_specs=[out_spec] * 4,
        out_shape=[jax.ShapeDtypeStruct(shp, dt) for dt in (BF16, BF16, F32, F32)],
        compiler_params=_cparams(("arbitrary",)),
        name="peer_route",
    )(h2t, wq_t, sub_keys, jnp.asarray(cst))


def _gelu_tanh(x):
    c = math.sqrt(2.0 / math.pi)
    hx = 0.5 * x
    return hx * jnp.tanh(x * (c + (c * 0.044715) * (x * x))) + hx


def _peer_kernel(h_ref, u_ref, vt_ref, r2_ref, e2_ref, n1_ref, e1_ref, o_ref):
    e = pl.program_id(1)
    eb = u_ref.shape[0]
    n_i = eb // PEER_NKEYS

    @pl.when(e == 0)
    def _():
        o_ref[...] = jnp.zeros_like(o_ref)

    n_sub = eb // PEER_SUB
    sub_rows = lambda s: slice(s * PEER_SUB, (s + 1) * PEER_SUB)
    pre = _dot(u_ref[sub_rows(0), :].astype(BF16), h_ref[...])
    for s in range(n_sub):
        rows = sub_rows(s)
        cur = pre
        if s + 1 < n_sub:
            pre = _dot(u_ref[sub_rows(s + 1), :].astype(BF16), h_ref[...])
        act = _gelu_tanh(cur)
        parts = []
        for ii in range(PEER_SUB // PEER_NKEYS):
            i = e * n_i + s * (PEER_SUB // PEER_NKEYS) + ii
            tb = act.shape[1]
            tiles = PEER_NKEYS // BF16_SUBLANES
            g = None
            for h in range(PEER_HEADS):
                n1 = jnp.broadcast_to(n1_ref[h, pl.ds(i, 1), :], (BF16_SUBLANES, tb)).astype(BF16)
                e1 = jnp.broadcast_to(e1_ref[h, pl.ds(i, 1), :], (BF16_SUBLANES, tb)).astype(BF16)
                r2t = r2_ref[h].reshape(tiles, BF16_SUBLANES, tb)
                e2t = e2_ref[h].reshape(tiles, BF16_SUBLANES, tb)
                t = jnp.where(r2t < n1[None], e2t, jnp.zeros_like(e2t)) * e1[None]
                g = t if g is None else g + t
            a16 = act[ii * PEER_NKEYS:(ii + 1) * PEER_NKEYS, :].astype(BF16).reshape(tiles, BF16_SUBLANES, tb)
            parts.append((a16 * g).reshape(PEER_NKEYS, tb))
        o_ref[...] += _dot(vt_ref[:, rows], jnp.concatenate(parts, axis=0))


def _peer_experts(h2t, u_tab, vt_tab, r2, e2, n1, e1, layer):
    d, tt = h2t.shape
    n_exp = u_tab.shape[1]
    tb = _pick(tt, (768, 512, 256))
    eb = PEER_EXPERT_BLK
    route_spec = pl.BlockSpec((PEER_HEADS, PEER_NKEYS, tb), lambda i, e: (0, 0, i),
                              pipeline_mode=pl.Buffered(1))
    return pl.pallas_call(
        _peer_kernel,
        grid=(tt // tb, n_exp // eb),
        in_specs=[pl.BlockSpec((d, tb), lambda i, e: (0, i), pipeline_mode=pl.Buffered(1)),
                  pl.BlockSpec((None, eb, d), lambda i, e: (layer, e, 0)),
                  pl.BlockSpec((None, d, eb), lambda i, e: (layer, 0, e)),
                  route_spec, route_spec, route_spec, route_spec],
        out_specs=pl.BlockSpec((d, tb), lambda i, e: (0, i)),
        out_shape=jax.ShapeDtypeStruct((d, tt), F32),
        compiler_params=_cparams(("arbitrary", "arbitrary")),
        name="peer_experts",
    )(h2t, u_tab, vt_tab, r2, e2, n1, e1)


def _peer_out_kernel(x_ref, mod_ref, pt_ref, g_ref, b_ref, o_ref, *, n_ctx, tm):
    is_ctx = _row_is_ctx(pl.program_id(0), tm, n_ctx)
    y = pt_ref[...].T
    o_ref[...] = (_layer_norm(DEEPNORM_ALPHA * x_ref[...] + _mod_row(mod_ref, 5, is_ctx) * y)
                  * g_ref[...] + b_ref[...])


def _peer_out(x1, mod, peer_t, ln_g, ln_b, n_ctx):
    tt, d = x1.shape
    tm = _pick(tt, (384, 256, 128))
    return pl.pallas_call(
        functools.partial(_peer_out_kernel, n_ctx=n_ctx, tm=tm),
        grid=(tt // tm,),
        in_specs=[pl.BlockSpec((tm, d), lambda i: (i, 0)),
                  pl.BlockSpec((2, 6, d), lambda i: (0, 0, 0)),
                  pl.BlockSpec((d, tm), lambda i: (0, i)),
                  pl.BlockSpec((1, d), lambda i: (0, 0)),
                  pl.BlockSpec((1, d), lambda i: (0, 0))],
        out_specs=pl.BlockSpec((tm, d), lambda i: (i, 0)),
        out_shape=jax.ShapeDtypeStruct((tt, d), F32),
        compiler_params=_cparams(("arbitrary",)),
        name="peer_out_postln",
    )(x1, mod, peer_t, ln_g, ln_b)


def _rope_tables(n_ctx, n_lat):
    half = GQA_HEAD_DIM // 2
    inv = ROPE_BASE ** (-jnp.arange(0, half, 2, dtype=F32) / half)
    t = jnp.arange(n_lat)
    row = (t // GRID_W).astype(F32)
    col = (t % GRID_W).astype(F32)
    ang = jnp.concatenate([row[:, None] * inv, col[:, None] * inv], axis=-1)
    cos, sin = jnp.cos(ang), jnp.sin(ang)
    cos2 = jnp.concatenate([jnp.ones((n_ctx, 2 * half), F32), jnp.concatenate([cos, cos], -1)], 0)
    sin2 = jnp.concatenate([jnp.zeros((n_ctx, 2 * half), F32), jnp.concatenate([-sin, sin], -1)], 0)
    return cos2, sin2


_EVEN_ODD = np.concatenate([np.arange(0, GQA_HEAD_DIM, 2), np.arange(1, GQA_HEAD_DIM, 2)])
_ODD_EVEN = np.concatenate([np.arange(1, GQA_HEAD_DIM, 2), np.arange(0, GQA_HEAD_DIM, 2)])


def _head_cols(perm, n_heads):
    return np.concatenate([h * GQA_HEAD_DIM + perm for h in range(n_heads)])


def kernel(x, c, ctx, c_ctx, w_mod, b_mod, w_in, na_rpb, gla_gate_w, gla_gate_b, gla_norm_w,
           gqa_q_norm, gqa_k_norm, w_out, ln1_g, ln1_b, peer_wq, peer_subkeys, peer_u, peer_v,
           ln2_g, ln2_b):
    batch, n_lat, d = x.shape
    assert batch == 1 and d == D_MODEL
    n_ctx = ctx.shape[1]
    assert n_lat % NA_QBLK == 0 and n_lat // GRID_W >= NA_BAND + NA_QROWS
    depth = w_mod.shape[0]

    xa = jnp.concatenate([ctx[0], x[0]], axis=0)
    mods = _modulation(jnp.stack([c_ctx, c[0]]), w_mod, b_mod).reshape(depth, 2, 6, d)

    o = np.cumsum([0, NA_W, NA_W, NA_W, GLA_KW, GLA_KW, GLA_VW, GLA_VW, 2 * GLA_GATE_RANK,
                   GQA_QW, GQA_KVW, GQA_KVW])
    w_na = w_in[:, :, o[0]:o[3]].astype(BF16)
    w_gla = w_in[:, :, o[3]:o[8] + (128 - 2 * GLA_GATE_RANK)].astype(BF16)
    wq_c, wk_c, wv_c = w_in[:, :, o[8]:o[9]], w_in[:, :, o[9]:o[10]], w_in[:, :, o[10]:o[11]]
    w_gqa = jnp.concatenate([wq_c[:, :, _head_cols(_EVEN_ODD, GQA_HEADS)],
                             wq_c[:, :, _head_cols(_ODD_EVEN, GQA_HEADS)],
                             wk_c[:, :, _head_cols(_EVEN_ODD, GQA_KV_HEADS)],
                             wk_c[:, :, _head_cols(_ODD_EVEN, GQA_KV_HEADS)],
                             wv_c], -1).astype(BF16)
    wg = jnp.zeros((depth, 2, 128, GLA_KW), F32)
    wg = wg.at[:, 0, 0:GLA_GATE_RANK].set(gla_gate_w[:, 0])
    wg = wg.at[:, 1, GLA_GATE_RANK:2 * GLA_GATE_RANK].set(gla_gate_w[:, 1])
    bg = gla_gate_b.reshape(depth, 2, 1, GLA_KW)
    w_out_b = w_out.astype(BF16)
    wq_t = jnp.swapaxes(peer_wq, 1, 2).astype(BF16)
    sub_keys = peer_subkeys.astype(BF16)
    u_tab = peer_u
    vt_tab = jnp.swapaxes(peer_v, 1, 2).astype(BF16)

    cos2, sin2 = _rope_tables(n_ctx, n_lat)
    q_scale = GQA_HEAD_DIM ** -0.5 * math.log2(math.e)
    na_bias = _na_bias(na_rpb)

    for l in range(depth):
        mod = mods[l]
        p_na, p_gla, p_gqa = _project(xa, mod, (w_na, w_gla, w_gqa), (BF16, F32, F32), n_ctx, l)

        y_na = _neighbourhood_attention(p_na, na_bias, n_ctx, l)
        o_f, o_b = _gla(p_gla, wg[l], bg[l])

        wqn, wkn = gqa_q_norm[l], gqa_k_norm[l]
        ta = jnp.concatenate([jnp.tile(cos2 * wqn[_EVEN_ODD] * q_scale, (1, 2)),
                              jnp.tile(cos2 * wkn[_EVEN_ODD], (1, 2))], -1)
        tb = jnp.concatenate([jnp.tile(sin2 * wqn[_ODD_EVEN] * q_scale, (1, 2)),
                              jnp.tile(sin2 * wkn[_ODD_EVEN], (1, 2))], -1)
        qt, kz, vt = _gqa_prep(p_gqa, ta, tb)
        y_gqa = _gqa_attention(qt, kz, vt, n_ctx)

        x1, h2t = _mix_out(xa, mod, y_na, o_f, o_b, p_gla, y_gqa, w_out_b,
                           gla_norm_w[l].reshape(1, GLA_DV), ln1_g[l].reshape(1, d), ln1_b[l].reshape(1, d),
                           n_ctx, l)
        r2, e2, n1, e1 = _peer_route(h2t, wq_t, sub_keys, l)
        peer_t = _peer_experts(h2t, u_tab, vt_tab, r2, e2, n1, e1, l)
        xa = _peer_out(x1, mod, peer_t, ln2_g[l].reshape(1, d), ln2_b[l].reshape(1, d), n_ctx)

    return xa[n_ctx:][None]
```

```python
import functools
import math

import numpy as np
import jax
import jax.numpy as jnp
from jax import lax
from jax.experimental import pallas as pl
from jax.experimental.pallas import tpu as pltpu

F32 = jnp.float32
BF16 = jnp.bfloat16

D_MODEL = 2048
DEPTH = 4
GRID_W = 64
EPS = 1e-6

NA_HEADS = 8
NA_HEAD_DIM = 64
NA_WIN_H = 8
NA_WIN_W = 16
GLA_HEADS = 4
GLA_DK = 128
GLA_DV = 256
GLA_GATE_RANK = 16
GLA_GATE_NORM = 16.0
GLA_CHUNK = 64
GQA_HEADS = 8
GQA_KV_HEADS = 2
GQA_HEAD_DIM = 64
ROPE_BASE = 10000.0
PEER_HEADS = 8
PEER_NKEYS = 128
PEER_KEY_DIM = 256
PEER_TOPK = 16

NA_W = NA_HEADS * NA_HEAD_DIM
GLA_KW = GLA_HEADS * GLA_DK
GLA_VW = GLA_HEADS * GLA_DV
GQA_QW = GQA_HEADS * GQA_HEAD_DIM
GQA_KVW = GQA_KV_HEADS * GQA_HEAD_DIM
DEEPNORM_ALPHA = (2.0 * DEPTH) ** 0.25

LANES = 128
BF16_SUBLANES = 16
VMEM_LIMIT_BYTES = 58 * 1024 * 1024

NA_QROWS = 4
NA_BAND = NA_QROWS + NA_WIN_H
NA_QBLK = NA_QROWS * GRID_W
NA_KBLK = NA_BAND * GRID_W
NA_STEP_HEADS = 4
GLA_BLK = 256
GQA_KEY_CHUNK = 256
GQA_LOOP_CHUNKS = 8
GQA_ONES_ROWS = 16
PEER_EXPERT_BLK = 1024
PEER_SUB = 512
NEG = -1e30


def _cparams(sem):
    return pltpu.CompilerParams(dimension_semantics=sem, vmem_limit_bytes=VMEM_LIMIT_BYTES)


def _pick(n, cands):
    for c in cands:
        if n % c == 0:
            return c
    raise ValueError(f"no block size in {cands} divides {n}")


def _dot(a, b):
    return jnp.dot(a, b, preferred_element_type=F32)


def _dot_nt(a, b):
    return lax.dot_general(a, b, (((1,), (1,)), ((), ())), preferred_element_type=F32)


def _dot_tn(a, b):
    return lax.dot_general(a, b, (((0,), (0,)), ((), ())), preferred_element_type=F32)


def _layer_norm(x):
    mu = jnp.mean(x, axis=-1, keepdims=True)
    xc = x - mu
    var = jnp.mean(xc * xc, axis=-1, keepdims=True)
    return xc * lax.rsqrt(var + EPS)


def _row_is_ctx(block_idx, tm, n_ctx):
    rows = block_idx * tm + lax.broadcasted_iota(jnp.int32, (tm, 1), 0)
    return rows < n_ctx


def _mod_row(mod_ref, idx, is_ctx):
    return jnp.where(is_ctx, mod_ref[0, idx:idx + 1, :], mod_ref[1, idx:idx + 1, :])


def _mod_kernel(c_ref, w_ref, b_ref, o_ref):
    w = w_ref[0]
    reps = w.shape[1] // LANES
    for m in range(2):
        cb = c_ref[m]
        s = cb / (1.0 + jnp.exp(-cb))
        o_ref[0, m:m + 1, :] = jnp.sum(w * jnp.tile(s, (1, reps)), axis=0, keepdims=True) + b_ref[0]


def _modulation(c2, w_mod, b_mod):
    depth, d, n = w_mod.shape
    tn = _pick(n, (512, 256, 128))
    cb = jnp.broadcast_to(c2[:, :, None], (2, d, LANES))
    return pl.pallas_call(
        _mod_kernel,
        grid=(depth, n // tn),
        in_specs=[pl.BlockSpec((2, d, LANES), lambda l, j: (0, 0, 0)),
                  pl.BlockSpec((1, d, tn), lambda l, j: (l, 0, j)),
                  pl.BlockSpec((1, 1, tn), lambda l, j: (l, 0, j))],
        out_specs=pl.BlockSpec((1, 2, tn), lambda l, j: (l, 0, j)),
        out_shape=jax.ShapeDtypeStruct((depth, 2, n), F32),
        compiler_params=_cparams(("arbitrary", "arbitrary")),
        name="modulation",
    )(cb, w_mod, b_mod.reshape(depth, 1, n))


def _proj_kernel(x_ref, mod_ref, *refs, n_ctx, tm):
    is_ctx = _row_is_ctx(pl.program_id(0), tm, n_ctx)
    xn = _layer_norm(x_ref[...])
    h = (xn * (1.0 + _mod_row(mod_ref, 1, is_ctx)) + _mod_row(mod_ref, 0, is_ctx)).astype(BF16)
    n_groups = len(refs) // 2
    for w_ref, o_ref in zip(refs[:n_groups], refs[n_groups:]):
        o_ref[...] = _dot(h, w_ref[...]).astype(o_ref.dtype)


def _layer_block(stacked, layer, **kw):
    zeros = (0,) * (stacked.ndim - 1)
    return pl.BlockSpec((None,) + stacked.shape[1:], lambda *_: (layer,) + zeros, **kw)


def _project(xa, mod, weights, out_dtypes, n_ctx, layer):
    tt, d = xa.shape
    tm = _pick(tt, (256, 128))
    w_specs = [_layer_block(w, layer, pipeline_mode=pl.Buffered(1)) for w in weights]
    return pl.pallas_call(
        functools.partial(_proj_kernel, n_ctx=n_ctx, tm=tm),
        grid=(tt // tm,),
        in_specs=[pl.BlockSpec((tm, d), lambda i: (i, 0)),
                  pl.BlockSpec((2, 6, d), lambda i: (0, 0, 0))] + w_specs,
        out_specs=[pl.BlockSpec((tm, w.shape[2]), lambda i: (i, 0)) for w in weights],
        out_shape=[jax.ShapeDtypeStruct((tt, w.shape[2]), dt) for w, dt in zip(weights, out_dtypes)],
        compiler_params=_cparams(("arbitrary",)),
        name="ln_mod_project",
    )(xa, mod, *weights)


def _na_kernel(q_ref, k_ref, v_ref, bias_ref, o_ref, *, n_ctx, n_band_starts):
    qb = pl.program_id(1)
    scale = NA_HEAD_DIM ** -0.5
    lane = lax.broadcasted_iota(jnp.int32, (1, LANES), 1)
    heads_per_tile = LANES // NA_HEAD_DIM

    def attend(pair, band_start):
        cols = slice(pair * LANES, (pair + 1) * LANES)
        q = q_ref[:, cols]
        kc, vc = k_ref[0:n_ctx, cols], v_ref[0:n_ctx, cols]
        if band_start is not None:
            kb, vb = k_ref[pl.ds(band_start, NA_KBLK), cols], v_ref[pl.ds(band_start, NA_KBLK), cols]
        outs = []
        for h in range(heads_per_tile):
            qh = jnp.where((lane // NA_HEAD_DIM) == h, q, jnp.zeros_like(q))
            sc = _dot_nt(qh, kc) * scale
            m = jnp.max(sc, axis=-1, keepdims=True)
            if band_start is not None:
                sw = _dot_nt(qh, kb) * scale + bias_ref[0, pair * heads_per_tile + h]
                m = jnp.maximum(m, jnp.max(sw, axis=-1, keepdims=True))
            pc = jnp.exp(sc - m)
            l = jnp.sum(pc, axis=-1, keepdims=True)
            o = _dot(pc.astype(BF16), vc)
            if band_start is not None:
                pw = jnp.exp(sw - m)
                l = l + jnp.sum(pw, axis=-1, keepdims=True)
                o = o + _dot(pw.astype(BF16), vb)
            outs.append(o / l)
        o_ref[:, cols] = jnp.where((lane // NA_HEAD_DIM) == 0, outs[0], outs[1]).astype(o_ref.dtype)

    @pl.when(qb == 0)
    def _():
        for pair in range(NA_STEP_HEADS // heads_per_tile):
            attend(pair, None)

    @pl.when(qb > 0)
    def _():
        start_blk = jnp.clip(qb - 2, 0, n_band_starts - 1)
        start = pl.multiple_of(n_ctx + start_blk * NA_QBLK, NA_QBLK)
        for pair in range(NA_STEP_HEADS // heads_per_tile):
            attend(pair, start)


def _na_bias(rpb):
    a = np.arange(NA_QROWS)[:, None]
    j = np.arange(NA_BAND)[None, :]
    ws = [np.zeros_like(a), a, np.full_like(a, NA_BAND - NA_WIN_H)]
    off = [0, -NA_WIN_H // 2, -(NA_QROWS + NA_WIN_H // 2)]
    qc = np.arange(GRID_W)[:, None]
    kc = np.arange(GRID_W)[None, :]
    wcs = np.clip(qc - NA_WIN_W // 2, 0, GRID_W - NA_WIN_W)
    col_ok = (kc >= wcs) & (kc < wcs + NA_WIN_W)
    depth, heads = rpb.shape[:2]
    n_dr, n_dc = 2 * NA_WIN_H - 1, 2 * NA_WIN_W - 1
    period = 2 * GRID_W + 1
    t = np.arange(period)
    dc = np.where(t <= GRID_W, t, t - period) + NA_WIN_W - 1
    u = jnp.where((dc >= 0) & (dc < n_dc), rpb[..., np.clip(dc, 0, n_dc - 1)], NEG)
    tiles = jnp.tile(u, (1, 1, 1, GRID_W))[..., :GRID_W * 2 * GRID_W]
    tiles = tiles.reshape(depth, heads, n_dr, GRID_W, 2 * GRID_W)[..., :GRID_W]
    tiles = jnp.where(col_ok, tiles, NEG)
    masked = jnp.full((depth, heads, GRID_W, GRID_W), NEG, F32)
    patterns = []
    for p in range(3):
        row_ok = (j >= ws[p]) & (j < ws[p] + NA_WIN_H)
        dr = j - a + off[p] + NA_WIN_H - 1
        blocks = [jnp.concatenate([tiles[:, :, dr[ai, ji]] if row_ok[ai, ji] else masked
                                   for ji in range(NA_BAND)], axis=-1) for ai in range(NA_QROWS)]
        patterns.append(jnp.stack(blocks, axis=2))
    return jnp.stack(patterns, axis=1).reshape(depth, 3, heads, NA_QBLK, NA_KBLK)


def _neighbourhood_attention(p_na, bias, n_ctx, layer):
    tt = p_na.shape[0]
    n_qb = tt // NA_QBLK
    n_lat_blk = n_qb - 1
    n_band_starts = n_lat_blk - NA_BAND // NA_QROWS + 1
    hp = NA_HEADS // NA_STEP_HEADS
    w = NA_STEP_HEADS * NA_HEAD_DIM

    def bias_idx(h, qb):
        pat = jnp.where(qb <= 1, 0, jnp.where(qb == n_qb - 1, 2, 1))
        return (layer, pat, h, 0, 0)

    return pl.pallas_call(
        functools.partial(_na_kernel, n_ctx=n_ctx, n_band_starts=n_band_starts),
        grid=(hp, n_qb),
        in_specs=[pl.BlockSpec((NA_QBLK, w), lambda h, qb: (qb, h)),
                  pl.BlockSpec((tt, w), lambda h, qb: (0, hp + h)),
                  pl.BlockSpec((tt, w), lambda h, qb: (0, 2 * hp + h)),
                  pl.BlockSpec((None, 1, NA_STEP_HEADS, NA_QBLK, NA_KBLK), bias_idx)],
        out_specs=pl.BlockSpec((NA_QBLK, w), lambda h, qb: (qb, h)),
        out_shape=jax.ShapeDtypeStruct((tt, NA_W), BF16),
        compiler_params=_cparams(("arbitrary", "arbitrary")),
        name="neighbourhood_attention",
    )(p_na, p_na, p_na, bias)


def _log_sigmoid(x):
    return jnp.minimum(x, 0.0) - jnp.log(1.0 + jnp.exp(-jnp.abs(x)))


def _split3(x):
    hi = x.astype(BF16)
    r = x - hi.astype(F32)
    mid = r.astype(BF16)
    lo = (r - mid.astype(F32)).astype(BF16)
    return hi, mid, lo


def _gla_prep(q, k, lr, wg, bg, tri, last_row):
    lh, ll, _ = _split3(lr)
    x = _dot(jnp.concatenate([lh, ll, lh], axis=1), wg) + bg
    la = _log_sigmoid(x) * (1.0 / GLA_GATE_NORM)
    b = _dot(tri, jnp.concatenate(_split3(la), axis=0))
    b_last = b[last_row:last_row + 1, :]
    q_t = (q * (GLA_DK ** -0.5) * jnp.exp(b)).astype(BF16)
    k_t = (k * jnp.exp(-b)).astype(BF16)
    k_end = (k * jnp.exp(b_last - b)).astype(BF16)
    return q_t, k_t, k_end, jnp.exp(b_last)


def _gla_attend(prep, v, causal, s_ref):
    q_t, k_t, k_end, dec = prep
    outs = []
    for h in range(GLA_HEADS):
        ks = slice(h * GLA_DK, (h + 1) * GLA_DK)
        vh = v[:, h * GLA_DV:(h + 1) * GLA_DV].astype(BF16)
        att = jnp.where(causal, _dot_nt(q_t[:, ks], k_t[:, ks]), 0.0)
        st = s_ref[h]
        o = _dot(att.astype(BF16), vh) + _dot_nt(q_t[:, ks], st.astype(BF16))
        s_ref[h] = st * dec[:, ks] + _dot_tn(vh, k_end[:, ks])
        outs.append(o)
    return jnp.concatenate(outs, axis=-1)


def _gla_kernel(qf_ref, kf_ref, vf_ref, lf_ref, qb_ref, kb_ref, vb_ref, lb_ref, wg_ref, bg_ref,
                of_ref, ob_ref, sf_ref, sb_ref):
    @pl.when(pl.program_id(0) == 0)
    def _():
        sf_ref[...] = jnp.zeros_like(sf_ref)
        sb_ref[...] = jnp.zeros_like(sb_ref)

    r = lax.broadcasted_iota(jnp.int32, (GLA_CHUNK, GLA_CHUNK), 0)
    c = lax.broadcasted_iota(jnp.int32, (GLA_CHUNK, GLA_CHUNK), 1)
    lower = r >= c
    upper = r <= c
    tri_f = jnp.tile(lower.astype(F32), (1, 3)).astype(BF16)
    tri_b = jnp.tile(upper.astype(F32), (1, 3)).astype(BF16)
    n_chunks = GLA_BLK // GLA_CHUNK
    rows_of = lambda ci: slice(ci * GLA_CHUNK, (ci + 1) * GLA_CHUNK)

    def prep_f(ci):
        rows = rows_of(ci)
        return _gla_prep(qf_ref[rows, :], kf_ref[rows, :], lf_ref[rows, :], wg_ref[0], bg_ref[0], tri_f,
                         GLA_CHUNK - 1)

    def prep_b(ci):
        rows = rows_of(ci)
        return _gla_prep(qb_ref[rows, :], kb_ref[rows, :], lb_ref[rows, :], wg_ref[1], bg_ref[1], tri_b, 0)

    pf, pb = prep_f(0), prep_b(n_chunks - 1)
    for ci in range(n_chunks):
        cb = n_chunks - 1 - ci
        cur_f, cur_b = pf, pb
        if ci + 1 < n_chunks:
            pf, pb = prep_f(ci + 1), prep_b(cb - 1)
        of_ref[rows_of(ci), :] = _gla_attend(cur_f, vf_ref[rows_of(ci), :], lower, sf_ref)
        ob_ref[rows_of(cb), :] = _gla_attend(cur_b, vb_ref[rows_of(cb), :], upper, sb_ref)


def _gla(p_gla, wg, bg):
    tt = p_gla.shape[0]
    nb = tt // GLA_BLK
    fwd = lambda i: i
    bwd = lambda i: jnp.where(i == 0, 0, nb - i)
    lr_blk = (2 * GLA_KW + 2 * GLA_VW) // 128

    def specs(rowmap):
        return [pl.BlockSpec((GLA_BLK, GLA_KW), lambda i: (rowmap(i), 0)),
                pl.BlockSpec((GLA_BLK, GLA_KW), lambda i: (rowmap(i), 1)),
                pl.BlockSpec((GLA_BLK, GLA_VW), lambda i: (rowmap(i), 1)),
                pl.BlockSpec((GLA_BLK, 128), lambda i: (rowmap(i), lr_blk))]

    return pl.pallas_call(
        _gla_kernel,
        grid=(nb,),
        in_specs=specs(fwd) + specs(bwd) + [
            pl.BlockSpec((2, 3 * 128, GLA_KW), lambda i: (0, 0, 0)),
            pl.BlockSpec((2, 1, GLA_KW), lambda i: (0, 0, 0))],
        out_specs=[pl.BlockSpec((GLA_BLK, GLA_VW), lambda i: (fwd(i), 0)),
                   pl.BlockSpec((GLA_BLK, GLA_VW), lambda i: (bwd(i), 0))],
        out_shape=[jax.ShapeDtypeStruct((tt, GLA_VW), F32)] * 2,
        scratch_shapes=[pltpu.VMEM((GLA_HEADS, GLA_DV, GLA_DK), F32)] * 2,
        compiler_params=_cparams(("arbitrary",)),
        name="gla_bidirectional",
    )(*([p_gla] * 8), wg, bg)


def _group_sum_sq(x, ones_bd):
    sq = x * x
    hi = sq.astype(BF16)
    lo = (sq - hi.astype(F32)).astype(BF16)
    return _dot(hi, ones_bd) + _dot(lo, ones_bd)


def _gqa_prep_kernel(p_ref, ta_ref, tb_ref, ones_ref, qt_ref, kz_ref, vt_ref):
    d = GQA_HEAD_DIM
    x = p_ref[...]
    q, qs = x[:, 0:GQA_QW], x[:, GQA_QW:2 * GQA_QW]
    o = 2 * GQA_QW
    k, ks, v = x[:, o:o + GQA_KVW], x[:, o + GQA_KVW:o + 2 * GQA_KVW], x[:, o + 2 * GQA_KVW:o + 3 * GQA_KVW]
    ta, tb = ta_ref[...], tb_ref[...]
    taq = jnp.tile(ta[:, 0:128], (1, GQA_QW // 128))
    tbq = jnp.tile(tb[:, 0:128], (1, GQA_QW // 128))
    rq = lax.rsqrt(_group_sum_sq(q, ones_ref[...]) * (1.0 / d) + EPS)
    qr = rq * (q * taq + qs * tbq)
    rk = lax.rsqrt(_group_sum_sq(k, ones_ref[0:GQA_KVW, 0:GQA_KVW]) * (1.0 / d) + EPS)
    kr = rk * (k * ta[:, 128:256] + ks * tb[:, 128:256])
    qt_ref[...] = qr.T.astype(BF16)
    lane = lax.broadcasted_iota(jnp.int32, (1, GQA_KVW), 1)
    for g in range(GQA_KV_HEADS):
        kz_ref[g] = jnp.where((lane // d) == g, kr, 0.0).astype(BF16)
    vt_ref[...] = v.T.astype(BF16)


def _gqa_prep(p_gqa, ta, tb):
    tt, n = p_gqa.shape
    tm = _pick(tt, (384, 256, 128))
    hid = np.arange(GQA_QW) // GQA_HEAD_DIM
    ones_bd = jnp.asarray(hid[:, None] == hid[None, :], BF16)
    return pl.pallas_call(
        _gqa_prep_kernel,
        grid=(tt // tm,),
        in_specs=[pl.BlockSpec((tm, n), lambda i: (i, 0)),
                  pl.BlockSpec((tm, 256), lambda i: (i, 0)),
                  pl.BlockSpec((tm, 256), lambda i: (i, 0)),
                  pl.BlockSpec((GQA_QW, GQA_QW), lambda i: (0, 0))],
        out_specs=[pl.BlockSpec((GQA_QW, tm), lambda i: (0, i)),
                   pl.BlockSpec((GQA_KV_HEADS, tm, GQA_KVW), lambda i: (0, i, 0)),
                   pl.BlockSpec((GQA_KVW, tm), lambda i: (0, i))],
        out_shape=[jax.ShapeDtypeStruct((GQA_QW, tt), BF16),
                   jax.ShapeDtypeStruct((GQA_KV_HEADS, tt, GQA_KVW), BF16),
                   jax.ShapeDtypeStruct((GQA_KVW, tt), BF16)],
        compiler_params=_cparams(("arbitrary",)),
        name="gqa_prep",
    )(p_gqa, ta, tb, ones_bd)


def _gqa_kernel(qt_ref, kz_ref, vt_ref, o_ref, s0_ref, s1_ref, acc_ref, *, n_ctx):
    qb = pl.program_id(1)
    d = GQA_HEAD_DIM
    group = GQA_HEADS // GQA_KV_HEADS
    kc = GQA_KEY_CHUNK
    tq = qt_ref.shape[1]
    q4 = jnp.concatenate([jnp.concatenate([qt_ref[g * d:(g + 1) * d, :]] * 2, axis=0) for g in range(group)],
                         axis=1)

    def scores(c, s_ref):
        start = pl.multiple_of(c * kc, kc)
        s = _dot(kz_ref[0, pl.ds(start, kc), :], q4)
        s_ref[...] = s
        return jnp.max(s, axis=0, keepdims=True)

    def accumulate(c, s_ref, m_chunk, m):
        m_new = jnp.maximum(m, m_chunk)
        p = jnp.exp2(s_ref[...] - m_new)
        acc_ref[...] = acc_ref[...] * jnp.exp2(m - m_new) + _dot(vt_ref[0, c], p.astype(BF16))
        return m_new

    def attend(n_chunks):
        acc_ref[...] = jnp.zeros_like(acc_ref)
        m = jnp.full((1, group * tq), -jnp.inf, F32)
        mc0 = scores(0, s0_ref)

        def body(j, carry):
            m, mc0 = carry
            for k in range(0, GQA_LOOP_CHUNKS, 2):
                c = GQA_LOOP_CHUNKS * j + k
                mc1 = scores(c + 1, s1_ref)
                m = accumulate(c, s0_ref, mc0, m)
                mc0 = scores(c + 2, s0_ref)
                m = accumulate(c + 1, s1_ref, mc1, m)
            return m, mc0

        m, mc0 = lax.fori_loop(0, (n_chunks - 1) // GQA_LOOP_CHUNKS, body, (m, mc0))
        accumulate(n_chunks - 1, s0_ref, mc0, m)
        out = acc_ref[0:d, :] / acc_ref[d:d + 1, :]
        o_ref[...] = jnp.concatenate([out[:, g * tq:(g + 1) * tq] for g in range(group)],
                                     axis=0).T.astype(o_ref.dtype)

    @pl.when(qb == 0)
    def _():
        attend(n_ctx // kc)

    @pl.when(qb > 0)
    def _():
        attend(kz_ref.shape[1] // kc)


def _gqa_attention(qt, kz, vt, n_ctx):
    tt = qt.shape[1]
    tq = n_ctx
    kc = GQA_KEY_CHUNK
    group = GQA_HEADS // GQA_KV_HEADS
    gw = group * GQA_HEAD_DIM
    assert tt % kc == 0 and n_ctx % kc == 0
    assert (tt // kc) % GQA_LOOP_CHUNKS == 1 and (n_ctx // kc) % GQA_LOOP_CHUNKS == 1
    vt3 = vt.reshape(GQA_KV_HEADS, GQA_HEAD_DIM, tt // kc, kc).transpose(0, 2, 1, 3)
    extra = jnp.zeros((GQA_KV_HEADS, tt // kc, GQA_ONES_ROWS, kc), BF16).at[:, :, 0, :].set(1.0)
    vt3 = jnp.concatenate([vt3, extra], axis=2)
    vrows = GQA_HEAD_DIM + GQA_ONES_ROWS
    return pl.pallas_call(
        functools.partial(_gqa_kernel, n_ctx=n_ctx),
        grid=(GQA_KV_HEADS, tt // tq),
        in_specs=[pl.BlockSpec((gw, tq), lambda g, i: (g, i)),
                  pl.BlockSpec((1, tt, GQA_KVW), lambda g, i: (g, 0, 0)),
                  pl.BlockSpec((1, tt // kc, vrows, kc), lambda g, i: (g, 0, 0, 0))],
        out_specs=pl.BlockSpec((tq, gw), lambda g, i: (i, g)),
        out_shape=jax.ShapeDtypeStruct((tt, GQA_QW), BF16),
        scratch_shapes=[pltpu.VMEM((kc, group * tq), F32), pltpu.VMEM((kc, group * tq), F32),
                        pltpu.VMEM((vrows, group * tq), F32)],
        compiler_params=_cparams(("arbitrary", "arbitrary")),
        name="gqa_attention",
    )(qt, kz, vt3)


def _mix_out_kernel(x_ref, mod_ref, yna_ref, of_ref, ob_ref, gate_ref, ygqa_ref, w_ref, nw_ref,
                    g_ref, b_ref, x1_ref, h2t_ref, *, n_ctx, tm):
    is_ctx = _row_is_ctx(pl.program_id(0), tm, n_ctx)
    o = of_ref[...] + ob_ref[...]
    nw = nw_ref[...]
    parts = []
    for h in range(GLA_HEADS):
        oh = o[:, h * GLA_DV:(h + 1) * GLA_DV]
        parts.append(oh * lax.rsqrt(jnp.mean(oh * oh, axis=-1, keepdims=True) + EPS) * nw)
    gate = gate_ref[...]
    y_gla = (jnp.concatenate(parts, axis=-1) * (gate / (1.0 + jnp.exp(-gate)))).astype(BF16)
    y = (_dot(yna_ref[...], w_ref[0:NA_W, :]) + _dot(y_gla, w_ref[NA_W:NA_W + GLA_VW, :])
         + _dot(ygqa_ref[...], w_ref[NA_W + GLA_VW:, :]))
    x1 = _layer_norm(DEEPNORM_ALPHA * x_ref[...] + _mod_row(mod_ref, 2, is_ctx) * y) * g_ref[...] + b_ref[...]
    x1_ref[...] = x1
    h2 = _layer_norm(x1) * (1.0 + _mod_row(mod_ref, 4, is_ctx)) + _mod_row(mod_ref, 3, is_ctx)
    h2t_ref[...] = h2.T.astype(BF16)


def _mix_out(xa, mod, y_na, o_f, o_b, p_gla, y_gqa, w_out, norm_w, ln_g, ln_b, n_ctx, layer):
    tt, d = xa.shape
    tm = _pick(tt, (384, 256, 128))
    row = lambda i: (i, 0)
    const2 = lambda i: (0, 0)
    return pl.pallas_call(
        functools.partial(_mix_out_kernel, n_ctx=n_ctx, tm=tm),
        grid=(tt // tm,),
        in_specs=[pl.BlockSpec((tm, d), row),
                  pl.BlockSpec((2, 6, d), lambda i: (0, 0, 0)),
                  pl.BlockSpec((tm, NA_W), row),
                  pl.BlockSpec((tm, GLA_VW), row),
                  pl.BlockSpec((tm, GLA_VW), row),
                  pl.BlockSpec((tm, GLA_VW), lambda i: (i, 2)),
                  pl.BlockSpec((tm, GQA_QW), row),
                  _layer_block(w_out, layer),
                  pl.BlockSpec((1, GLA_DV), const2),
                  pl.BlockSpec((1, d), const2),
                  pl.BlockSpec((1, d), const2)],
        out_specs=[pl.BlockSpec((tm, d), row), pl.BlockSpec((d, tm), lambda i: (0, i))],
        out_shape=[jax.ShapeDtypeStruct((tt, d), F32), jax.ShapeDtypeStruct((d, tt), BF16)],
        compiler_params=_cparams(("arbitrary",)),
        name="mixer_out_postln",
    )(xa, mod, y_na, o_f, o_b, p_gla, y_gqa, w_out, norm_w, ln_g, ln_b)


def _top16_exact(s, key_iota):
    cur = s
    rank = jnp.full(s.shape, float(PEER_TOPK), F32)
    vals = []
    for r in range(PEER_TOPK):
        m = jnp.max(cur, axis=0, keepdims=True)
        idx = jnp.min(jnp.where(cur == m, key_iota, float(PEER_NKEYS)), axis=0, keepdims=True)
        sel = key_iota == idx
        rank = jnp.where(sel, float(r), rank)
        cur = jnp.where(sel, -jnp.inf, cur)
        vals.append(m)
    return jnp.concatenate(vals, axis=0), rank


_CODE_SCALE = 2.0 ** 100
_CODE_STEP = 64.0


def _rank_code(r):
    return -_CODE_SCALE * (1.0 + (r + 1) / _CODE_STEP)


def _top16_fast(s):
    cur = s
    vals = []
    for r in range(PEER_TOPK):
        m = jnp.max(cur, axis=0, keepdims=True)
        cur = jnp.where(cur == m, _rank_code(r), cur)
        vals.append(m)
    return jnp.concatenate(vals, axis=0), cur


def _coded(cur):
    return cur <= _rank_code(0)


def _decode_rank(cur):
    rank = (cur * (-1.0 / _CODE_SCALE) - 1.0) * _CODE_STEP - 1.0
    return jnp.where(_coded(cur), rank, float(PEER_TOPK))


def _cand_layout():
    k = PEER_TOPK
    rows = [(0, rb) for rb in range(k)]
    for ra in range(1, 5):
        rows += [(ra, rb) for rb in range(8)]
    rows += [(ra, 0) for ra in range(8, k)]
    rows += [(ra, 1) for ra in range(8)]
    rows += [(ra, 0) for ra in range(8)]
    seen, valid = set(), []
    for pair in rows:
        valid.append(pair not in seen)
        seen.add(pair)
    needed = {(ra, rb) for ra in range(k) for rb in range(k) if (ra + 1) * (rb + 1) <= k}
    assert needed <= seen
    ra = np.array([p[0] for p in rows], np.float32)
    rb = np.array([p[1] for p in rows], np.float32)
    return ra, rb, np.array(valid)


def _select_pairs(a, b, c_ra, c_flat, c_ok, exact):
    k = PEER_TOPK
    tt = a.shape[1]
    blocks = [a[0:1, :] + b]
    for ra in range(1, 5):
        blocks.append(a[ra:ra + 1, :] + b[0:8, :])
    blocks.append(a[8:k, :] + b[0:1, :])
    blocks.append(a[0:8, :] + b[1:2, :])
    blocks.append(a[0:8, :] + b[0:1, :])
    cand = jnp.where(c_ok, jnp.concatenate(blocks, axis=0), -jnp.inf)
    m0 = a[0:1, :] + b[0:1, :]
    z = jnp.zeros((1, tt), F32)
    if exact:
        rank_iota = lax.broadcasted_iota(jnp.int32, (k, tt), 0).astype(F32)
        cnt = jnp.zeros((k, tt), F32)
        for _ in range(k):
            m = jnp.max(cand, axis=0, keepdims=True)
            idx = jnp.min(jnp.where(cand == m, c_flat, 1e9), axis=0, keepdims=True)
            sel = c_flat == idx
            ra_sel = jnp.sum(jnp.where(sel, c_ra, 0.0), axis=0, keepdims=True)
            cnt = cnt + jnp.where(rank_iota == ra_sel, 1.0, 0.0)
            z = z + jnp.exp(m - m0)
            cand = jnp.where(sel, -jnp.inf, cand)
        return cnt, z, None
    for _ in range(k):
        m = jnp.max(cand, axis=0, keepdims=True)
        z = z + jnp.exp(m - m0)
        cand = jnp.where(cand == m, -jnp.inf, cand)
    sel = jnp.where(jnp.logical_and(c_ok, cand == -jnp.inf), 1.0, 0.0)
    low = sel[56:64, :] + sel[64:72, :]
    row8 = lax.broadcasted_iota(jnp.int32, (8, tt), 0)
    sums = [jnp.sum(sel[0:16, :], axis=0, keepdims=True)]
    sums += [jnp.sum(sel[16 + 8 * i:24 + 8 * i, :], axis=0, keepdims=True) for i in range(4)]
    for r, v in enumerate(sums):
        low = jnp.where(row8 == r, v, low)
    cnt = jnp.concatenate([low, sel[48:56, :]], axis=0)
    return cnt, z, jnp.sum(sel, axis=0, keepdims=True)


def _route_head(s1, s2, key_iota, c_ra, c_flat, c_ok, exact):
    k = PEER_TOPK
    if exact:
        a, r1 = _top16_exact(s1, key_iota)
        b, r2 = _top16_exact(s2, key_iota)
    else:
        a, cur1 = _top16_fast(s1)
        b, cur2 = _top16_fast(s2)
        r2 = _decode_rank(cur2)
    cnt, z, n_sel = _select_pairs(a, b, c_ra, c_flat, c_ok, exact)
    n1 = jnp.zeros(s1.shape, F32)
    for r in range(k):
        hit = (r1 == float(r)) if exact else (s1 == a[r:r + 1, :])
        n1 = jnp.where(hit, cnt[r:r + 1, :], n1)
    outs = (r2, jnp.exp(s2 - b[0:1, :]) / z, n1, jnp.exp(s1 - a[0:1, :]))
    if exact:
        return outs, None
    n_coded = lambda cur: jnp.sum(jnp.where(_coded(cur), 1.0, 0.0), axis=0, keepdims=True)
    clean = jnp.logical_and(n_sel == float(k),
                            jnp.logical_and(n_coded(cur1) == float(k), n_coded(cur2) == float(k)))
    return outs, jnp.where(clean, 0.0, 1.0)


def _route_kernel(h_ref, wq_ref, sk_ref, cst_ref, r2_ref, e2_ref, n1_ref, e1_ref):
    tt = h_ref.shape[1]
    key_iota = lax.broadcasted_iota(jnp.int32, (PEER_NKEYS, tt), 0).astype(F32)
    cst = cst_ref[...]
    n_rows = cst.shape[0]
    c_ra = jnp.broadcast_to(cst[:, 0:1], (n_rows, tt))
    c_flat = jnp.broadcast_to(cst[:, 1:2], (n_rows, tt))
    c_ok = jnp.broadcast_to(cst[:, 2:3], (n_rows, tt)) > 0.5
    half = PEER_KEY_DIM // 2

    def head(h, qt_h, exact):
        s1 = _dot(sk_ref[h, 0], qt_h[0:half, :].astype(BF16))
        s2 = _dot(sk_ref[h, 1], qt_h[half:2 * half, :].astype(BF16))
        outs, bad = _route_head(s1, s2, key_iota, c_ra, c_flat, c_ok, exact)
        for ref, val in zip((r2_ref, e2_ref, n1_ref, e1_ref), outs):
            ref[h] = val.astype(ref.dtype)
        return bad

    qt = _dot(wq_ref[...], h_ref[...])
    tied = [jnp.max(head(h, qt[h * PEER_KEY_DIM:(h + 1) * PEER_KEY_DIM, :], False)) for h in range(PEER_HEADS)]
    for h in range(PEER_HEADS):
        @pl.when(tied[h] > 0.0)
        def _(h=h):
            head(h, _dot(wq_ref[h * PEER_KEY_DIM:(h + 1) * PEER_KEY_DIM, :], h_ref[...]), True)


def _peer_route(h2t, wq_t, sub_keys, layer):
    d, tt = h2t.shape
    tb = _pick(tt, (256, 128))
    ra, rb, valid = _cand_layout()
    cst = np.zeros((ra.shape[0], LANES), np.float32)
    flat = np.where(valid, ra * PEER_TOPK + rb, 1000.0 + np.arange(ra.shape[0]))
    cst[:, 0], cst[:, 1], cst[:, 2] = ra, flat, valid
    shp = (PEER_HEADS, PEER_NKEYS, tt)
    out_spec = pl.BlockSpec((PEER_HEADS, PEER_NKEYS, tb), lambda i: (0, 0, i))
    return pl.pallas_call(
        _route_kernel,
        grid=(tt // tb,),
        in_specs=[pl.BlockSpec((d, tb), lambda i: (0, i)),
                  _layer_block(wq_t, layer),
                  _layer_block(sub_keys, layer),
                  pl.BlockSpec(cst.shape, lambda i: (0, 0))],
        out_specs=[out_spec] * 4,
        out_shape=[jax.ShapeDtypeStruct(shp, dt) for dt in (BF16, BF16, F32, F32)],
        compiler_params=_cparams(("arbitrary",)),
        name="peer_route",
    )(h2t, wq_t, sub_keys, jnp.asarray(cst))


def _gelu_tanh(x):
    c = math.sqrt(2.0 / math.pi)
    hx = 0.5 * x
    return hx * jnp.tanh(x * (c + (c * 0.044715) * (x * x))) + hx


def _peer_kernel(h_ref, u_ref, vt_ref, r2_ref, e2_ref, n1_ref, e1_ref, o_ref):
    e = pl.program_id(1)
    eb = u_ref.shape[0]
    n_i = eb // PEER_NKEYS

    @pl.when(e == 0)
    def _():
        o_ref[...] = jnp.zeros_like(o_ref)

    n_sub = eb // PEER_SUB
    sub_rows = lambda s: slice(s * PEER_SUB, (s + 1) * PEER_SUB)
    pre = _dot(u_ref[sub_rows(0), :].astype(BF16), h_ref[...])
    for s in range(n_sub):
        rows = sub_rows(s)
        cur = pre
        if s + 1 < n_sub:
            pre = _dot(u_ref[sub_rows(s + 1), :].astype(BF16), h_ref[...])
        act = _gelu_tanh(cur)
        parts = []
        for ii in range(PEER_SUB // PEER_NKEYS):
            i = e * n_i + s * (PEER_SUB // PEER_NKEYS) + ii
            tb = act.shape[1]
            tiles = PEER_NKEYS // BF16_SUBLANES
            g = None
            for h in range(PEER_HEADS):
                n1 = jnp.broadcast_to(n1_ref[h, pl.ds(i, 1), :], (BF16_SUBLANES, tb)).astype(BF16)
                e1 = jnp.broadcast_to(e1_ref[h, pl.ds(i, 1), :], (BF16_SUBLANES, tb)).astype(BF16)
                r2t = r2_ref[h].reshape(tiles, BF16_SUBLANES, tb)
                e2t = e2_ref[h].reshape(tiles, BF16_SUBLANES, tb)
                t = jnp.where(r2t < n1[None], e2t, jnp.zeros_like(e2t)) * e1[None]
                g = t if g is None else g + t
            a16 = act[ii * PEER_NKEYS:(ii + 1) * PEER_NKEYS, :].astype(BF16).reshape(tiles, BF16_SUBLANES, tb)
            parts.append((a16 * g).reshape(PEER_NKEYS, tb))
        o_ref[...] += _dot(vt_ref[:, rows], jnp.concatenate(parts, axis=0))


def _peer_experts(h2t, u_tab, vt_tab, r2, e2, n1, e1, layer):
    d, tt = h2t.shape
    n_exp = u_tab.shape[1]
    tb = _pick(tt, (768, 512, 256))
    eb = PEER_EXPERT_BLK
    route_spec = pl.BlockSpec((PEER_HEADS, PEER_NKEYS, tb), lambda i, e: (0, 0, i),
                              pipeline_mode=pl.Buffered(1))
    return pl.pallas_call(
        _peer_kernel,
        grid=(tt // tb, n_exp // eb),
        in_specs=[pl.BlockSpec((d, tb), lambda i, e: (0, i), pipeline_mode=pl.Buffered(1)),
                  pl.BlockSpec((None, eb, d), lambda i, e: (layer, e, 0)),
                  pl.BlockSpec((None, d, eb), lambda i, e: (layer, 0, e)),
                  route_spec, route_spec, route_spec, route_spec],
        out_specs=pl.BlockSpec((d, tb), lambda i, e: (0, i)),
        out_shape=jax.ShapeDtypeStruct((d, tt), F32),
        compiler_params=_cparams(("arbitrary", "arbitrary")),
        name="peer_experts",
    )(h2t, u_tab, vt_tab, r2, e2, n1, e1)


def _peer_out_kernel(x_ref, mod_ref, pt_ref, g_ref, b_ref, o_ref, *, n_ctx, tm):
    is_ctx = _row_is_ctx(pl.program_id(0), tm, n_ctx)
    y = pt_ref[...].T
    o_ref[...] = (_layer_norm(DEEPNORM_ALPHA * x_ref[...] + _mod_row(mod_ref, 5, is_ctx) * y)
                  * g_ref[...] + b_ref[...])


def _peer_out(x1, mod, peer_t, ln_g, ln_b, n_ctx):
    tt, d = x1.shape
    tm = _pick(tt, (384, 256, 128))
    return pl.pallas_call(
        functools.partial(_peer_out_kernel, n_ctx=n_ctx, tm=tm),
        grid=(tt // tm,),
        in_specs=[pl.BlockSpec((tm, d), lambda i: (i, 0)),
                  pl.BlockSpec((2, 6, d), lambda i: (0, 0, 0)),
                  pl.BlockSpec((d, tm), lambda i: (0, i)),
                  pl.BlockSpec((1, d), lambda i: (0, 0)),
                  pl.BlockSpec((1, d), lambda i: (0, 0))],
        out_specs=pl.BlockSpec((tm, d), lambda i: (i, 0)),
        out_shape=jax.ShapeDtypeStruct((tt, d), F32),
        compiler_params=_cparams(("arbitrary",)),
        name="peer_out_postln",
    )(x1, mod, peer_t, ln_g, ln_b)


def _rope_tables(n_ctx, n_lat):
    half = GQA_HEAD_DIM // 2
    inv = ROPE_BASE ** (-jnp.arange(0, half, 2, dtype=F32) / half)
    t = jnp.arange(n_lat)
    row = (t // GRID_W).astype(F32)
    col = (t % GRID_W).astype(F32)
    ang = jnp.concatenate([row[:, None] * inv, col[:, None] * inv], axis=-1)
    cos, sin = jnp.cos(ang), jnp.sin(ang)
    cos2 = jnp.concatenate([jnp.ones((n_ctx, 2 * half), F32), jnp.concatenate([cos, cos], -1)], 0)
    sin2 = jnp.concatenate([jnp.zeros((n_ctx, 2 * half), F32), jnp.concatenate([-sin, sin], -1)], 0)
    return cos2, sin2


_EVEN_ODD = np.concatenate([np.arange(0, GQA_HEAD_DIM, 2), np.arange(1, GQA_HEAD_DIM, 2)])
_ODD_EVEN = np.concatenate([np.arange(1, GQA_HEAD_DIM, 2), np.arange(0, GQA_HEAD_DIM, 2)])


def _head_cols(perm, n_heads):
    return np.concatenate([h * GQA_HEAD_DIM + perm for h in range(n_heads)])


def kernel(x, c, ctx, c_ctx, w_mod, b_mod, w_in, na_rpb, gla_gate_w, gla_gate_b, gla_norm_w,
           gqa_q_norm, gqa_k_norm, w_out, ln1_g, ln1_b, peer_wq, peer_subkeys, peer_u, peer_v,
           ln2_g, ln2_b):
    batch, n_lat, d = x.shape
    assert batch == 1 and d == D_MODEL
    n_ctx = ctx.shape[1]
    assert n_lat % NA_QBLK == 0 and n_lat // GRID_W >= NA_BAND + NA_QROWS
    depth = w_mod.shape[0]

    xa = jnp.concatenate([ctx[0], x[0]], axis=0)
    mods = _modulation(jnp.stack([c_ctx, c[0]]), w_mod, b_mod).reshape(depth, 2, 6, d)

    o = np.cumsum([0, NA_W, NA_W, NA_W, GLA_KW, GLA_KW, GLA_VW, GLA_VW, 2 * GLA_GATE_RANK,
                   GQA_QW, GQA_KVW, GQA_KVW])
    w_na = w_in[:, :, o[0]:o[3]].astype(BF16)
    w_gla = w_in[:, :, o[3]:o[8] + (128 - 2 * GLA_GATE_RANK)].astype(BF16)
    wq_c, wk_c, wv_c = w_in[:, :, o[8]:o[9]], w_in[:, :, o[9]:o[10]], w_in[:, :, o[10]:o[11]]
    w_gqa = jnp.concatenate([wq_c[:, :, _head_cols(_EVEN_ODD, GQA_HEADS)],
                             wq_c[:, :, _head_cols(_ODD_EVEN, GQA_HEADS)],
                             wk_c[:, :, _head_cols(_EVEN_ODD, GQA_KV_HEADS)],
                             wk_c[:, :, _head_cols(_ODD_EVEN, GQA_KV_HEADS)],
                             wv_c], -1).astype(BF16)
    wg = jnp.zeros((depth, 2, 128, GLA_KW), F32)
    wg = wg.at[:, 0, 0:GLA_GATE_RANK].set(gla_gate_w[:, 0])
    wg = wg.at[:, 1, GLA_GATE_RANK:2 * GLA_GATE_RANK].set(gla_gate_w[:, 1])
    wg_hi = wg.astype(BF16)
    wg_lo = (wg - wg_hi.astype(F32)).astype(BF16)
    wg = jnp.concatenate([wg_hi, wg_hi, wg_lo], axis=2)
    bg = gla_gate_b.reshape(depth, 2, 1, GLA_KW)
    w_out_b = w_out.astype(BF16)
    wq_t = jnp.swapaxes(peer_wq, 1, 2).astype(BF16)
    sub_keys = peer_subkeys.astype(BF16)
    u_tab = peer_u
    vt_tab = jnp.swapaxes(peer_v, 1, 2).astype(BF16)

    cos2, sin2 = _rope_tables(n_ctx, n_lat)
    q_scale = GQA_HEAD_DIM ** -0.5 * math.log2(math.e)
    na_bias = _na_bias(na_rpb)

    for l in range(depth):
        mod = mods[l]
        p_na, p_gla, p_gqa = _project(xa, mod, (w_na, w_gla, w_gqa), (BF16, F32, F32), n_ctx, l)

        y_na = _neighbourhood_attention(p_na, na_bias, n_ctx, l)
        o_f, o_b = _gla(p_gla, wg[l], bg[l])

        wqn, wkn = gqa_q_norm[l], gqa_k_norm[l]
        ta = jnp.concatenate([jnp.tile(cos2 * wqn[_EVEN_ODD] * q_scale, (1, 2)),
                              jnp.tile(cos2 * wkn[_EVEN_ODD], (1, 2))], -1)
        tb = jnp.concatenate([jnp.tile(sin2 * wqn[_ODD_EVEN] * q_scale, (1, 2)),
                              jnp.tile(sin2 * wkn[_ODD_EVEN], (1, 2))], -1)
        qt, kz, vt = _gqa_prep(p_gqa, ta, tb)
        y_gqa = _gqa_attention(qt, kz, vt, n_ctx)

        x1, h2t = _mix_out(xa, mod, y_na, o_f, o_b, p_gla, y_gqa, w_out_b,
                           gla_norm_w[l].reshape(1, GLA_DV), ln1_g[l].reshape(1, d), ln1_b[l].reshape(1, d),
                           n_ctx, l)
        r2, e2, n1, e1 = _peer_route(h2t, wq_t, sub_keys, l)
        peer_t = _peer_experts(h2t, u_tab, vt_tab, r2, e2, n1, e1, l)
        xa = _peer_out(x1, mod, peer_t, ln2_g[l].reshape(1, d), ln2_b[l].reshape(1, d), n_ctx)

    return xa[n_ctx:][None]
```

```python
import functools
import math

import numpy as np
import jax
import jax.numpy as jnp
from jax import lax
from jax.experimental import pallas as pl
from jax.experimental.pallas import tpu as pltpu

F32 = jnp.float32
BF16 = jnp.bfloat16

D_MODEL = 2048
DEPTH = 4
GRID_W = 64
EPS = 1e-6

NA_HEADS = 8
NA_HEAD_DIM = 64
NA_WIN_H = 8
NA_WIN_W = 16
GLA_HEADS = 4
GLA_DK = 128
GLA_DV = 256
GLA_GATE_RANK = 16
GLA_GATE_NORM = 16.0
GLA_CHUNK = 64
GQA_HEADS = 8
GQA_KV_HEADS = 2
GQA_HEAD_DIM = 64
ROPE_BASE = 10000.0
PEER_HEADS = 8
PEER_NKEYS = 128
PEER_KEY_DIM = 256
PEER_TOPK = 16

NA_W = NA_HEADS * NA_HEAD_DIM
GLA_KW = GLA_HEADS * GLA_DK
GLA_VW = GLA_HEADS * GLA_DV
GQA_QW = GQA_HEADS * GQA_HEAD_DIM
GQA_KVW = GQA_KV_HEADS * GQA_HEAD_DIM
DEEPNORM_ALPHA = (2.0 * DEPTH) ** 0.25

LANES = 128
BF16_SUBLANES = 16
VMEM_LIMIT_BYTES = 58 * 1024 * 1024

NA_QROWS = 4
NA_BAND = NA_QROWS + NA_WIN_H
NA_QBLK = NA_QROWS * GRID_W
NA_KBLK = NA_BAND * GRID_W
NA_STEP_HEADS = 4
GLA_BLK = 256
GQA_KEY_CHUNK = 256
GQA_LOOP_CHUNKS = 8
GQA_ONES_ROWS = 16
PEER_EXPERT_BLK = 1024
PEER_SUB = 512
NEG = -1e30


def _cparams(sem):
    return pltpu.CompilerParams(dimension_semantics=sem, vmem_limit_bytes=VMEM_LIMIT_BYTES)


def _pick(n, cands):
    for c in cands:
        if n % c == 0:
            return c
    raise ValueError(f"no block size in {cands} divides {n}")


def _dot(a, b):
    return jnp.dot(a, b, preferred_element_type=F32)


def _dot_nt(a, b):
    return lax.dot_general(a, b, (((1,), (1,)), ((), ())), preferred_element_type=F32)


def _dot_tn(a, b):
    return lax.dot_general(a, b, (((0,), (0,)), ((), ())), preferred_element_type=F32)


def _layer_norm(x):
    mu = jnp.mean(x, axis=-1, keepdims=True)
    xc = x - mu
    var = jnp.mean(xc * xc, axis=-1, keepdims=True)
    return xc * lax.rsqrt(var + EPS)


def _row_block(tt, n_ctx):
    tm = _pick(tt, (256, 128))
    assert n_ctx % tm == 0
    return tm


def _mod_spec(n_ctx, tm, d):
    return pl.BlockSpec((None, 6, d), lambda i: (jnp.where(i < n_ctx // tm, 0, 1), 0, 0))


def _mod_row(mod_ref, idx):
    return mod_ref[idx:idx + 1, :]


def _mod_kernel(c_ref, w_ref, b_ref, o_ref):
    w = w_ref[0]
    reps = w.shape[1] // LANES
    for m in range(2):
        cb = c_ref[m]
        s = cb / (1.0 + jnp.exp(-cb))
        o_ref[0, m:m + 1, :] = jnp.sum(w * jnp.tile(s, (1, reps)), axis=0, keepdims=True) + b_ref[0]


def _modulation(c2, w_mod, b_mod):
    depth, d, n = w_mod.shape
    tn = _pick(n, (512, 256, 128))
    cb = jnp.broadcast_to(c2[:, :, None], (2, d, LANES))
    return pl.pallas_call(
        _mod_kernel,
        grid=(depth, n // tn),
        in_specs=[pl.BlockSpec((2, d, LANES), lambda l, j: (0, 0, 0)),
                  pl.BlockSpec((1, d, tn), lambda l, j: (l, 0, j)),
                  pl.BlockSpec((1, 1, tn), lambda l, j: (l, 0, j))],
        out_specs=pl.BlockSpec((1, 2, tn), lambda l, j: (l, 0, j)),
        out_shape=jax.ShapeDtypeStruct((depth, 2, n), F32),
        compiler_params=_cparams(("arbitrary", "arbitrary")),
        name="modulation",
    )(cb, w_mod, b_mod.reshape(depth, 1, n))


def _proj_kernel(x_ref, mod_ref, *refs):
    xn = _layer_norm(x_ref[...])
    h = (xn * (1.0 + _mod_row(mod_ref, 1)) + _mod_row(mod_ref, 0)).astype(BF16)
    n_groups = len(refs) // 2
    for w_ref, o_ref in zip(refs[:n_groups], refs[n_groups:]):
        o_ref[...] = _dot(h, w_ref[...]).astype(o_ref.dtype)


def _layer_block(stacked, layer, **kw):
    zeros = (0,) * (stacked.ndim - 1)
    return pl.BlockSpec((None,) + stacked.shape[1:], lambda *_: (layer,) + zeros, **kw)


def _project(xa, mod, weights, out_dtypes, n_ctx, layer):
    tt, d = xa.shape
    tm = _row_block(tt, n_ctx)
    w_specs = [_layer_block(w, layer, pipeline_mode=pl.Buffered(1)) for w in weights]
    return pl.pallas_call(
        _proj_kernel,
        grid=(tt // tm,),
        in_specs=[pl.BlockSpec((tm, d), lambda i: (i, 0)), _mod_spec(n_ctx, tm, d)] + w_specs,
        out_specs=[pl.BlockSpec((tm, w.shape[2]), lambda i: (i, 0)) for w in weights],
        out_shape=[jax.ShapeDtypeStruct((tt, w.shape[2]), dt) for w, dt in zip(weights, out_dtypes)],
        compiler_params=_cparams(("arbitrary",)),
        name="ln_mod_project",
    )(xa, mod, *weights)


def _na_kernel(q_ref, k_ref, v_ref, bias_ref, o_ref, *, n_ctx, n_band_starts):
    qb = pl.program_id(1)
    scale = NA_HEAD_DIM ** -0.5
    lane = lax.broadcasted_iota(jnp.int32, (1, LANES), 1)
    heads_per_tile = LANES // NA_HEAD_DIM

    def attend(pair, band_start):
        cols = slice(pair * LANES, (pair + 1) * LANES)
        q = q_ref[:, cols]
        kc, vc = k_ref[0:n_ctx, cols], v_ref[0:n_ctx, cols]
        if band_start is not None:
            kb, vb = k_ref[pl.ds(band_start, NA_KBLK), cols], v_ref[pl.ds(band_start, NA_KBLK), cols]
        outs = []
        for h in range(heads_per_tile):
            qh = jnp.where((lane // NA_HEAD_DIM) == h, q, jnp.zeros_like(q))
            sc = _dot_nt(qh, kc) * scale
            m = jnp.max(sc, axis=-1, keepdims=True)
            if band_start is not None:
                sw = _dot_nt(qh, kb) * scale + bias_ref[0, pair * heads_per_tile + h]
                m = jnp.maximum(m, jnp.max(sw, axis=-1, keepdims=True))
            pc = jnp.exp(sc - m)
            l = jnp.sum(pc, axis=-1, keepdims=True)
            o = _dot(pc.astype(BF16), vc)
            if band_start is not None:
                pw = jnp.exp(sw - m)
                l = l + jnp.sum(pw, axis=-1, keepdims=True)
                o = o + _dot(pw.astype(BF16), vb)
            outs.append(o / l)
        o_ref[:, cols] = jnp.where((lane // NA_HEAD_DIM) == 0, outs[0], outs[1]).astype(o_ref.dtype)

    @pl.when(qb == 0)
    def _():
        for pair in range(NA_STEP_HEADS // heads_per_tile):
            attend(pair, None)

    @pl.when(qb > 0)
    def _():
        start_blk = jnp.clip(qb - 2, 0, n_band_starts - 1)
        start = pl.multiple_of(n_ctx + start_blk * NA_QBLK, NA_QBLK)
        for pair in range(NA_STEP_HEADS // heads_per_tile):
            attend(pair, start)


def _na_bias(rpb):
    a = np.arange(NA_QROWS)[:, None]
    j = np.arange(NA_BAND)[None, :]
    ws = [np.zeros_like(a), a, np.full_like(a, NA_BAND - NA_WIN_H)]
    off = [0, -NA_WIN_H // 2, -(NA_QROWS + NA_WIN_H // 2)]
    qc = np.arange(GRID_W)[:, None]
    kc = np.arange(GRID_W)[None, :]
    wcs = np.clip(qc - NA_WIN_W // 2, 0, GRID_W - NA_WIN_W)
    col_ok = (kc >= wcs) & (kc < wcs + NA_WIN_W)
    depth, heads = rpb.shape[:2]
    n_dr, n_dc = 2 * NA_WIN_H - 1, 2 * NA_WIN_W - 1
    period = 2 * GRID_W + 1
    t = np.arange(period)
    dc = np.where(t <= GRID_W, t, t - period) + NA_WIN_W - 1
    u = jnp.where((dc >= 0) & (dc < n_dc), rpb[..., np.clip(dc, 0, n_dc - 1)], NEG)
    tiles = jnp.tile(u, (1, 1, 1, GRID_W))[..., :GRID_W * 2 * GRID_W]
    tiles = tiles.reshape(depth, heads, n_dr, GRID_W, 2 * GRID_W)[..., :GRID_W]
    tiles = jnp.where(col_ok, tiles, NEG)
    masked = jnp.full((depth, heads, GRID_W, GRID_W), NEG, F32)
    patterns = []
    for p in range(3):
        row_ok = (j >= ws[p]) & (j < ws[p] + NA_WIN_H)
        dr = j - a + off[p] + NA_WIN_H - 1
        blocks = [jnp.concatenate([tiles[:, :, dr[ai, ji]] if row_ok[ai, ji] else masked
                                   for ji in range(NA_BAND)], axis=-1) for ai in range(NA_QROWS)]
        patterns.append(jnp.stack(blocks, axis=2))
    return jnp.stack(patterns, axis=1).reshape(depth, 3, heads, NA_QBLK, NA_KBLK)


def _neighbourhood_attention(p_na, bias, n_ctx, layer):
    tt = p_na.shape[0]
    n_qb = tt // NA_QBLK
    n_lat_blk = n_qb - 1
    n_band_starts = n_lat_blk - NA_BAND // NA_QROWS + 1
    hp = NA_HEADS // NA_STEP_HEADS
    w = NA_STEP_HEADS * NA_HEAD_DIM

    def bias_idx(h, qb):
        pat = jnp.where(qb <= 1, 0, jnp.where(qb == n_qb - 1, 2, 1))
        return (layer, pat, h, 0, 0)

    return pl.pallas_call(
        functools.partial(_na_kernel, n_ctx=n_ctx, n_band_starts=n_band_starts),
        grid=(hp, n_qb),
        in_specs=[pl.BlockSpec((NA_QBLK, w), lambda h, qb: (qb, h)),
                  pl.BlockSpec((tt, w), lambda h, qb: (0, hp + h)),
                  pl.BlockSpec((tt, w), lambda h, qb: (0, 2 * hp + h)),
                  pl.BlockSpec((None, 1, NA_STEP_HEADS, NA_QBLK, NA_KBLK), bias_idx)],
        out_specs=pl.BlockSpec((NA_QBLK, w), lambda h, qb: (qb, h)),
        out_shape=jax.ShapeDtypeStruct((tt, NA_W), BF16),
        compiler_params=_cparams(("arbitrary", "arbitrary")),
        name="neighbourhood_attention",
    )(p_na, p_na, p_na, bias)


def _log_sigmoid(x):
    return jnp.minimum(x, 0.0) - jnp.log(1.0 + jnp.exp(-jnp.abs(x)))


def _split3(x):
    hi = x.astype(BF16)
    r = x - hi.astype(F32)
    mid = r.astype(BF16)
    lo = (r - mid.astype(F32)).astype(BF16)
    return hi, mid, lo


def _gla_prep(q, k, lr, wg, bg, tri, last_row):
    lh, ll, _ = _split3(lr)
    x = _dot(jnp.concatenate([lh, ll, lh], axis=1), wg) + bg
    la = _log_sigmoid(x) * (1.0 / GLA_GATE_NORM)
    b = _dot(tri, jnp.concatenate(_split3(la), axis=0))
    b_last = b[last_row:last_row + 1, :]
    q_t = (q * (GLA_DK ** -0.5) * jnp.exp(b)).astype(BF16)
    k_t = (k * jnp.exp(-b)).astype(BF16)
    k_end = (k * jnp.exp(b_last - b)).astype(BF16)
    return q_t, k_t, k_end, jnp.exp(b_last)


def _gla_attend(prep, v, causal, s_ref):
    q_t, k_t, k_end, dec = prep
    outs = []
    for h in range(GLA_HEADS):
        ks = slice(h * GLA_DK, (h + 1) * GLA_DK)
        vh = v[:, h * GLA_DV:(h + 1) * GLA_DV].astype(BF16)
        att = jnp.where(causal, _dot_nt(q_t[:, ks], k_t[:, ks]), 0.0)
        st = s_ref[h]
        o = _dot(att.astype(BF16), vh) + _dot_nt(q_t[:, ks], st.astype(BF16))
        s_ref[h] = st * dec[:, ks] + _dot_tn(vh, k_end[:, ks])
        outs.append(o)
    return jnp.concatenate(outs, axis=-1)


def _gla_kernel(qf_ref, kf_ref, vf_ref, lf_ref, qb_ref, kb_ref, vb_ref, lb_ref, wg_ref, bg_ref,
                of_ref, ob_ref, sf_ref, sb_ref):
    @pl.when(pl.program_id(0) == 0)
    def _():
        sf_ref[...] = jnp.zeros_like(sf_ref)
        sb_ref[...] = jnp.zeros_like(sb_ref)

    r = lax.broadcasted_iota(jnp.int32, (GLA_CHUNK, GLA_CHUNK), 0)
    c = lax.broadcasted_iota(jnp.int32, (GLA_CHUNK, GLA_CHUNK), 1)
    lower = r >= c
    upper = r <= c
    tri_f = jnp.tile(lower.astype(F32), (1, 3)).astype(BF16)
    tri_b = jnp.tile(upper.astype(F32), (1, 3)).astype(BF16)
    n_chunks = GLA_BLK // GLA_CHUNK
    rows_of = lambda ci: slice(ci * GLA_CHUNK, (ci + 1) * GLA_CHUNK)

    def prep_f(ci):
        rows = rows_of(ci)
        return _gla_prep(qf_ref[rows, :], kf_ref[rows, :], lf_ref[rows, :], wg_ref[0], bg_ref[0], tri_f,
                         GLA_CHUNK - 1)

    def prep_b(ci):
        rows = rows_of(ci)
        return _gla_prep(qb_ref[rows, :], kb_ref[rows, :], lb_ref[rows, :], wg_ref[1], bg_ref[1], tri_b, 0)

    pf, pb = prep_f(0), prep_b(n_chunks - 1)
    for ci in range(n_chunks):
        cb = n_chunks - 1 - ci
        cur_f, cur_b = pf, pb
        if ci + 1 < n_chunks:
            pf, pb = prep_f(ci + 1), prep_b(cb - 1)
        of_ref[rows_of(ci), :] = _gla_attend(cur_f, vf_ref[rows_of(ci), :], lower, sf_ref)
        ob_ref[rows_of(cb), :] = _gla_attend(cur_b, vb_ref[rows_of(cb), :], upper, sb_ref)


def _gla(p_gla, wg, bg):
    tt = p_gla.shape[0]
    nb = tt // GLA_BLK
    fwd = lambda i: i
    bwd = lambda i: jnp.where(i == 0, 0, nb - i)
    lr_blk = (2 * GLA_KW + 2 * GLA_VW) // 128

    def specs(rowmap):
        return [pl.BlockSpec((GLA_BLK, GLA_KW), lambda i: (rowmap(i), 0)),
                pl.BlockSpec((GLA_BLK, GLA_KW), lambda i: (rowmap(i), 1)),
                pl.BlockSpec((GLA_BLK, GLA_VW), lambda i: (rowmap(i), 1)),
                pl.BlockSpec((GLA_BLK, 128), lambda i: (rowmap(i), lr_blk))]

    return pl.pallas_call(
        _gla_kernel,
        grid=(nb,),
        in_specs=specs(fwd) + specs(bwd) + [
            pl.BlockSpec((2, 3 * 128, GLA_KW), lambda i: (0, 0, 0)),
            pl.BlockSpec((2, 1, GLA_KW), lambda i: (0, 0, 0))],
        out_specs=[pl.BlockSpec((GLA_BLK, GLA_VW), lambda i: (fwd(i), 0)),
                   pl.BlockSpec((GLA_BLK, GLA_VW), lambda i: (bwd(i), 0))],
        out_shape=[jax.ShapeDtypeStruct((tt, GLA_VW), F32)] * 2,
        scratch_shapes=[pltpu.VMEM((GLA_HEADS, GLA_DV, GLA_DK), F32)] * 2,
        compiler_params=_cparams(("arbitrary",)),
        name="gla_bidirectional",
    )(*([p_gla] * 8), wg, bg)


def _group_sum_sq(x, ones_bd):
    sq = x * x
    hi = sq.astype(BF16)
    lo = (sq - hi.astype(F32)).astype(BF16)
    return _dot(hi, ones_bd) + _dot(lo, ones_bd)


def _gqa_prep_kernel(p_ref, ta_ref, tb_ref, ones_ref, qt_ref, kz_ref, vt_ref):
    d = GQA_HEAD_DIM
    x = p_ref[...]
    q, qs = x[:, 0:GQA_QW], x[:, GQA_QW:2 * GQA_QW]
    o = 2 * GQA_QW
    k, ks, v = x[:, o:o + GQA_KVW], x[:, o + GQA_KVW:o + 2 * GQA_KVW], x[:, o + 2 * GQA_KVW:o + 3 * GQA_KVW]
    ta, tb = ta_ref[...], tb_ref[...]
    taq = jnp.tile(ta[:, 0:128], (1, GQA_QW // 128))
    tbq = jnp.tile(tb[:, 0:128], (1, GQA_QW // 128))
    rq = lax.rsqrt(_group_sum_sq(q, ones_ref[...]) * (1.0 / d) + EPS)
    qr = rq * (q * taq + qs * tbq)
    rk = lax.rsqrt(_group_sum_sq(k, ones_ref[0:GQA_KVW, 0:GQA_KVW]) * (1.0 / d) + EPS)
    kr = rk * (k * ta[:, 128:256] + ks * tb[:, 128:256])
    qt_ref[...] = qr.T.astype(BF16)
    lane = lax.broadcasted_iota(jnp.int32, (1, GQA_KVW), 1)
    for g in range(GQA_KV_HEADS):
        kz_ref[g] = jnp.where((lane // d) == g, kr, 0.0).astype(BF16)
    vt_ref[...] = v.T.astype(BF16)


def _gqa_prep(p_gqa, ta, tb):
    tt, n = p_gqa.shape
    tm = _pick(tt, (384, 256, 128))
    hid = np.arange(GQA_QW) // GQA_HEAD_DIM
    ones_bd = jnp.asarray(hid[:, None] == hid[None, :], BF16)
    return pl.pallas_call(
        _gqa_prep_kernel,
        grid=(tt // tm,),
        in_specs=[pl.BlockSpec((tm, n), lambda i: (i, 0)),
                  pl.BlockSpec((tm, 256), lambda i: (i, 0)),
                  pl.BlockSpec((tm, 256), lambda i: (i, 0)),
                  pl.BlockSpec((GQA_QW, GQA_QW), lambda i: (0, 0))],
        out_specs=[pl.BlockSpec((GQA_QW, tm), lambda i: (0, i)),
                   pl.BlockSpec((GQA_KV_HEADS, tm, GQA_KVW), lambda i: (0, i, 0)),
                   pl.BlockSpec((GQA_KVW, tm), lambda i: (0, i))],
        out_shape=[jax.ShapeDtypeStruct((GQA_QW, tt), BF16),
                   jax.ShapeDtypeStruct((GQA_KV_HEADS, tt, GQA_KVW), BF16),
                   jax.ShapeDtypeStruct((GQA_KVW, tt), BF16)],
        compiler_params=_cparams(("arbitrary",)),
        name="gqa_prep",
    )(p_gqa, ta, tb, ones_bd)


def _gqa_kernel(qt_ref, kz_ref, vt_ref, o_ref, s0_ref, s1_ref, acc_ref, *, n_ctx):
    qb = pl.program_id(1)
    d = GQA_HEAD_DIM
    group = GQA_HEADS // GQA_KV_HEADS
    kc = GQA_KEY_CHUNK
    tq = qt_ref.shape[1]
    q4 = jnp.concatenate([jnp.concatenate([qt_ref[g * d:(g + 1) * d, :]] * 2, axis=0) for g in range(group)],
                         axis=1)

    def scores(c, s_ref):
        start = pl.multiple_of(c * kc, kc)
        s = _dot(kz_ref[0, pl.ds(start, kc), :], q4)
        s_ref[...] = s
        return jnp.max(s, axis=0, keepdims=True)

    def accumulate(c, s_ref, m_chunk, m):
        m_new = jnp.maximum(m, m_chunk)
        p = jnp.exp2(s_ref[...] - m_new)
        acc_ref[...] = acc_ref[...] * jnp.exp2(m - m_new) + _dot(vt_ref[0, c], p.astype(BF16))
        return m_new

    def attend(n_chunks):
        acc_ref[...] = jnp.zeros_like(acc_ref)
        m = jnp.full((1, group * tq), -jnp.inf, F32)
        mc0 = scores(0, s0_ref)

        def body(j, carry):
            m, mc0 = carry
            for k in range(0, GQA_LOOP_CHUNKS, 2):
                c = GQA_LOOP_CHUNKS * j + k
                mc1 = scores(c + 1, s1_ref)
                m = accumulate(c, s0_ref, mc0, m)
                mc0 = scores(c + 2, s0_ref)
                m = accumulate(c + 1, s1_ref, mc1, m)
            return m, mc0

        m, mc0 = lax.fori_loop(0, (n_chunks - 1) // GQA_LOOP_CHUNKS, body, (m, mc0))
        accumulate(n_chunks - 1, s0_ref, mc0, m)
        out = acc_ref[0:d, :] / acc_ref[d:d + 1, :]
        o_ref[...] = jnp.concatenate([out[:, g * tq:(g + 1) * tq] for g in range(group)],
                                     axis=0).T.astype(o_ref.dtype)

    @pl.when(qb == 0)
    def _():
        attend(n_ctx // kc)

    @pl.when(qb > 0)
    def _():
        attend(kz_ref.shape[1] // kc)


def _gqa_attention(qt, kz, vt, n_ctx):
    tt = qt.shape[1]
    tq = n_ctx
    kc = GQA_KEY_CHUNK
    group = GQA_HEADS // GQA_KV_HEADS
    gw = group * GQA_HEAD_DIM
    assert tt % kc == 0 and n_ctx % kc == 0
    assert (tt // kc) % GQA_LOOP_CHUNKS == 1 and (n_ctx // kc) % GQA_LOOP_CHUNKS == 1
    vt3 = vt.reshape(GQA_KV_HEADS, GQA_HEAD_DIM, tt // kc, kc).transpose(0, 2, 1, 3)
    extra = jnp.zeros((GQA_KV_HEADS, tt // kc, GQA_ONES_ROWS, kc), BF16).at[:, :, 0, :].set(1.0)
    vt3 = jnp.concatenate([vt3, extra], axis=2)
    vrows = GQA_HEAD_DIM + GQA_ONES_ROWS
    return pl.pallas_call(
        functools.partial(_gqa_kernel, n_ctx=n_ctx),
        grid=(GQA_KV_HEADS, tt // tq),
        in_specs=[pl.BlockSpec((gw, tq), lambda g, i: (g, i)),
                  pl.BlockSpec((1, tt, GQA_KVW), lambda g, i: (g, 0, 0)),
                  pl.BlockSpec((1, tt // kc, vrows, kc), lambda g, i: (g, 0, 0, 0))],
        out_specs=pl.BlockSpec((tq, gw), lambda g, i: (i, g)),
        out_shape=jax.ShapeDtypeStruct((tt, GQA_QW), BF16),
        scratch_shapes=[pltpu.VMEM((kc, group * tq), F32), pltpu.VMEM((kc, group * tq), F32),
                        pltpu.VMEM((vrows, group * tq), F32)],
        compiler_params=_cparams(("arbitrary", "arbitrary")),
        name="gqa_attention",
    )(qt, kz, vt3)


def _mix_out_kernel(x_ref, mod_ref, yna_ref, of_ref, ob_ref, gate_ref, ygqa_ref, w_ref, nw_ref,
                    g_ref, b_ref, x1_ref, h2t_ref):
    o = of_ref[...] + ob_ref[...]
    nw = nw_ref[...]
    parts = []
    for h in range(GLA_HEADS):
        oh = o[:, h * GLA_DV:(h + 1) * GLA_DV]
        parts.append(oh * lax.rsqrt(jnp.mean(oh * oh, axis=-1, keepdims=True) + EPS) * nw)
    gate = gate_ref[...]
    y_gla = (jnp.concatenate(parts, axis=-1) * (gate / (1.0 + jnp.exp(-gate)))).astype(BF16)
    y = (_dot(yna_ref[...], w_ref[0:NA_W, :]) + _dot(y_gla, w_ref[NA_W:NA_W + GLA_VW, :])
         + _dot(ygqa_ref[...], w_ref[NA_W + GLA_VW:, :]))
    x1 = _layer_norm(DEEPNORM_ALPHA * x_ref[...] + _mod_row(mod_ref, 2) * y) * g_ref[...] + b_ref[...]
    x1_ref[...] = x1
    h2 = _layer_norm(x1) * (1.0 + _mod_row(mod_ref, 4)) + _mod_row(mod_ref, 3)
    h2t_ref[...] = h2.T.astype(BF16)


def _mix_out(xa, mod, y_na, o_f, o_b, p_gla, y_gqa, w_out, norm_w, ln_g, ln_b, n_ctx, layer):
    tt, d = xa.shape
    tm = _row_block(tt, n_ctx)
    row = lambda i: (i, 0)
    const2 = lambda i: (0, 0)
    return pl.pallas_call(
        _mix_out_kernel,
        grid=(tt // tm,),
        in_specs=[pl.BlockSpec((tm, d), row),
                  _mod_spec(n_ctx, tm, d),
                  pl.BlockSpec((tm, NA_W), row),
                  pl.BlockSpec((tm, GLA_VW), row),
                  pl.BlockSpec((tm, GLA_VW), row),
                  pl.BlockSpec((tm, GLA_VW), lambda i: (i, 2)),
                  pl.BlockSpec((tm, GQA_QW), row),
                  _layer_block(w_out, layer),
                  pl.BlockSpec((1, GLA_DV), const2),
                  pl.BlockSpec((1, d), const2),
                  pl.BlockSpec((1, d), const2)],
        out_specs=[pl.BlockSpec((tm, d), row), pl.BlockSpec((d, tm), lambda i: (0, i))],
        out_shape=[jax.ShapeDtypeStruct((tt, d), F32), jax.ShapeDtypeStruct((d, tt), BF16)],
        compiler_params=_cparams(("arbitrary",)),
        name="mixer_out_postln",
    )(xa, mod, y_na, o_f, o_b, p_gla, y_gqa, w_out, norm_w, ln_g, ln_b)


def _top16_exact(s, key_iota):
    cur = s
    rank = jnp.full(s.shape, float(PEER_TOPK), F32)
    vals = []
    for r in range(PEER_TOPK):
        m = jnp.max(cur, axis=0, keepdims=True)
        idx = jnp.min(jnp.where(cur == m, key_iota, float(PEER_NKEYS)), axis=0, keepdims=True)
        sel = key_iota == idx
        rank = jnp.where(sel, float(r), rank)
        cur = jnp.where(sel, -jnp.inf, cur)
        vals.append(m)
    return jnp.concatenate(vals, axis=0), rank


_CODE_SCALE = 2.0 ** 100
_CODE_STEP = 64.0


def _rank_code(r):
    return -_CODE_SCALE * (1.0 + (r + 1) / _CODE_STEP)


def _top16_fast(s):
    cur = s
    vals = []
    for r in range(PEER_TOPK):
        m = jnp.max(cur, axis=0, keepdims=True)
        cur = jnp.where(cur == m, _rank_code(r), cur)
        vals.append(m)
    return jnp.concatenate(vals, axis=0), cur


def _coded(cur):
    return cur <= _rank_code(0)


def _decode_rank(cur):
    rank = (cur * (-1.0 / _CODE_SCALE) - 1.0) * _CODE_STEP - 1.0
    return jnp.where(_coded(cur), rank, float(PEER_TOPK))


def _cand_layout():
    k = PEER_TOPK
    rows = [(0, rb) for rb in range(k)]
    for ra in range(1, 5):
        rows += [(ra, rb) for rb in range(8)]
    rows += [(ra, 0) for ra in range(8, k)]
    rows += [(ra, 1) for ra in range(8)]
    rows += [(ra, 0) for ra in range(8)]
    seen, valid = set(), []
    for pair in rows:
        valid.append(pair not in seen)
        seen.add(pair)
    needed = {(ra, rb) for ra in range(k) for rb in range(k) if (ra + 1) * (rb + 1) <= k}
    assert needed <= seen
    ra = np.array([p[0] for p in rows], np.float32)
    rb = np.array([p[1] for p in rows], np.float32)
    return ra, rb, np.array(valid)


def _select_pairs(a, b, c_ra, c_flat, c_ok, exact):
    k = PEER_TOPK
    tt = a.shape[1]
    blocks = [a[0:1, :] + b]
    for ra in range(1, 5):
        blocks.append(a[ra:ra + 1, :] + b[0:8, :])
    blocks.append(a[8:k, :] + b[0:1, :])
    blocks.append(a[0:8, :] + b[1:2, :])
    blocks.append(a[0:8, :] + b[0:1, :])
    cand = jnp.where(c_ok, jnp.concatenate(blocks, axis=0), -jnp.inf)
    m0 = a[0:1, :] + b[0:1, :]
    z = jnp.zeros((1, tt), F32)
    if exact:
        rank_iota = lax.broadcasted_iota(jnp.int32, (k, tt), 0).astype(F32)
        cnt = jnp.zeros((k, tt), F32)
        for _ in range(k):
            m = jnp.max(cand, axis=0, keepdims=True)
            idx = jnp.min(jnp.where(cand == m, c_flat, 1e9), axis=0, keepdims=True)
            sel = c_flat == idx
            ra_sel = jnp.sum(jnp.where(sel, c_ra, 0.0), axis=0, keepdims=True)
            cnt = cnt + jnp.where(rank_iota == ra_sel, 1.0, 0.0)
            z = z + jnp.exp(m - m0)
            cand = jnp.where(sel, -jnp.inf, cand)
        return cnt, z, None
    for _ in range(k):
        m = jnp.max(cand, axis=0, keepdims=True)
        z = z + jnp.exp(m - m0)
        cand = jnp.where(cand == m, -jnp.inf, cand)
    sel = jnp.where(jnp.logical_and(c_ok, cand == -jnp.inf), 1.0, 0.0)
    low = sel[56:64, :] + sel[64:72, :]
    row8 = lax.broadcasted_iota(jnp.int32, (8, tt), 0)
    sums = [jnp.sum(sel[0:16, :], axis=0, keepdims=True)]
    sums += [jnp.sum(sel[16 + 8 * i:24 + 8 * i, :], axis=0, keepdims=True) for i in range(4)]
    for r, v in enumerate(sums):
        low = jnp.where(row8 == r, v, low)
    cnt = jnp.concatenate([low, sel[48:56, :]], axis=0)
    return cnt, z, jnp.sum(sel, axis=0, keepdims=True)


def _route_head(s1, s2, key_iota, c_ra, c_flat, c_ok, exact):
    k = PEER_TOPK
    if exact:
        a, r1 = _top16_exact(s1, key_iota)
        b, r2 = _top16_exact(s2, key_iota)
    else:
        a, cur1 = _top16_fast(s1)
        b, cur2 = _top16_fast(s2)
        r2 = _decode_rank(cur2)
    cnt, z, n_sel = _select_pairs(a, b, c_ra, c_flat, c_ok, exact)
    n1 = jnp.zeros(s1.shape, F32)
    for r in range(k):
        hit = (r1 == float(r)) if exact else (s1 == a[r:r + 1, :])
        n1 = jnp.where(hit, cnt[r:r + 1, :], n1)
    outs = (r2, jnp.exp(s2 - b[0:1, :]) / z, n1, jnp.exp(s1 - a[0:1, :]))
    if exact:
        return outs, None
    n_coded = lambda cur: jnp.sum(jnp.where(_coded(cur), 1.0, 0.0), axis=0, keepdims=True)
    clean = jnp.logical_and(n_sel == float(k),
                            jnp.logical_and(n_coded(cur1) == float(k), n_coded(cur2) == float(k)))
    return outs, jnp.where(clean, 0.0, 1.0)


def _route_kernel(h_ref, wq_ref, sk_ref, cst_ref, r2_ref, e2_ref, n1_ref, e1_ref):
    tt = h_ref.shape[1]
    key_iota = lax.broadcasted_iota(jnp.int32, (PEER_NKEYS, tt), 0).astype(F32)
    cst = cst_ref[...]
    n_rows = cst.shape[0]
    c_ra = jnp.broadcast_to(cst[:, 0:1], (n_rows, tt))
    c_flat = jnp.broadcast_to(cst[:, 1:2], (n_rows, tt))
    c_ok = jnp.broadcast_to(cst[:, 2:3], (n_rows, tt)) > 0.5
    half = PEER_KEY_DIM // 2

    def head(h, qt_h, exact):
        s1 = _dot(sk_ref[h, 0], qt_h[0:half, :].astype(BF16))
        s2 = _dot(sk_ref[h, 1], qt_h[half:2 * half, :].astype(BF16))
        outs, bad = _route_head(s1, s2, key_iota, c_ra, c_flat, c_ok, exact)
        for ref, val in zip((r2_ref, e2_ref, n1_ref, e1_ref), outs):
            ref[h] = val.astype(ref.dtype)
        return bad

    qt = _dot(wq_ref[...], h_ref[...])
    tied = [jnp.max(head(h, qt[h * PEER_KEY_DIM:(h + 1) * PEER_KEY_DIM, :], False)) for h in range(PEER_HEADS)]
    for h in range(PEER_HEADS):
        @pl.when(tied[h] > 0.0)
        def _(h=h):
            head(h, _dot(wq_ref[h * PEER_KEY_DIM:(h + 1) * PEER_KEY_DIM, :], h_ref[...]), True)


def _peer_route(h2t, wq_t, sub_keys, layer):
    d, tt = h2t.shape
    tb = _pick(tt, (256, 128))
    ra, rb, valid = _cand_layout()
    cst = np.zeros((ra.shape[0], LANES), np.float32)
    flat = np.where(valid, ra * PEER_TOPK + rb, 1000.0 + np.arange(ra.shape[0]))
    cst[:, 0], cst[:, 1], cst[:, 2] = ra, flat, valid
    shp = (PEER_HEADS, PEER_NKEYS, tt)
    out_spec = pl.BlockSpec((PEER_HEADS, PEER_NKEYS, tb), lambda i: (0, 0, i))
    return pl.pallas_call(
        _route_kernel,
        grid=(tt // tb,),
        in_specs=[pl.BlockSpec((d, tb), lambda i: (0, i)),
                  _layer_block(wq_t, layer),
                  _layer_block(sub_keys, layer),
                  pl.BlockSpec(cst.shape, lambda i: (0, 0))],
        out_specs=[out_spec] * 4,
        out_shape=[jax.ShapeDtypeStruct(shp, dt) for dt in (BF16, BF16, F32, F32)],
        compiler_params=_cparams(("arbitrary",)),
        name="peer_route",
    )(h2t, wq_t, sub_keys, jnp.asarray(cst))


def _gelu_tanh(x):
    c = math.sqrt(2.0 / math.pi)
    hx = 0.5 * x
    return hx * jnp.tanh(x * (c + (c * 0.044715) * (x * x))) + hx


def _peer_kernel(h_ref, u_ref, vt_ref, r2_ref, e2_ref, n1_ref, e1_ref, o_ref):
    e = pl.program_id(1)
    eb = u_ref.shape[0]
    n_i = eb // PEER_NKEYS

    @pl.when(e == 0)
    def _():
        o_ref[...] = jnp.zeros_like(o_ref)

    n_sub = eb // PEER_SUB
    sub_rows = lambda s: slice(s * PEER_SUB, (s + 1) * PEER_SUB)
    pre = _dot(u_ref[sub_rows(0), :].astype(BF16), h_ref[...])
    for s in range(n_sub):
        rows = sub_rows(s)
        cur = pre
        if s + 1 < n_sub:
            pre = _dot(u_ref[sub_rows(s + 1), :].astype(BF16), h_ref[...])
        act = _gelu_tanh(cur)
        parts = []
        for ii in range(PEER_SUB // PEER_NKEYS):
            i = e * n_i + s * (PEER_SUB // PEER_NKEYS) + ii
            tb = act.shape[1]
            tiles = PEER_NKEYS // BF16_SUBLANES
            g = None
            for h in range(PEER_HEADS):
                n1 = jnp.broadcast_to(n1_ref[h, pl.ds(i, 1), :], (BF16_SUBLANES, tb)).astype(BF16)
                e1 = jnp.broadcast_to(e1_ref[h, pl.ds(i, 1), :], (BF16_SUBLANES, tb)).astype(BF16)
                r2t = r2_ref[h].reshape(tiles, BF16_SUBLANES, tb)
                e2t = e2_ref[h].reshape(tiles, BF16_SUBLANES, tb)
                t = jnp.where(r2t < n1[None], e2t, jnp.zeros_like(e2t)) * e1[None]
                g = t if g is None else g + t
            a16 = act[ii * PEER_NKEYS:(ii + 1) * PEER_NKEYS, :].astype(BF16).reshape(tiles, BF16_SUBLANES, tb)
            parts.append((a16 * g).reshape(PEER_NKEYS, tb))
        o_ref[...] += _dot(vt_ref[:, rows], jnp.concatenate(parts, axis=0))


def _peer_experts(h2t, u_tab, vt_tab, r2, e2, n1, e1, layer):
    d, tt = h2t.shape
    n_exp = u_tab.shape[1]
    tb = _pick(tt, (768, 512, 256))
    eb = PEER_EXPERT_BLK
    route_spec = pl.BlockSpec((PEER_HEADS, PEER_NKEYS, tb), lambda i, e: (0, 0, i),
                              pipeline_mode=pl.Buffered(1))
    return pl.pallas_call(
        _peer_kernel,
        grid=(tt // tb, n_exp // eb),
        in_specs=[pl.BlockSpec((d, tb), lambda i, e: (0, i), pipeline_mode=pl.Buffered(1)),
                  pl.BlockSpec((None, eb, d), lambda i, e: (layer, e, 0)),
                  pl.BlockSpec((None, d, eb), lambda i, e: (layer, 0, e)),
                  route_spec, route_spec, route_spec, route_spec],
        out_specs=pl.BlockSpec((d, tb), lambda i, e: (0, i)),
        out_shape=jax.ShapeDtypeStruct((d, tt), F32),
        compiler_params=_cparams(("arbitrary", "arbitrary")),
        name="peer_experts",
    )(h2t, u_tab, vt_tab, r2, e2, n1, e1)


def _peer_out_kernel(x_ref, mod_ref, pt_ref, g_ref, b_ref, o_ref):
    y = pt_ref[...].T
    o_ref[...] = (_layer_norm(DEEPNORM_ALPHA * x_ref[...] + _mod_row(mod_ref, 5) * y)
                  * g_ref[...] + b_ref[...])


def _peer_out(x1, mod, peer_t, ln_g, ln_b, n_ctx, latent_only):
    tt, d = x1.shape
    tm = _row_block(tt, n_ctx)
    skip = n_ctx // tm if latent_only else 0
    return pl.pallas_call(
        _peer_out_kernel,
        grid=(tt // tm,),
        in_specs=[pl.BlockSpec((tm, d), lambda i: (i, 0)),
                  _mod_spec(n_ctx, tm, d),
                  pl.BlockSpec((d, tm), lambda i: (0, i)),
                  pl.BlockSpec((1, d), lambda i: (0, 0)),
                  pl.BlockSpec((1, d), lambda i: (0, 0))],
        out_specs=pl.BlockSpec((tm, d), lambda i: (jnp.maximum(i - skip, 0), 0)),
        out_shape=jax.ShapeDtypeStruct((tt - skip * tm, d), F32),
        compiler_params=_cparams(("arbitrary",)),
        name="peer_out_postln",
    )(x1, mod, peer_t, ln_g, ln_b)


def _rope_tables(n_ctx, n_lat):
    half = GQA_HEAD_DIM // 2
    inv = ROPE_BASE ** (-jnp.arange(0, half, 2, dtype=F32) / half)
    t = jnp.arange(n_lat)
    row = (t // GRID_W).astype(F32)
    col = (t % GRID_W).astype(F32)
    ang = jnp.concatenate([row[:, None] * inv, col[:, None] * inv], axis=-1)
    cos, sin = jnp.cos(ang), jnp.sin(ang)
    cos2 = jnp.concatenate([jnp.ones((n_ctx, 2 * half), F32), jnp.concatenate([cos, cos], -1)], 0)
    sin2 = jnp.concatenate([jnp.zeros((n_ctx, 2 * half), F32), jnp.concatenate([-sin, sin], -1)], 0)
    return cos2, sin2


_EVEN_ODD = np.concatenate([np.arange(0, GQA_HEAD_DIM, 2), np.arange(1, GQA_HEAD_DIM, 2)])
_ODD_EVEN = np.concatenate([np.arange(1, GQA_HEAD_DIM, 2), np.arange(0, GQA_HEAD_DIM, 2)])


def _head_cols(perm, n_heads):
    return np.concatenate([h * GQA_HEAD_DIM + perm for h in range(n_heads)])


def kernel(x, c, ctx, c_ctx, w_mod, b_mod, w_in, na_rpb, gla_gate_w, gla_gate_b, gla_norm_w,
           gqa_q_norm, gqa_k_norm, w_out, ln1_g, ln1_b, peer_wq, peer_subkeys, peer_u, peer_v,
           ln2_g, ln2_b):
    batch, n_lat, d = x.shape
    assert batch == 1 and d == D_MODEL
    n_ctx = ctx.shape[1]
    assert n_lat % NA_QBLK == 0 and n_lat // GRID_W >= NA_BAND + NA_QROWS
    depth = w_mod.shape[0]

    xa = jnp.concatenate([ctx[0], x[0]], axis=0)
    mods = _modulation(jnp.stack([c_ctx, c[0]]), w_mod, b_mod).reshape(depth, 2, 6, d)

    o = np.cumsum([0, NA_W, NA_W, NA_W, GLA_KW, GLA_KW, GLA_VW, GLA_VW, 2 * GLA_GATE_RANK,
                   GQA_QW, GQA_KVW, GQA_KVW])
    w_na = w_in[:, :, o[0]:o[3]].astype(BF16)
    w_gla = w_in[:, :, o[3]:o[8] + (128 - 2 * GLA_GATE_RANK)].astype(BF16)
    wq_c, wk_c, wv_c = w_in[:, :, o[8]:o[9]], w_in[:, :, o[9]:o[10]], w_in[:, :, o[10]:o[11]]
    w_gqa = jnp.concatenate([wq_c[:, :, _head_cols(_EVEN_ODD, GQA_HEADS)],
                             wq_c[:, :, _head_cols(_ODD_EVEN, GQA_HEADS)],
                             wk_c[:, :, _head_cols(_EVEN_ODD, GQA_KV_HEADS)],
                             wk_c[:, :, _head_cols(_ODD_EVEN, GQA_KV_HEADS)],
                             wv_c], -1).astype(BF16)
    wg = jnp.zeros((depth, 2, 128, GLA_KW), F32)
    wg = wg.at[:, 0, 0:GLA_GATE_RANK].set(gla_gate_w[:, 0])
    wg = wg.at[:, 1, GLA_GATE_RANK:2 * GLA_GATE_RANK].set(gla_gate_w[:, 1])
    wg_hi = wg.astype(BF16)
    wg_lo = (wg - wg_hi.astype(F32)).astype(BF16)
    wg = jnp.concatenate([wg_hi, wg_hi, wg_lo], axis=2)
    bg = gla_gate_b.reshape(depth, 2, 1, GLA_KW)
    w_out_b = w_out.astype(BF16)
    wq_t = jnp.swapaxes(peer_wq, 1, 2).astype(BF16)
    sub_keys = peer_subkeys.astype(BF16)
    u_tab = peer_u
    vt_tab = jnp.swapaxes(peer_v, 1, 2).astype(BF16)

    cos2, sin2 = _rope_tables(n_ctx, n_lat)
    q_scale = GQA_HEAD_DIM ** -0.5 * math.log2(math.e)
    na_bias = _na_bias(na_rpb)

    for l in range(depth):
        mod = mods[l]
        p_na, p_gla, p_gqa = _project(xa, mod, (w_na, w_gla, w_gqa), (BF16, F32, F32), n_ctx, l)

        y_na = _neighbourhood_attention(p_na, na_bias, n_ctx, l)
        o_f, o_b = _gla(p_gla, wg[l], bg[l])

        wqn, wkn = gqa_q_norm[l], gqa_k_norm[l]
        ta = jnp.concatenate([jnp.tile(cos2 * wqn[_EVEN_ODD] * q_scale, (1, 2)),
                              jnp.tile(cos2 * wkn[_EVEN_ODD], (1, 2))], -1)
        tb = jnp.concatenate([jnp.tile(sin2 * wqn[_ODD_EVEN] * q_scale, (1, 2)),
                              jnp.tile(sin2 * wkn[_ODD_EVEN], (1, 2))], -1)
        qt, kz, vt = _gqa_prep(p_gqa, ta, tb)
        y_gqa = _gqa_attention(qt, kz, vt, n_ctx)

        x1, h2t = _mix_out(xa, mod, y_na, o_f, o_b, p_gla, y_gqa, w_out_b,
                           gla_norm_w[l].reshape(1, GLA_DV), ln1_g[l].reshape(1, d), ln1_b[l].reshape(1, d),
                           n_ctx, l)
        r2, e2, n1, e1 = _peer_route(h2t, wq_t, sub_keys, l)
        peer_t = _peer_experts(h2t, u_tab, vt_tab, r2, e2, n1, e1, l)
        xa = _peer_out(x1, mod, peer_t, ln2_g[l].reshape(1, d), ln2_b[l].reshape(1, d), n_ctx,
                       latent_only=(l == depth - 1))

    return xa[None]
```

```python
import functools
import math

import numpy as np
import jax
import jax.numpy as jnp
from jax import lax
from jax.experimental import pallas as pl
from jax.experimental.pallas import tpu as pltpu

F32 = jnp.float32
BF16 = jnp.bfloat16

D_MODEL = 2048
DEPTH = 4
GRID_W = 64
EPS = 1e-6

NA_HEADS = 8
NA_HEAD_DIM = 64
NA_WIN_H = 8
NA_WIN_W = 16
GLA_HEADS = 4
GLA_DK = 128
GLA_DV = 256
GLA_GATE_RANK = 16
GLA_GATE_NORM = 16.0
GLA_CHUNK = 64
GQA_HEADS = 8
GQA_KV_HEADS = 2
GQA_HEAD_DIM = 64
ROPE_BASE = 10000.0
PEER_HEADS = 8
PEER_NKEYS = 128
PEER_KEY_DIM = 256
PEER_TOPK = 16

NA_W = NA_HEADS * NA_HEAD_DIM
GLA_KW = GLA_HEADS * GLA_DK
GLA_VW = GLA_HEADS * GLA_DV
GQA_QW = GQA_HEADS * GQA_HEAD_DIM
GQA_KVW = GQA_KV_HEADS * GQA_HEAD_DIM
DEEPNORM_ALPHA = (2.0 * DEPTH) ** 0.25

LANES = 128
BF16_SUBLANES = 16
VMEM_LIMIT_BYTES = 58 * 1024 * 1024

NA_QROWS = 4
NA_BAND = NA_QROWS + NA_WIN_H
NA_QBLK = NA_QROWS * GRID_W
NA_KBLK = NA_BAND * GRID_W
NA_STEP_HEADS = 4
GLA_BLK = 256
GQA_KEY_CHUNK = 256
GQA_LOOP_CHUNKS = 16
GQA_ONES_ROWS = 16
PEER_EXPERT_BLK = 1024
PEER_SUB = 512
NEG = -1e30


def _cparams(sem):
    return pltpu.CompilerParams(dimension_semantics=sem, vmem_limit_bytes=VMEM_LIMIT_BYTES)


def _pick(n, cands):
    for c in cands:
        if n % c == 0:
            return c
    raise ValueError(f"no block size in {cands} divides {n}")


def _dot(a, b):
    return jnp.dot(a, b, preferred_element_type=F32)


def _dot_nt(a, b):
    return lax.dot_general(a, b, (((1,), (1,)), ((), ())), preferred_element_type=F32)


def _dot_tn(a, b):
    return lax.dot_general(a, b, (((0,), (0,)), ((), ())), preferred_element_type=F32)


def _layer_norm(x):
    mu = jnp.mean(x, axis=-1, keepdims=True)
    xc = x - mu
    var = jnp.mean(xc * xc, axis=-1, keepdims=True)
    return xc * lax.rsqrt(var + EPS)


def _row_block(tt, n_ctx):
    tm = _pick(tt, (256, 128))
    assert n_ctx % tm == 0
    return tm


def _mod_spec(n_ctx, tm, d):
    return pl.BlockSpec((None, 6, d), lambda i: (jnp.where(i < n_ctx // tm, 0, 1), 0, 0))


def _mod_row(mod_ref, idx):
    return mod_ref[idx:idx + 1, :]


def _mod_kernel(c_ref, w_ref, b_ref, o_ref, s_ref):
    @pl.when(jnp.logical_and(pl.program_id(0) == 0, pl.program_id(1) == 0))
    def _():
        cb = c_ref[...]
        s_ref[...] = cb / (1.0 + jnp.exp(-cb))

    w = w_ref[0]
    reps = w.shape[1] // LANES
    for m in range(2):
        o_ref[0, m:m + 1, :] = jnp.sum(w * jnp.tile(s_ref[m], (1, reps)), axis=0, keepdims=True) + b_ref[0]


def _modulation(c2, w_mod, b_mod):
    depth, d, n = w_mod.shape
    tn = _pick(n, (512, 256, 128))
    cb = jnp.broadcast_to(c2[:, :, None], (2, d, LANES))
    return pl.pallas_call(
        _mod_kernel,
        grid=(depth, n // tn),
        in_specs=[pl.BlockSpec((2, d, LANES), lambda l, j: (0, 0, 0)),
                  pl.BlockSpec((1, d, tn), lambda l, j: (l, 0, j)),
                  pl.BlockSpec((1, 1, tn), lambda l, j: (l, 0, j))],
        out_specs=pl.BlockSpec((1, 2, tn), lambda l, j: (l, 0, j)),
        out_shape=jax.ShapeDtypeStruct((depth, 2, n), F32),
        scratch_shapes=[pltpu.VMEM((2, d, LANES), F32)],
        compiler_params=_cparams(("arbitrary", "arbitrary")),
        name="modulation",
    )(cb, w_mod, b_mod.reshape(depth, 1, n))


def _proj_kernel(x_ref, mod_ref, *refs):
    xn = _layer_norm(x_ref[...])
    h = (xn * (1.0 + _mod_row(mod_ref, 1)) + _mod_row(mod_ref, 0)).astype(BF16)
    n_groups = len(refs) // 2
    for w_ref, o_ref in zip(refs[:n_groups], refs[n_groups:]):
        o_ref[...] = _dot(h, w_ref[...]).astype(o_ref.dtype)


def _layer_block(stacked, layer, **kw):
    zeros = (0,) * (stacked.ndim - 1)
    return pl.BlockSpec((None,) + stacked.shape[1:], lambda *_: (layer,) + zeros, **kw)


def _project(xa, mod, weights, out_dtypes, n_ctx, layer):
    tt, d = xa.shape
    tm = _row_block(tt, n_ctx)
    w_specs = [_layer_block(w, layer, pipeline_mode=pl.Buffered(1)) for w in weights]
    return pl.pallas_call(
        _proj_kernel,
        grid=(tt // tm,),
        in_specs=[pl.BlockSpec((tm, d), lambda i: (i, 0)), _mod_spec(n_ctx, tm, d)] + w_specs,
        out_specs=[pl.BlockSpec((tm, w.shape[2]), lambda i: (i, 0)) for w in weights],
        out_shape=[jax.ShapeDtypeStruct((tt, w.shape[2]), dt) for w, dt in zip(weights, out_dtypes)],
        compiler_params=_cparams(("arbitrary",)),
        name="ln_mod_project",
    )(xa, mod, *weights)


def _na_kernel(q_ref, k_ref, v_ref, bias_ref, o_ref, *, n_ctx, n_band_starts):
    qb = pl.program_id(1)
    scale = NA_HEAD_DIM ** -0.5
    lane = lax.broadcasted_iota(jnp.int32, (1, LANES), 1)
    heads_per_tile = LANES // NA_HEAD_DIM

    def attend(pair, band_start):
        cols = slice(pair * LANES, (pair + 1) * LANES)
        q = q_ref[:, cols]
        kc, vc = k_ref[0:n_ctx, cols], v_ref[0:n_ctx, cols]
        if band_start is not None:
            kb, vb = k_ref[pl.ds(band_start, NA_KBLK), cols], v_ref[pl.ds(band_start, NA_KBLK), cols]
        outs = []
        for h in range(heads_per_tile):
            qh = jnp.where((lane // NA_HEAD_DIM) == h, q, jnp.zeros_like(q))
            sc = _dot_nt(qh, kc) * scale
            m = jnp.max(sc, axis=-1, keepdims=True)
            if band_start is not None:
                sw = _dot_nt(qh, kb) * scale + bias_ref[0, pair * heads_per_tile + h]
                m = jnp.maximum(m, jnp.max(sw, axis=-1, keepdims=True))
            pc = jnp.exp(sc - m)
            l = jnp.sum(pc, axis=-1, keepdims=True)
            o = _dot(pc.astype(BF16), vc)
            if band_start is not None:
                pw = jnp.exp(sw - m)
                l = l + jnp.sum(pw, axis=-1, keepdims=True)
                o = o + _dot(pw.astype(BF16), vb)
            outs.append(o / l)
        o_ref[:, cols] = jnp.where((lane // NA_HEAD_DIM) == 0, outs[0], outs[1]).astype(o_ref.dtype)

    @pl.when(qb == 0)
    def _():
        for pair in range(NA_STEP_HEADS // heads_per_tile):
            attend(pair, None)

    @pl.when(qb > 0)
    def _():
        start_blk = jnp.clip(qb - 2, 0, n_band_starts - 1)
        start = pl.multiple_of(n_ctx + start_blk * NA_QBLK, NA_QBLK)
        for pair in range(NA_STEP_HEADS // heads_per_tile):
            attend(pair, start)


def _na_bias(rpb):
    a = np.arange(NA_QROWS)[:, None]
    j = np.arange(NA_BAND)[None, :]
    ws = [np.zeros_like(a), a, np.full_like(a, NA_BAND - NA_WIN_H)]
    off = [0, -NA_WIN_H // 2, -(NA_QROWS + NA_WIN_H // 2)]
    qc = np.arange(GRID_W)[:, None]
    kc = np.arange(GRID_W)[None, :]
    wcs = np.clip(qc - NA_WIN_W // 2, 0, GRID_W - NA_WIN_W)
    col_ok = (kc >= wcs) & (kc < wcs + NA_WIN_W)
    depth, heads = rpb.shape[:2]
    n_dr, n_dc = 2 * NA_WIN_H - 1, 2 * NA_WIN_W - 1
    period = 2 * GRID_W + 1
    t = np.arange(period)
    dc = np.where(t <= GRID_W, t, t - period) + NA_WIN_W - 1
    u = jnp.where((dc >= 0) & (dc < n_dc), rpb[..., np.clip(dc, 0, n_dc - 1)], NEG)
    tiles = jnp.tile(u, (1, 1, 1, GRID_W))[..., :GRID_W * 2 * GRID_W]
    tiles = tiles.reshape(depth, heads, n_dr, GRID_W, 2 * GRID_W)[..., :GRID_W]
    tiles = jnp.where(col_ok, tiles, NEG)
    masked = jnp.full((depth, heads, GRID_W, GRID_W), NEG, F32)
    patterns = []
    for p in range(3):
        row_ok = (j >= ws[p]) & (j < ws[p] + NA_WIN_H)
        dr = j - a + off[p] + NA_WIN_H - 1
        blocks = [jnp.concatenate([tiles[:, :, dr[ai, ji]] if row_ok[ai, ji] else masked
                                   for ji in range(NA_BAND)], axis=-1) for ai in range(NA_QROWS)]
        patterns.append(jnp.stack(blocks, axis=2))
    return jnp.stack(patterns, axis=1).reshape(depth, 3, heads, NA_QBLK, NA_KBLK)


def _neighbourhood_attention(p_na, bias, n_ctx, layer):
    tt = p_na.shape[0]
    n_qb = tt // NA_QBLK
    n_lat_blk = n_qb - 1
    n_band_starts = n_lat_blk - NA_BAND // NA_QROWS + 1
    hp = NA_HEADS // NA_STEP_HEADS
    w = NA_STEP_HEADS * NA_HEAD_DIM

    def bias_idx(h, qb):
        pat = jnp.where(qb <= 1, 0, jnp.where(qb == n_qb - 1, 2, 1))
        return (layer, pat, h, 0, 0)

    return pl.pallas_call(
        functools.partial(_na_kernel, n_ctx=n_ctx, n_band_starts=n_band_starts),
        grid=(hp, n_qb),
        in_specs=[pl.BlockSpec((NA_QBLK, w), lambda h, qb: (qb, h)),
                  pl.BlockSpec((tt, w), lambda h, qb: (0, hp + h)),
                  pl.BlockSpec((tt, w), lambda h, qb: (0, 2 * hp + h)),
                  pl.BlockSpec((None, 1, NA_STEP_HEADS, NA_QBLK, NA_KBLK), bias_idx)],
        out_specs=pl.BlockSpec((NA_QBLK, w), lambda h, qb: (qb, h)),
        out_shape=jax.ShapeDtypeStruct((tt, NA_W), BF16),
        compiler_params=_cparams(("arbitrary", "arbitrary")),
        name="neighbourhood_attention",
    )(p_na, p_na, p_na, bias)


def _log_sigmoid(x):
    return jnp.minimum(x, 0.0) - jnp.log(1.0 + jnp.exp(-jnp.abs(x)))


def _split3(x):
    hi = x.astype(BF16)
    r = x - hi.astype(F32)
    mid = r.astype(BF16)
    lo = (r - mid.astype(F32)).astype(BF16)
    return hi, mid, lo


def _gla_prep(q, k, lr, wg, bg, tri, last_row):
    lh, ll, _ = _split3(lr)
    x = _dot(jnp.concatenate([lh, ll, lh], axis=1), wg) + bg
    la = _log_sigmoid(x) * (1.0 / GLA_GATE_NORM)
    b = _dot(tri, jnp.concatenate(_split3(la), axis=0))
    b_last = b[last_row:last_row + 1, :]
    q_t = (q * (GLA_DK ** -0.5) * jnp.exp(b)).astype(BF16)
    k_t = (k * jnp.exp(-b)).astype(BF16)
    k_end = (k * jnp.exp(b_last - b)).astype(BF16)
    return q_t, k_t, k_end, jnp.exp(b_last)


def _gla_attend(prep, v, causal, s_ref):
    q_t, k_t, k_end, dec = prep
    outs = []
    for h in range(GLA_HEADS):
        ks = slice(h * GLA_DK, (h + 1) * GLA_DK)
        vh = v[:, h * GLA_DV:(h + 1) * GLA_DV].astype(BF16)
        att = jnp.where(causal, _dot_nt(q_t[:, ks], k_t[:, ks]), 0.0)
        st = s_ref[h]
        o = _dot(att.astype(BF16), vh) + _dot_nt(q_t[:, ks], st.astype(BF16))
        s_ref[h] = st * dec[:, ks] + _dot_tn(vh, k_end[:, ks])
        outs.append(o)
    return jnp.concatenate(outs, axis=-1)


def _gla_kernel(qf_ref, kf_ref, vf_ref, lf_ref, qb_ref, kb_ref, vb_ref, lb_ref, wg_ref, bg_ref,
                of_ref, ob_ref, sf_ref, sb_ref):
    @pl.when(pl.program_id(0) == 0)
    def _():
        sf_ref[...] = jnp.zeros_like(sf_ref)
        sb_ref[...] = jnp.zeros_like(sb_ref)

    r = lax.broadcasted_iota(jnp.int32, (GLA_CHUNK, GLA_CHUNK), 0)
    c = lax.broadcasted_iota(jnp.int32, (GLA_CHUNK, GLA_CHUNK), 1)
    lower = r >= c
    upper = r <= c
    tri_f = jnp.tile(lower.astype(F32), (1, 3)).astype(BF16)
    tri_b = jnp.tile(upper.astype(F32), (1, 3)).astype(BF16)
    n_chunks = GLA_BLK // GLA_CHUNK
    rows_of = lambda ci: slice(ci * GLA_CHUNK, (ci + 1) * GLA_CHUNK)

    def prep_f(ci):
        rows = rows_of(ci)
        return _gla_prep(qf_ref[rows, :], kf_ref[rows, :], lf_ref[rows, :], wg_ref[0], bg_ref[0], tri_f,
                         GLA_CHUNK - 1)

    def prep_b(ci):
        rows = rows_of(ci)
        return _gla_prep(qb_ref[rows, :], kb_ref[rows, :], lb_ref[rows, :], wg_ref[1], bg_ref[1], tri_b, 0)

    pf, pb = prep_f(0), prep_b(n_chunks - 1)
    for ci in range(n_chunks):
        cb = n_chunks - 1 - ci
        cur_f, cur_b = pf, pb
        if ci + 1 < n_chunks:
            pf, pb = prep_f(ci + 1), prep_b(cb - 1)
        of_ref[rows_of(ci), :] = _gla_attend(cur_f, vf_ref[rows_of(ci), :], lower, sf_ref)
        ob_ref[rows_of(cb), :] = _gla_attend(cur_b, vb_ref[rows_of(cb), :], upper, sb_ref)


def _gla(p_gla, wg, bg):
    tt = p_gla.shape[0]
    nb = tt // GLA_BLK
    fwd = lambda i: i
    bwd = lambda i: jnp.where(i == 0, 0, nb - i)
    lr_blk = (2 * GLA_KW + 2 * GLA_VW) // 128

    def specs(rowmap):
        return [pl.BlockSpec((GLA_BLK, GLA_KW), lambda i: (rowmap(i), 0)),
                pl.BlockSpec((GLA_BLK, GLA_KW), lambda i: (rowmap(i), 1)),
                pl.BlockSpec((GLA_BLK, GLA_VW), lambda i: (rowmap(i), 1)),
                pl.BlockSpec((GLA_BLK, 128), lambda i: (rowmap(i), lr_blk))]

    return pl.pallas_call(
        _gla_kernel,
        grid=(nb,),
        in_specs=specs(fwd) + specs(bwd) + [
            pl.BlockSpec((2, 3 * 128, GLA_KW), lambda i: (0, 0, 0)),
            pl.BlockSpec((2, 1, GLA_KW), lambda i: (0, 0, 0))],
        out_specs=[pl.BlockSpec((GLA_BLK, GLA_VW), lambda i: (fwd(i), 0)),
                   pl.BlockSpec((GLA_BLK, GLA_VW), lambda i: (bwd(i), 0))],
        out_shape=[jax.ShapeDtypeStruct((tt, GLA_VW), F32)] * 2,
        scratch_shapes=[pltpu.VMEM((GLA_HEADS, GLA_DV, GLA_DK), F32)] * 2,
        compiler_params=_cparams(("arbitrary",)),
        name="gla_bidirectional",
    )(*([p_gla] * 8), wg, bg)


def _group_sum_sq(x, ones_bd):
    sq = x * x
    hi = sq.astype(BF16)
    lo = (sq - hi.astype(F32)).astype(BF16)
    return _dot(hi, ones_bd) + _dot(lo, ones_bd)


def _gqa_prep_kernel(p_ref, ta_ref, tb_ref, ones_ref, qt_ref, kz_ref, vt_ref):
    d = GQA_HEAD_DIM
    x = p_ref[...]
    q, qs = x[:, 0:GQA_QW], x[:, GQA_QW:2 * GQA_QW]
    o = 2 * GQA_QW
    k, ks, v = x[:, o:o + GQA_KVW], x[:, o + GQA_KVW:o + 2 * GQA_KVW], x[:, o + 2 * GQA_KVW:o + 3 * GQA_KVW]
    ta, tb = ta_ref[...], tb_ref[...]
    taq = jnp.tile(ta[:, 0:128], (1, GQA_QW // 128))
    tbq = jnp.tile(tb[:, 0:128], (1, GQA_QW // 128))
    rq = lax.rsqrt(_group_sum_sq(q, ones_ref[...]) * (1.0 / d) + EPS)
    qr = rq * (q * taq + qs * tbq)
    rk = lax.rsqrt(_group_sum_sq(k, ones_ref[0:GQA_KVW, 0:GQA_KVW]) * (1.0 / d) + EPS)
    kr = rk * (k * ta[:, 128:256] + ks * tb[:, 128:256])
    qt_ref[...] = qr.T.astype(BF16)
    lane = lax.broadcasted_iota(jnp.int32, (1, GQA_KVW), 1)
    for g in range(GQA_KV_HEADS):
        kz_ref[g] = jnp.where((lane // d) == g, kr, 0.0).astype(BF16)
    vt_ref[...] = v.T.astype(BF16)


def _gqa_prep(p_gqa, ta, tb):
    tt, n = p_gqa.shape
    tm = _pick(tt, (384, 256, 128))
    hid = np.arange(GQA_QW) // GQA_HEAD_DIM
    ones_bd = jnp.asarray(hid[:, None] == hid[None, :], BF16)
    return pl.pallas_call(
        _gqa_prep_kernel,
        grid=(tt // tm,),
        in_specs=[pl.BlockSpec((tm, n), lambda i: (i, 0)),
                  pl.BlockSpec((tm, 256), lambda i: (i, 0)),
                  pl.BlockSpec((tm, 256), lambda i: (i, 0)),
                  pl.BlockSpec((GQA_QW, GQA_QW), lambda i: (0, 0))],
        out_specs=[pl.BlockSpec((GQA_QW, tm), lambda i: (0, i)),
                   pl.BlockSpec((GQA_KV_HEADS, tm, GQA_KVW), lambda i: (0, i, 0)),
                   pl.BlockSpec((GQA_KVW, tm), lambda i: (0, i))],
        out_shape=[jax.ShapeDtypeStruct((GQA_QW, tt), BF16),
                   jax.ShapeDtypeStruct((GQA_KV_HEADS, tt, GQA_KVW), BF16),
                   jax.ShapeDtypeStruct((GQA_KVW, tt), BF16)],
        compiler_params=_cparams(("arbitrary",)),
        name="gqa_prep",
    )(p_gqa, ta, tb, ones_bd)


def _gqa_kernel(qt_ref, kz_ref, vt_ref, o_ref, s0_ref, s1_ref, acc_ref, *, n_ctx):
    qb = pl.program_id(1)
    d = GQA_HEAD_DIM
    group = GQA_HEADS // GQA_KV_HEADS
    kc = GQA_KEY_CHUNK
    tq = qt_ref.shape[1]
    q4 = jnp.concatenate([jnp.concatenate([qt_ref[g * d:(g + 1) * d, :]] * 2, axis=0) for g in range(group)],
                         axis=1)

    def scores(c, s_ref):
        start = pl.multiple_of(c * kc, kc)
        s = _dot(kz_ref[0, pl.ds(start, kc), :], q4)
        s_ref[...] = s
        return jnp.max(s, axis=0, keepdims=True)

    def accumulate(c, s_ref, m_chunk, m):
        m_new = jnp.maximum(m, m_chunk)
        p = jnp.exp2(s_ref[...] - m_new)
        acc_ref[...] = acc_ref[...] * jnp.exp2(m - m_new) + _dot(vt_ref[0, c], p.astype(BF16))
        return m_new

    def attend(n_chunks):
        acc_ref[...] = jnp.zeros_like(acc_ref)
        m = jnp.full((1, group * tq), -jnp.inf, F32)
        mc0 = scores(0, s0_ref)

        def body(j, carry):
            m, mc0 = carry
            for k in range(0, GQA_LOOP_CHUNKS, 2):
                c = GQA_LOOP_CHUNKS * j + k
                mc1 = scores(c + 1, s1_ref)
                m = accumulate(c, s0_ref, mc0, m)
                mc0 = scores(c + 2, s0_ref)
                m = accumulate(c + 1, s1_ref, mc1, m)
            return m, mc0

        m, mc0 = lax.fori_loop(0, (n_chunks - 1) // GQA_LOOP_CHUNKS, body, (m, mc0))
        accumulate(n_chunks - 1, s0_ref, mc0, m)
        out = acc_ref[0:d, :] / acc_ref[d:d + 1, :]
        o_ref[...] = jnp.concatenate([out[:, g * tq:(g + 1) * tq] for g in range(group)],
                                     axis=0).T.astype(o_ref.dtype)

    @pl.when(qb == 0)
    def _():
        attend(n_ctx // kc)

    @pl.when(qb > 0)
    def _():
        attend(kz_ref.shape[1] // kc)


def _gqa_attention(qt, kz, vt, n_ctx):
    tt = qt.shape[1]
    tq = n_ctx
    kc = GQA_KEY_CHUNK
    group = GQA_HEADS // GQA_KV_HEADS
    gw = group * GQA_HEAD_DIM
    assert tt % kc == 0 and n_ctx % kc == 0
    assert (tt // kc) % GQA_LOOP_CHUNKS == 1 and (n_ctx // kc) % GQA_LOOP_CHUNKS == 1
    vt3 = vt.reshape(GQA_KV_HEADS, GQA_HEAD_DIM, tt // kc, kc).transpose(0, 2, 1, 3)
    extra = jnp.zeros((GQA_KV_HEADS, tt // kc, GQA_ONES_ROWS, kc), BF16).at[:, :, 0, :].set(1.0)
    vt3 = jnp.concatenate([vt3, extra], axis=2)
    vrows = GQA_HEAD_DIM + GQA_ONES_ROWS
    return pl.pallas_call(
        functools.partial(_gqa_kernel, n_ctx=n_ctx),
        grid=(GQA_KV_HEADS, tt // tq),
        in_specs=[pl.BlockSpec((gw, tq), lambda g, i: (g, i)),
                  pl.BlockSpec((1, tt, GQA_KVW), lambda g, i: (g, 0, 0)),
                  pl.BlockSpec((1, tt // kc, vrows, kc), lambda g, i: (g, 0, 0, 0))],
        out_specs=pl.BlockSpec((tq, gw), lambda g, i: (i, g)),
        out_shape=jax.ShapeDtypeStruct((tt, GQA_QW), BF16),
        scratch_shapes=[pltpu.VMEM((kc, group * tq), F32), pltpu.VMEM((kc, group * tq), F32),
                        pltpu.VMEM((vrows, group * tq), F32)],
        compiler_params=_cparams(("arbitrary", "arbitrary")),
        name="gqa_attention",
    )(qt, kz, vt3)


def _mix_out_kernel(x_ref, mod_ref, yna_ref, of_ref, ob_ref, gate_ref, ygqa_ref, w_ref, nw_ref,
                    g_ref, b_ref, x1_ref, h2t_ref):
    o = of_ref[...] + ob_ref[...]
    nw = nw_ref[...]
    parts = []
    for h in range(GLA_HEADS):
        oh = o[:, h * GLA_DV:(h + 1) * GLA_DV]
        parts.append(oh * lax.rsqrt(jnp.mean(oh * oh, axis=-1, keepdims=True) + EPS) * nw)
    gate = gate_ref[...]
    y_gla = (jnp.concatenate(parts, axis=-1) * (gate / (1.0 + jnp.exp(-gate)))).astype(BF16)
    y = (_dot(yna_ref[...], w_ref[0:NA_W, :]) + _dot(y_gla, w_ref[NA_W:NA_W + GLA_VW, :])
         + _dot(ygqa_ref[...], w_ref[NA_W + GLA_VW:, :]))
    x1 = _layer_norm(DEEPNORM_ALPHA * x_ref[...] + _mod_row(mod_ref, 2) * y) * g_ref[...] + b_ref[...]
    x1_ref[...] = x1
    h2 = _layer_norm(x1) * (1.0 + _mod_row(mod_ref, 4)) + _mod_row(mod_ref, 3)
    h2t_ref[...] = h2.T.astype(BF16)


def _mix_out(xa, mod, y_na, o_f, o_b, p_gla, y_gqa, w_out, norm_w, ln_g, ln_b, n_ctx, layer):
    tt, d = xa.shape
    tm = _row_block(tt, n_ctx)
    row = lambda i: (i, 0)
    const2 = lambda i: (0, 0)
    return pl.pallas_call(
        _mix_out_kernel,
        grid=(tt // tm,),
        in_specs=[pl.BlockSpec((tm, d), row),
                  _mod_spec(n_ctx, tm, d),
                  pl.BlockSpec((tm, NA_W), row),
                  pl.BlockSpec((tm, GLA_VW), row),
                  pl.BlockSpec((tm, GLA_VW), row),
                  pl.BlockSpec((tm, GLA_VW), lambda i: (i, 2)),
                  pl.BlockSpec((tm, GQA_QW), row),
                  _layer_block(w_out, layer),
                  pl.BlockSpec((1, GLA_DV), const2),
                  pl.BlockSpec((1, d), const2),
                  pl.BlockSpec((1, d), const2)],
        out_specs=[pl.BlockSpec((tm, d), row), pl.BlockSpec((d, tm), lambda i: (0, i))],
        out_shape=[jax.ShapeDtypeStruct((tt, d), F32), jax.ShapeDtypeStruct((d, tt), BF16)],
        compiler_params=_cparams(("arbitrary",)),
        name="mixer_out_postln",
    )(xa, mod, y_na, o_f, o_b, p_gla, y_gqa, w_out, norm_w, ln_g, ln_b)


def _top16_exact(s, key_iota):
    cur = s
    rank = jnp.full(s.shape, float(PEER_TOPK), F32)
    vals = []
    for r in range(PEER_TOPK):
        m = jnp.max(cur, axis=0, keepdims=True)
        idx = jnp.min(jnp.where(cur == m, key_iota, float(PEER_NKEYS)), axis=0, keepdims=True)
        sel = key_iota == idx
        rank = jnp.where(sel, float(r), rank)
        cur = jnp.where(sel, -jnp.inf, cur)
        vals.append(m)
    return jnp.concatenate(vals, axis=0), rank


_CODE_SCALE = 2.0 ** 100
_CODE_STEP = 64.0


def _rank_code(r):
    return -_CODE_SCALE * (1.0 + (r + 1) / _CODE_STEP)


def _top16_fast(s):
    cur = s
    vals = []
    for r in range(PEER_TOPK):
        m = jnp.max(cur, axis=0, keepdims=True)
        cur = jnp.where(cur == m, _rank_code(r), cur)
        vals.append(m)
    return jnp.concatenate(vals, axis=0), cur


def _coded(cur):
    return cur <= _rank_code(0)


def _decode_rank(cur):
    rank = (cur * (-1.0 / _CODE_SCALE) - 1.0) * _CODE_STEP - 1.0
    return jnp.where(_coded(cur), rank, float(PEER_TOPK))


def _cand_layout():
    k = PEER_TOPK
    rows = [(0, rb) for rb in range(k)]
    for ra in range(1, 5):
        rows += [(ra, rb) for rb in range(8)]
    rows += [(ra, 0) for ra in range(8, k)]
    rows += [(ra, 1) for ra in range(8)]
    rows += [(ra, 0) for ra in range(8)]
    seen, valid = set(), []
    for pair in rows:
        valid.append(pair not in seen)
        seen.add(pair)
    needed = {(ra, rb) for ra in range(k) for rb in range(k) if (ra + 1) * (rb + 1) <= k}
    assert needed <= seen
    ra = np.array([p[0] for p in rows], np.float32)
    rb = np.array([p[1] for p in rows], np.float32)
    return ra, rb, np.array(valid)


def _select_pairs(a, b, c_ra, c_flat, c_ok, exact):
    k = PEER_TOPK
    tt = a.shape[1]
    blocks = [a[0:1, :] + b]
    for ra in range(1, 5):
        blocks.append(a[ra:ra + 1, :] + b[0:8, :])
    blocks.append(a[8:k, :] + b[0:1, :])
    blocks.append(a[0:8, :] + b[1:2, :])
    blocks.append(a[0:8, :] + b[0:1, :])
    cand = jnp.where(c_ok, jnp.concatenate(blocks, axis=0), -jnp.inf)
    m0 = a[0:1, :] + b[0:1, :]
    z = jnp.zeros((1, tt), F32)
    if exact:
        rank_iota = lax.broadcasted_iota(jnp.int32, (k, tt), 0).astype(F32)
        cnt = jnp.zeros((k, tt), F32)
        for _ in range(k):
            m = jnp.max(cand, axis=0, keepdims=True)
            idx = jnp.min(jnp.where(cand == m, c_flat, 1e9), axis=0, keepdims=True)
            sel = c_flat == idx
            ra_sel = jnp.sum(jnp.where(sel, c_ra, 0.0), axis=0, keepdims=True)
            cnt = cnt + jnp.where(rank_iota == ra_sel, 1.0, 0.0)
            z = z + jnp.exp(m - m0)
            cand = jnp.where(sel, -jnp.inf, cand)
        return cnt, z, None
    for _ in range(k):
        m = jnp.max(cand, axis=0, keepdims=True)
        z = z + jnp.exp(m - m0)
        cand = jnp.where(cand == m, -jnp.inf, cand)
    sel = jnp.where(jnp.logical_and(c_ok, cand == -jnp.inf), 1.0, 0.0)
    low = sel[56:64, :] + sel[64:72, :]
    row8 = lax.broadcasted_iota(jnp.int32, (8, tt), 0)
    sums = [jnp.sum(sel[0:16, :], axis=0, keepdims=True)]
    sums += [jnp.sum(sel[16 + 8 * i:24 + 8 * i, :], axis=0, keepdims=True) for i in range(4)]
    for r, v in enumerate(sums):
        low = jnp.where(row8 == r, v, low)
    cnt = jnp.concatenate([low, sel[48:56, :]], axis=0)
    return cnt, z, jnp.sum(sel, axis=0, keepdims=True)


def _route_head(s1, s2, key_iota, c_ra, c_flat, c_ok, exact):
    k = PEER_TOPK
    if exact:
        a, r1 = _top16_exact(s1, key_iota)
        b, r2 = _top16_exact(s2, key_iota)
    else:
        a, cur1 = _top16_fast(s1)
        b, cur2 = _top16_fast(s2)
        r2 = _decode_rank(cur2)
    cnt, z, n_sel = _select_pairs(a, b, c_ra, c_flat, c_ok, exact)
    n1 = jnp.zeros(s1.shape, F32)
    for r in range(k):
        hit = (r1 == float(r)) if exact else (s1 == a[r:r + 1, :])
        n1 = jnp.where(hit, cnt[r:r + 1, :], n1)
    outs = (r2, jnp.exp(s2 - b[0:1, :]) / z, n1, jnp.exp(s1 - a[0:1, :]))
    if exact:
        return outs, None
    n_coded = lambda cur: jnp.sum(jnp.where(_coded(cur), 1.0, 0.0), axis=0, keepdims=True)
    clean = jnp.logical_and(n_sel == float(k),
                            jnp.logical_and(n_coded(cur1) == float(k), n_coded(cur2) == float(k)))
    return outs, jnp.where(clean, 0.0, 1.0)


def _route_kernel(h_ref, wq_ref, sk_ref, cst_ref, r2_ref, e2_ref, n1_ref, e1_ref):
    tt = h_ref.shape[1]
    key_iota = lax.broadcasted_iota(jnp.int32, (PEER_NKEYS, tt), 0).astype(F32)
    cst = cst_ref[...]
    n_rows = cst.shape[0]
    c_ra = jnp.broadcast_to(cst[:, 0:1], (n_rows, tt))
    c_flat = jnp.broadcast_to(cst[:, 1:2], (n_rows, tt))
    c_ok = jnp.broadcast_to(cst[:, 2:3], (n_rows, tt)) > 0.5
    half = PEER_KEY_DIM // 2

    def head(h, qt_h, exact):
        s1 = _dot(sk_ref[h, 0], qt_h[0:half, :].astype(BF16))
        s2 = _dot(sk_ref[h, 1], qt_h[half:2 * half, :].astype(BF16))
        outs, bad = _route_head(s1, s2, key_iota, c_ra, c_flat, c_ok, exact)
        for ref, val in zip((r2_ref, e2_ref, n1_ref, e1_ref), outs):
            ref[h] = val.astype(ref.dtype)
        return bad

    qt = _dot(wq_ref[...], h_ref[...])
    tied = [jnp.max(head(h, qt[h * PEER_KEY_DIM:(h + 1) * PEER_KEY_DIM, :], False)) for h in range(PEER_HEADS)]
    for h in range(PEER_HEADS):
        @pl.when(tied[h] > 0.0)
        def _(h=h):
            head(h, _dot(wq_ref[h * PEER_KEY_DIM:(h + 1) * PEER_KEY_DIM, :], h_ref[...]), True)


def _peer_route(h2t, wq_t, sub_keys, layer):
    d, tt = h2t.shape
    tb = _pick(tt, (256, 128))
    ra, rb, valid = _cand_layout()
    cst = np.zeros((ra.shape[0], LANES), np.float32)
    flat = np.where(valid, ra * PEER_TOPK + rb, 1000.0 + np.arange(ra.shape[0]))
    cst[:, 0], cst[:, 1], cst[:, 2] = ra, flat, valid
    shp = (PEER_HEADS, PEER_NKEYS, tt)
    out_spec = pl.BlockSpec((PEER_HEADS, PEER_NKEYS, tb), lambda i: (0, 0, i))
    return pl.pallas_call(
        _route_kernel,
        grid=(tt // tb,),
        in_specs=[pl.BlockSpec((d, tb), lambda i: (0, i)),
                  _layer_block(wq_t, layer),
                  _layer_block(sub_keys, layer),
                  pl.BlockSpec(cst.shape, lambda i: (0, 0))],
        out_specs=[out_spec] * 4,
        out_shape=[jax.ShapeDtypeStruct(shp, dt) for dt in (BF16, BF16, F32, F32)],
        compiler_params=_cparams(("arbitrary",)),
        name="peer_route",
    )(h2t, wq_t, sub_keys, jnp.asarray(cst))


def _gelu_tanh(x):
    c = math.sqrt(2.0 / math.pi)
    hx = 0.5 * x
    return hx * jnp.tanh(x * (c + (c * 0.044715) * (x * x))) + hx


def _peer_kernel(h_ref, u_ref, vt_ref, r2_ref, e2_ref, n1_ref, e1_ref, o_ref):
    e = pl.program_id(1)
    eb = u_ref.shape[0]
    n_i = eb // PEER_NKEYS

    @pl.when(e == 0)
    def _():
        o_ref[...] = jnp.zeros_like(o_ref)

    n_sub = eb // PEER_SUB
    sub_rows = lambda s: slice(s * PEER_SUB, (s + 1) * PEER_SUB)
    pre = _dot(u_ref[sub_rows(0), :].astype(BF16), h_ref[...])
    for s in range(n_sub):
        rows = sub_rows(s)
        cur = pre
        if s + 1 < n_sub:
            pre = _dot(u_ref[sub_rows(s + 1), :].astype(BF16), h_ref[...])
        act = _gelu_tanh(cur)
        parts = []
        for ii in range(PEER_SUB // PEER_NKEYS):
            i = e * n_i + s * (PEER_SUB // PEER_NKEYS) + ii
            tb = act.shape[1]
            tiles = PEER_NKEYS // BF16_SUBLANES
            g = None
            for h in range(PEER_HEADS):
                n1 = jnp.broadcast_to(n1_ref[h, pl.ds(i, 1), :], (BF16_SUBLANES, tb)).astype(BF16)
                e1 = jnp.broadcast_to(e1_ref[h, pl.ds(i, 1), :], (BF16_SUBLANES, tb)).astype(BF16)
                r2t = r2_ref[h].reshape(tiles, BF16_SUBLANES, tb)
                e2t = e2_ref[h].reshape(tiles, BF16_SUBLANES, tb)
                t = jnp.where(r2t < n1[None], e2t, jnp.zeros_like(e2t)) * e1[None]
                g = t if g is None else g + t
            a16 = act[ii * PEER_NKEYS:(ii + 1) * PEER_NKEYS, :].astype(BF16).reshape(tiles, BF16_SUBLANES, tb)
            parts.append((a16 * g).reshape(PEER_NKEYS, tb))
        o_ref[...] += _dot(vt_ref[:, rows], jnp.concatenate(parts, axis=0))


def _peer_experts(h2t, u_tab, vt_tab, r2, e2, n1, e1, layer):
    d, tt = h2t.shape
    n_exp = u_tab.shape[1]
    tb = _pick(tt, (768, 512, 256))
    eb = PEER_EXPERT_BLK
    route_spec = pl.BlockSpec((PEER_HEADS, PEER_NKEYS, tb), lambda i, e: (0, 0, i),
                              pipeline_mode=pl.Buffered(1))
    return pl.pallas_call(
        _peer_kernel,
        grid=(tt // tb, n_exp // eb),
        in_specs=[pl.BlockSpec((d, tb), lambda i, e: (0, i), pipeline_mode=pl.Buffered(1)),
                  pl.BlockSpec((None, eb, d), lambda i, e: (layer, e, 0)),
                  pl.BlockSpec((None, d, eb), lambda i, e: (layer, 0, e)),
                  route_spec, route_spec, route_spec, route_spec],
        out_specs=pl.BlockSpec((d, tb), lambda i, e: (0, i)),
        out_shape=jax.ShapeDtypeStruct((d, tt), F32),
        compiler_params=_cparams(("arbitrary", "arbitrary")),
        name="peer_experts",
    )(h2t, u_tab, vt_tab, r2, e2, n1, e1)


def _peer_out_kernel(x_ref, mod_ref, pt_ref, g_ref, b_ref, o_ref):
    y = pt_ref[...].T
    o_ref[...] = (_layer_norm(DEEPNORM_ALPHA * x_ref[...] + _mod_row(mod_ref, 5) * y)
                  * g_ref[...] + b_ref[...])


def _peer_out(x1, mod, peer_t, ln_g, ln_b, n_ctx, latent_only):
    tt, d = x1.shape
    tm = _row_block(tt, n_ctx)
    skip = n_ctx // tm if latent_only else 0
    return pl.pallas_call(
        _peer_out_kernel,
        grid=(tt // tm,),
        in_specs=[pl.BlockSpec((tm, d), lambda i: (i, 0)),
                  _mod_spec(n_ctx, tm, d),
                  pl.BlockSpec((d, tm), lambda i: (0, i)),
                  pl.BlockSpec((1, d), lambda i: (0, 0)),
                  pl.BlockSpec((1, d), lambda i: (0, 0))],
        out_specs=pl.BlockSpec((tm, d), lambda i: (jnp.maximum(i - skip, 0), 0)),
        out_shape=jax.ShapeDtypeStruct((tt - skip * tm, d), F32),
        compiler_params=_cparams(("arbitrary",)),
        name="peer_out_postln",
    )(x1, mod, peer_t, ln_g, ln_b)


def _rope_tables(n_ctx, n_lat):
    half = GQA_HEAD_DIM // 2
    inv = ROPE_BASE ** (-jnp.arange(0, half, 2, dtype=F32) / half)
    t = jnp.arange(n_lat)
    row = (t // GRID_W).astype(F32)
    col = (t % GRID_W).astype(F32)
    ang = jnp.concatenate([row[:, None] * inv, col[:, None] * inv], axis=-1)
    cos, sin = jnp.cos(ang), jnp.sin(ang)
    cos2 = jnp.concatenate([jnp.ones((n_ctx, 2 * half), F32), jnp.concatenate([cos, cos], -1)], 0)
    sin2 = jnp.concatenate([jnp.zeros((n_ctx, 2 * half), F32), jnp.concatenate([-sin, sin], -1)], 0)
    return cos2, sin2


_EVEN_ODD = np.concatenate([np.arange(0, GQA_HEAD_DIM, 2), np.arange(1, GQA_HEAD_DIM, 2)])
_ODD_EVEN = np.concatenate([np.arange(1, GQA_HEAD_DIM, 2), np.arange(0, GQA_HEAD_DIM, 2)])


def _head_cols(perm, n_heads):
    return np.concatenate([h * GQA_HEAD_DIM + perm for h in range(n_heads)])


def kernel(x, c, ctx, c_ctx, w_mod, b_mod, w_in, na_rpb, gla_gate_w, gla_gate_b, gla_norm_w,
           gqa_q_norm, gqa_k_norm, w_out, ln1_g, ln1_b, peer_wq, peer_subkeys, peer_u, peer_v,
           ln2_g, ln2_b):
    batch, n_lat, d = x.shape
    assert batch == 1 and d == D_MODEL
    n_ctx = ctx.shape[1]
    assert n_lat % NA_QBLK == 0 and n_lat // GRID_W >= NA_BAND + NA_QROWS
    depth = w_mod.shape[0]

    xa = jnp.concatenate([ctx[0], x[0]], axis=0)
    mods = _modulation(jnp.stack([c_ctx, c[0]]), w_mod, b_mod).reshape(depth, 2, 6, d)

    o = np.cumsum([0, NA_W, NA_W, NA_W, GLA_KW, GLA_KW, GLA_VW, GLA_VW, 2 * GLA_GATE_RANK,
                   GQA_QW, GQA_KVW, GQA_KVW])
    w_na = w_in[:, :, o[0]:o[3]].astype(BF16)
    w_gla = w_in[:, :, o[3]:o[8] + (128 - 2 * GLA_GATE_RANK)].astype(BF16)
    wq_c, wk_c, wv_c = w_in[:, :, o[8]:o[9]], w_in[:, :, o[9]:o[10]], w_in[:, :, o[10]:o[11]]
    w_gqa = jnp.concatenate([wq_c[:, :, _head_cols(_EVEN_ODD, GQA_HEADS)],
                             wq_c[:, :, _head_cols(_ODD_EVEN, GQA_HEADS)],
                             wk_c[:, :, _head_cols(_EVEN_ODD, GQA_KV_HEADS)],
                             wk_c[:, :, _head_cols(_ODD_EVEN, GQA_KV_HEADS)],
                             wv_c], -1).astype(BF16)
    wg = jnp.zeros((depth, 2, 128, GLA_KW), F32)
    wg = wg.at[:, 0, 0:GLA_GATE_RANK].set(gla_gate_w[:, 0])
    wg = wg.at[:, 1, GLA_GATE_RANK:2 * GLA_GATE_RANK].set(gla_gate_w[:, 1])
    wg_hi = wg.astype(BF16)
    wg_lo = (wg - wg_hi.astype(F32)).astype(BF16)
    wg = jnp.concatenate([wg_hi, wg_hi, wg_lo], axis=2)
    bg = gla_gate_b.reshape(depth, 2, 1, GLA_KW)
    w_out_b = w_out.astype(BF16)
    wq_t = jnp.swapaxes(peer_wq, 1, 2).astype(BF16)
    sub_keys = peer_subkeys.astype(BF16)
    u_tab = peer_u
    vt_tab = jnp.swapaxes(peer_v, 1, 2).astype(BF16)

    cos2, sin2 = _rope_tables(n_ctx, n_lat)
    q_scale = GQA_HEAD_DIM ** -0.5 * math.log2(math.e)
    na_bias = _na_bias(na_rpb)

    for l in range(depth):
        mod = mods[l]
        p_na, p_gla, p_gqa = _project(xa, mod, (w_na, w_gla, w_gqa), (BF16, F32, F32), n_ctx, l)

        y_na = _neighbourhood_attention(p_na, na_bias, n_ctx, l)
        o_f, o_b = _gla(p_gla, wg[l], bg[l])

        wqn, wkn = gqa_q_norm[l], gqa_k_norm[l]
        ta = jnp.concatenate([jnp.tile(cos2 * wqn[_EVEN_ODD] * q_scale, (1, 2)),
                              jnp.tile(cos2 * wkn[_EVEN_ODD], (1, 2))], -1)
        tb = jnp.concatenate([jnp.tile(sin2 * wqn[_ODD_EVEN] * q_scale, (1, 2)),
                              jnp.tile(sin2 * wkn[_ODD_EVEN], (1, 2))], -1)
        qt, kz, vt = _gqa_prep(p_gqa, ta, tb)
        y_gqa = _gqa_attention(qt, kz, vt, n_ctx)

        x1, h2t = _mix_out(xa, mod, y_na, o_f, o_b, p_gla, y_gqa, w_out_b,
                           gla_norm_w[l].reshape(1, GLA_DV), ln1_g[l].reshape(1, d), ln1_b[l].reshape(1, d),
                           n_ctx, l)
        r2, e2, n1, e1 = _peer_route(h2t, wq_t, sub_keys, l)
        peer_t = _peer_experts(h2t, u_tab, vt_tab, r2, e2, n1, e1, l)
        xa = _peer_out(x1, mod, peer_t, ln2_g[l].reshape(1, d), ln2_b[l].reshape(1, d), n_ctx,
                       latent_only=(l == depth - 1))

    return xa[None]
```

```python
import functools
import math

import numpy as np
import jax
import jax.numpy as jnp
from jax import lax
from jax.experimental import pallas as pl
from jax.experimental.pallas import tpu as pltpu

F32 = jnp.float32
BF16 = jnp.bfloat16

D_MODEL = 2048
DEPTH = 4
GRID_W = 64
EPS = 1e-6

NA_HEADS = 8
NA_HEAD_DIM = 64
NA_WIN_H = 8
NA_WIN_W = 16
GLA_HEADS = 4
GLA_DK = 128
GLA_DV = 256
GLA_GATE_RANK = 16
GLA_GATE_NORM = 16.0
GLA_CHUNK = 64
GQA_HEADS = 8
GQA_KV_HEADS = 2
GQA_HEAD_DIM = 64
ROPE_BASE = 10000.0
PEER_HEADS = 8
PEER_NKEYS = 128
PEER_KEY_DIM = 256
PEER_TOPK = 16

NA_W = NA_HEADS * NA_HEAD_DIM
GLA_KW = GLA_HEADS * GLA_DK
GLA_VW = GLA_HEADS * GLA_DV
GQA_QW = GQA_HEADS * GQA_HEAD_DIM
GQA_KVW = GQA_KV_HEADS * GQA_HEAD_DIM
DEEPNORM_ALPHA = (2.0 * DEPTH) ** 0.25

LANES = 128
BF16_SUBLANES = 16
VMEM_LIMIT_BYTES = 58 * 1024 * 1024

NA_QROWS = 4
NA_BAND = NA_QROWS + NA_WIN_H
NA_QBLK = NA_QROWS * GRID_W
NA_KBLK = NA_BAND * GRID_W
NA_STEP_HEADS = 8
GLA_BLK = 256
GQA_KEY_CHUNK = 256
GQA_LOOP_CHUNKS = 32
GQA_ONES_ROWS = 16
PEER_EXPERT_BLK = 1024
PEER_SUB = 512
NEG = -1e30


def _cparams(sem):
    return pltpu.CompilerParams(dimension_semantics=sem, vmem_limit_bytes=VMEM_LIMIT_BYTES)


def _pick(n, cands):
    for c in cands:
        if n % c == 0:
            return c
    raise ValueError(f"no block size in {cands} divides {n}")


def _dot(a, b):
    return jnp.dot(a, b, preferred_element_type=F32)


def _dot_nt(a, b):
    return lax.dot_general(a, b, (((1,), (1,)), ((), ())), preferred_element_type=F32)


def _dot_tn(a, b):
    return lax.dot_general(a, b, (((0,), (0,)), ((), ())), preferred_element_type=F32)


def _layer_norm(x):
    mu = jnp.mean(x, axis=-1, keepdims=True)
    xc = x - mu
    var = jnp.mean(xc * xc, axis=-1, keepdims=True)
    return xc * lax.rsqrt(var + EPS)


def _row_block(tt, n_ctx):
    tm = _pick(tt, (256, 128))
    assert n_ctx % tm == 0
    return tm


def _mod_spec(n_ctx, tm, d):
    return pl.BlockSpec((None, 6, d), lambda i: (jnp.where(i < n_ctx // tm, 0, 1), 0, 0))


def _mod_row(mod_ref, idx):
    return mod_ref[idx:idx + 1, :]


def _mod_kernel(c_ref, w_ref, b_ref, o_ref, s_ref):
    @pl.when(jnp.logical_and(pl.program_id(0) == 0, pl.program_id(1) == 0))
    def _():
        cb = c_ref[...]
        s_ref[...] = cb / (1.0 + jnp.exp(-cb))

    w = w_ref[0]
    reps = w.shape[1] // LANES
    for m in range(2):
        o_ref[0, m:m + 1, :] = jnp.sum(w * jnp.tile(s_ref[m], (1, reps)), axis=0, keepdims=True) + b_ref[0]


def _modulation(c2, w_mod, b_mod):
    depth, d, n = w_mod.shape
    tn = _pick(n, (512, 256, 128))
    cb = jnp.broadcast_to(c2[:, :, None], (2, d, LANES))
    return pl.pallas_call(
        _mod_kernel,
        grid=(depth, n // tn),
        in_specs=[pl.BlockSpec((2, d, LANES), lambda l, j: (0, 0, 0)),
                  pl.BlockSpec((1, d, tn), lambda l, j: (l, 0, j)),
                  pl.BlockSpec((1, 1, tn), lambda l, j: (l, 0, j))],
        out_specs=pl.BlockSpec((1, 2, tn), lambda l, j: (l, 0, j)),
        out_shape=jax.ShapeDtypeStruct((depth, 2, n), F32),
        scratch_shapes=[pltpu.VMEM((2, d, LANES), F32)],
        compiler_params=_cparams(("arbitrary", "arbitrary")),
        name="modulation",
    )(cb, w_mod, b_mod.reshape(depth, 1, n))


def _proj_kernel(x_ref, mod_ref, *refs):
    xn = _layer_norm(x_ref[...])
    h = (xn * (1.0 + _mod_row(mod_ref, 1)) + _mod_row(mod_ref, 0)).astype(BF16)
    n_groups = len(refs) // 2
    for w_ref, o_ref in zip(refs[:n_groups], refs[n_groups:]):
        o_ref[...] = _dot(h, w_ref[...]).astype(o_ref.dtype)


def _layer_block(stacked, layer, **kw):
    zeros = (0,) * (stacked.ndim - 1)
    return pl.BlockSpec((None,) + stacked.shape[1:], lambda *_: (layer,) + zeros, **kw)


def _project(xa, mod, weights, out_dtypes, n_ctx, layer):
    tt, d = xa.shape
    tm = _row_block(tt, n_ctx)
    w_specs = [_layer_block(w, layer, pipeline_mode=pl.Buffered(1)) for w in weights]
    return pl.pallas_call(
        _proj_kernel,
        grid=(tt // tm,),
        in_specs=[pl.BlockSpec((tm, d), lambda i: (i, 0)), _mod_spec(n_ctx, tm, d)] + w_specs,
        out_specs=[pl.BlockSpec((tm, w.shape[2]), lambda i: (i, 0)) for w in weights],
        out_shape=[jax.ShapeDtypeStruct((tt, w.shape[2]), dt) for w, dt in zip(weights, out_dtypes)],
        compiler_params=_cparams(("arbitrary",)),
        name="ln_mod_project",
    )(xa, mod, *weights)


def _na_kernel(q_ref, k_ref, v_ref, bias_ref, o_ref, *, n_ctx, n_band_starts):
    qb = pl.program_id(1)
    scale = NA_HEAD_DIM ** -0.5
    lane = lax.broadcasted_iota(jnp.int32, (1, LANES), 1)
    heads_per_tile = LANES // NA_HEAD_DIM

    def attend(pair, band_start):
        cols = slice(pair * LANES, (pair + 1) * LANES)
        q = q_ref[:, cols]
        kc, vc = k_ref[0:n_ctx, cols], v_ref[0:n_ctx, cols]
        if band_start is not None:
            kb, vb = k_ref[pl.ds(band_start, NA_KBLK), cols], v_ref[pl.ds(band_start, NA_KBLK), cols]
        outs = []
        for h in range(heads_per_tile):
            qh = jnp.where((lane // NA_HEAD_DIM) == h, q, jnp.zeros_like(q))
            sc = _dot_nt(qh, kc) * scale
            m = jnp.max(sc, axis=-1, keepdims=True)
            if band_start is not None:
                sw = _dot_nt(qh, kb) * scale + bias_ref[0, pair * heads_per_tile + h]
                m = jnp.maximum(m, jnp.max(sw, axis=-1, keepdims=True))
            pc = jnp.exp(sc - m)
            l = jnp.sum(pc, axis=-1, keepdims=True)
            o = _dot(pc.astype(BF16), vc)
            if band_start is not None:
                pw = jnp.exp(sw - m)
                l = l + jnp.sum(pw, axis=-1, keepdims=True)
                o = o + _dot(pw.astype(BF16), vb)
            outs.append(o / l)
        o_ref[:, cols] = jnp.where((lane // NA_HEAD_DIM) == 0, outs[0], outs[1]).astype(o_ref.dtype)

    @pl.when(qb == 0)
    def _():
        for pair in range(NA_STEP_HEADS // heads_per_tile):
            attend(pair, None)

    @pl.when(qb > 0)
    def _():
        start_blk = jnp.clip(qb - 2, 0, n_band_starts - 1)
        start = pl.multiple_of(n_ctx + start_blk * NA_QBLK, NA_QBLK)
        for pair in range(NA_STEP_HEADS // heads_per_tile):
            attend(pair, start)


def _na_bias(rpb):
    a = np.arange(NA_QROWS)[:, None]
    j = np.arange(NA_BAND)[None, :]
    ws = [np.zeros_like(a), a, np.full_like(a, NA_BAND - NA_WIN_H)]
    off = [0, -NA_WIN_H // 2, -(NA_QROWS + NA_WIN_H // 2)]
    qc = np.arange(GRID_W)[:, None]
    kc = np.arange(GRID_W)[None, :]
    wcs = np.clip(qc - NA_WIN_W // 2, 0, GRID_W - NA_WIN_W)
    col_ok = (kc >= wcs) & (kc < wcs + NA_WIN_W)
    depth, heads = rpb.shape[:2]
    n_dr, n_dc = 2 * NA_WIN_H - 1, 2 * NA_WIN_W - 1
    period = 2 * GRID_W + 1
    t = np.arange(period)
    dc = np.where(t <= GRID_W, t, t - period) + NA_WIN_W - 1
    u = jnp.where((dc >= 0) & (dc < n_dc), rpb[..., np.clip(dc, 0, n_dc - 1)], NEG)
    tiles = jnp.tile(u, (1, 1, 1, GRID_W))[..., :GRID_W * 2 * GRID_W]
    tiles = tiles.reshape(depth, heads, n_dr, GRID_W, 2 * GRID_W)[..., :GRID_W]
    tiles = jnp.where(col_ok, tiles, NEG)
    masked = jnp.full((depth, heads, GRID_W, GRID_W), NEG, F32)
    patterns = []
    for p in range(3):
        row_ok = (j >= ws[p]) & (j < ws[p] + NA_WIN_H)
        dr = j - a + off[p] + NA_WIN_H - 1
        blocks = [jnp.concatenate([tiles[:, :, dr[ai, ji]] if row_ok[ai, ji] else masked
                                   for ji in range(NA_BAND)], axis=-1) for ai in range(NA_QROWS)]
        patterns.append(jnp.stack(blocks, axis=2))
    return jnp.stack(patterns, axis=1).reshape(depth, 3, heads, NA_QBLK, NA_KBLK)


def _neighbourhood_attention(p_na, bias, n_ctx, layer):
    tt = p_na.shape[0]
    n_qb = tt // NA_QBLK
    n_lat_blk = n_qb - 1
    n_band_starts = n_lat_blk - NA_BAND // NA_QROWS + 1
    hp = NA_HEADS // NA_STEP_HEADS
    w = NA_STEP_HEADS * NA_HEAD_DIM

    def bias_idx(h, qb):
        pat = jnp.where(qb <= 1, 0, jnp.where(qb == n_qb - 1, 2, 1))
        return (layer, pat, h, 0, 0)

    return pl.pallas_call(
        functools.partial(_na_kernel, n_ctx=n_ctx, n_band_starts=n_band_starts),
        grid=(hp, n_qb),
        in_specs=[pl.BlockSpec((NA_QBLK, w), lambda h, qb: (qb, h)),
                  pl.BlockSpec((tt, w), lambda h, qb: (0, hp + h)),
                  pl.BlockSpec((tt, w), lambda h, qb: (0, 2 * hp + h)),
                  pl.BlockSpec((None, 1, NA_STEP_HEADS, NA_QBLK, NA_KBLK), bias_idx)],
        out_specs=pl.BlockSpec((NA_QBLK, w), lambda h, qb: (qb, h)),
        out_shape=jax.ShapeDtypeStruct((tt, NA_W), BF16),
        compiler_params=_cparams(("arbitrary", "arbitrary")),
        name="neighbourhood_attention",
    )(p_na, p_na, p_na, bias)


def _log_sigmoid(x):
    return jnp.minimum(x, 0.0) - jnp.log(1.0 + jnp.exp(-jnp.abs(x)))


def _split3(x):
    hi = x.astype(BF16)
    r = x - hi.astype(F32)
    mid = r.astype(BF16)
    lo = (r - mid.astype(F32)).astype(BF16)
    return hi, mid, lo


def _gla_prep(q, k, lr, wg, bg, tri, last_row):
    lh, ll, _ = _split3(lr)
    x = _dot(jnp.concatenate([lh, ll, lh], axis=1), wg) + bg
    la = _log_sigmoid(x) * (1.0 / GLA_GATE_NORM)
    b = _dot(tri, jnp.concatenate(_split3(la), axis=0))
    b_last = b[last_row:last_row + 1, :]
    q_t = (q * (GLA_DK ** -0.5) * jnp.exp(b)).astype(BF16)
    k_t = (k * jnp.exp(-b)).astype(BF16)
    k_end = (k * jnp.exp(b_last - b)).astype(BF16)
    return q_t, k_t, k_end, jnp.exp(b_last)


def _gla_attend(prep, v, causal, s_ref):
    q_t, k_t, k_end, dec = prep
    outs = []
    for h in range(GLA_HEADS):
        ks = slice(h * GLA_DK, (h + 1) * GLA_DK)
        vh = v[:, h * GLA_DV:(h + 1) * GLA_DV].astype(BF16)
        att = jnp.where(causal, _dot_nt(q_t[:, ks], k_t[:, ks]), 0.0)
        st = s_ref[h]
        o = _dot(att.astype(BF16), vh) + _dot_nt(q_t[:, ks], st.astype(BF16))
        s_ref[h] = st * dec[:, ks] + _dot_tn(vh, k_end[:, ks])
        outs.append(o)
    return jnp.concatenate(outs, axis=-1)


def _gla_kernel(qf_ref, kf_ref, vf_ref, lf_ref, qb_ref, kb_ref, vb_ref, lb_ref, wg_ref, bg_ref,
                of_ref, ob_ref, sf_ref, sb_ref):
    @pl.when(pl.program_id(0) == 0)
    def _():
        sf_ref[...] = jnp.zeros_like(sf_ref)
        sb_ref[...] = jnp.zeros_like(sb_ref)

    r = lax.broadcasted_iota(jnp.int32, (GLA_CHUNK, GLA_CHUNK), 0)
    c = lax.broadcasted_iota(jnp.int32, (GLA_CHUNK, GLA_CHUNK), 1)
    lower = r >= c
    upper = r <= c
    tri_f = jnp.tile(lower.astype(F32), (1, 3)).astype(BF16)
    tri_b = jnp.tile(upper.astype(F32), (1, 3)).astype(BF16)
    n_chunks = GLA_BLK // GLA_CHUNK
    rows_of = lambda ci: slice(ci * GLA_CHUNK, (ci + 1) * GLA_CHUNK)

    def prep_f(ci):
        rows = rows_of(ci)
        return _gla_prep(qf_ref[rows, :], kf_ref[rows, :], lf_ref[rows, :], wg_ref[0], bg_ref[0], tri_f,
                         GLA_CHUNK - 1)

    def prep_b(ci):
        rows = rows_of(ci)
        return _gla_prep(qb_ref[rows, :], kb_ref[rows, :], lb_ref[rows, :], wg_ref[1], bg_ref[1], tri_b, 0)

    pf, pb = prep_f(0), prep_b(n_chunks - 1)
    for ci in range(n_chunks):
        cb = n_chunks - 1 - ci
        cur_f, cur_b = pf, pb
        if ci + 1 < n_chunks:
            pf, pb = prep_f(ci + 1), prep_b(cb - 1)
        of_ref[rows_of(ci), :] = _gla_attend(cur_f, vf_ref[rows_of(ci), :], lower, sf_ref)
        ob_ref[rows_of(cb), :] = _gla_attend(cur_b, vb_ref[rows_of(cb), :], upper, sb_ref)


def _gla(p_gla, wg, bg):
    tt = p_gla.shape[0]
    nb = tt // GLA_BLK
    fwd = lambda i: i
    bwd = lambda i: jnp.where(i == 0, 0, nb - i)
    lr_blk = (2 * GLA_KW + 2 * GLA_VW) // 128

    def specs(rowmap):
        return [pl.BlockSpec((GLA_BLK, GLA_KW), lambda i: (rowmap(i), 0)),
                pl.BlockSpec((GLA_BLK, GLA_KW), lambda i: (rowmap(i), 1)),
                pl.BlockSpec((GLA_BLK, GLA_VW), lambda i: (rowmap(i), 1)),
                pl.BlockSpec((GLA_BLK, 128), lambda i: (rowmap(i), lr_blk))]

    return pl.pallas_call(
        _gla_kernel,
        grid=(nb,),
        in_specs=specs(fwd) + specs(bwd) + [
            pl.BlockSpec((2, 3 * 128, GLA_KW), lambda i: (0, 0, 0)),
            pl.BlockSpec((2, 1, GLA_KW), lambda i: (0, 0, 0))],
        out_specs=[pl.BlockSpec((GLA_BLK, GLA_VW), lambda i: (fwd(i), 0)),
                   pl.BlockSpec((GLA_BLK, GLA_VW), lambda i: (bwd(i), 0))],
        out_shape=[jax.ShapeDtypeStruct((tt, GLA_VW), F32)] * 2,
        scratch_shapes=[pltpu.VMEM((GLA_HEADS, GLA_DV, GLA_DK), F32)] * 2,
        compiler_params=_cparams(("arbitrary",)),
        name="gla_bidirectional",
    )(*([p_gla] * 8), wg, bg)


def _group_sum_sq(x, ones_bd):
    sq = x * x
    hi = sq.astype(BF16)
    lo = (sq - hi.astype(F32)).astype(BF16)
    return _dot(hi, ones_bd) + _dot(lo, ones_bd)


def _gqa_prep_kernel(p_ref, ta_ref, tb_ref, ones_ref, qt_ref, kz_ref, vt_ref):
    d = GQA_HEAD_DIM
    x = p_ref[...]
    q, qs = x[:, 0:GQA_QW], x[:, GQA_QW:2 * GQA_QW]
    o = 2 * GQA_QW
    k, ks, v = x[:, o:o + GQA_KVW], x[:, o + GQA_KVW:o + 2 * GQA_KVW], x[:, o + 2 * GQA_KVW:o + 3 * GQA_KVW]
    ta, tb = ta_ref[...], tb_ref[...]
    taq = jnp.tile(ta[:, 0:128], (1, GQA_QW // 128))
    tbq = jnp.tile(tb[:, 0:128], (1, GQA_QW // 128))
    rq = lax.rsqrt(_group_sum_sq(q, ones_ref[...]) * (1.0 / d) + EPS)
    qr = rq * (q * taq + qs * tbq)
    rk = lax.rsqrt(_group_sum_sq(k, ones_ref[0:GQA_KVW, 0:GQA_KVW]) * (1.0 / d) + EPS)
    kr = rk * (k * ta[:, 128:256] + ks * tb[:, 128:256])
    qt_ref[...] = qr.T.astype(BF16)
    lane = lax.broadcasted_iota(jnp.int32, (1, GQA_KVW), 1)
    for g in range(GQA_KV_HEADS):
        kz_ref[g] = jnp.where((lane // d) == g, kr, 0.0).astype(BF16)
    vt_ref[...] = v.T.astype(BF16)


def _gqa_prep(p_gqa, ta, tb):
    tt, n = p_gqa.shape
    tm = _pick(tt, (384, 256, 128))
    hid = np.arange(GQA_QW) // GQA_HEAD_DIM
    ones_bd = jnp.asarray(hid[:, None] == hid[None, :], BF16)
    return pl.pallas_call(
        _gqa_prep_kernel,
        grid=(tt // tm,),
        in_specs=[pl.BlockSpec((tm, n), lambda i: (i, 0)),
                  pl.BlockSpec((tm, 256), lambda i: (i, 0)),
                  pl.BlockSpec((tm, 256), lambda i: (i, 0)),
                  pl.BlockSpec((GQA_QW, GQA_QW), lambda i: (0, 0))],
        out_specs=[pl.BlockSpec((GQA_QW, tm), lambda i: (0, i)),
                   pl.BlockSpec((GQA_KV_HEADS, tm, GQA_KVW), lambda i: (0, i, 0)),
                   pl.BlockSpec((GQA_KVW, tm), lambda i: (0, i))],
        out_shape=[jax.ShapeDtypeStruct((GQA_QW, tt), BF16),
                   jax.ShapeDtypeStruct((GQA_KV_HEADS, tt, GQA_KVW), BF16),
                   jax.ShapeDtypeStruct((GQA_KVW, tt), BF16)],
        compiler_params=_cparams(("arbitrary",)),
        name="gqa_prep",
    )(p_gqa, ta, tb, ones_bd)


def _gqa_kernel(qt_ref, kz_ref, vt_ref, o_ref, s0_ref, s1_ref, acc_ref, *, n_ctx):
    qb = pl.program_id(1)
    d = GQA_HEAD_DIM
    group = GQA_HEADS // GQA_KV_HEADS
    kc = GQA_KEY_CHUNK
    tq = qt_ref.shape[1]
    q4 = jnp.concatenate([jnp.concatenate([qt_ref[g * d:(g + 1) * d, :]] * 2, axis=0) for g in range(group)],
                         axis=1)

    def scores(c, s_ref):
        start = pl.multiple_of(c * kc, kc)
        s = _dot(kz_ref[0, pl.ds(start, kc), :], q4)
        s_ref[...] = s
        return jnp.max(s, axis=0, keepdims=True)

    def accumulate(c, s_ref, m_chunk, m):
        m_new = jnp.maximum(m, m_chunk)
        p = jnp.exp2(s_ref[...] - m_new)
        acc_ref[...] = acc_ref[...] * jnp.exp2(m - m_new) + _dot(vt_ref[0, c], p.astype(BF16))
        return m_new

    def attend(n_chunks):
        acc_ref[...] = jnp.zeros_like(acc_ref)
        m = jnp.full((1, group * tq), -jnp.inf, F32)
        mc0 = scores(0, s0_ref)

        def body(j, carry):
            m, mc0 = carry
            for k in range(0, GQA_LOOP_CHUNKS, 2):
                c = GQA_LOOP_CHUNKS * j + k
                mc1 = scores(c + 1, s1_ref)
                m = accumulate(c, s0_ref, mc0, m)
                mc0 = scores(c + 2, s0_ref)
                m = accumulate(c + 1, s1_ref, mc1, m)
            return m, mc0

        m, mc0 = lax.fori_loop(0, (n_chunks - 1) // GQA_LOOP_CHUNKS, body, (m, mc0))
        accumulate(n_chunks - 1, s0_ref, mc0, m)
        out = acc_ref[0:d, :] / acc_ref[d:d + 1, :]
        o_ref[...] = jnp.concatenate([out[:, g * tq:(g + 1) * tq] for g in range(group)],
                                     axis=0).T.astype(o_ref.dtype)

    @pl.when(qb == 0)
    def _():
        attend(n_ctx // kc)

    @pl.when(qb > 0)
    def _():
        attend(kz_ref.shape[1] // kc)


def _gqa_attention(qt, kz, vt, n_ctx):
    tt = qt.shape[1]
    tq = n_ctx
    kc = GQA_KEY_CHUNK
    group = GQA_HEADS // GQA_KV_HEADS
    gw = group * GQA_HEAD_DIM
    assert tt % kc == 0 and n_ctx % kc == 0
    assert (tt // kc) % GQA_LOOP_CHUNKS == 1 and (n_ctx // kc) % GQA_LOOP_CHUNKS == 1
    vt3 = vt.reshape(GQA_KV_HEADS, GQA_HEAD_DIM, tt // kc, kc).transpose(0, 2, 1, 3)
    extra = jnp.zeros((GQA_KV_HEADS, tt // kc, GQA_ONES_ROWS, kc), BF16).at[:, :, 0, :].set(1.0)
    vt3 = jnp.concatenate([vt3, extra], axis=2)
    vrows = GQA_HEAD_DIM + GQA_ONES_ROWS
    return pl.pallas_call(
        functools.partial(_gqa_kernel, n_ctx=n_ctx),
        grid=(GQA_KV_HEADS, tt // tq),
        in_specs=[pl.BlockSpec((gw, tq), lambda g, i: (g, i)),
                  pl.BlockSpec((1, tt, GQA_KVW), lambda g, i: (g, 0, 0)),
                  pl.BlockSpec((1, tt // kc, vrows, kc), lambda g, i: (g, 0, 0, 0))],
        out_specs=pl.BlockSpec((tq, gw), lambda g, i: (i, g)),
        out_shape=jax.ShapeDtypeStruct((tt, GQA_QW), BF16),
        scratch_shapes=[pltpu.VMEM((kc, group * tq), F32), pltpu.VMEM((kc, group * tq), F32),
                        pltpu.VMEM((vrows, group * tq), F32)],
        compiler_params=_cparams(("arbitrary", "arbitrary")),
        name="gqa_attention",
    )(qt, kz, vt3)


def _mix_out_kernel(x_ref, mod_ref, yna_ref, of_ref, ob_ref, gate_ref, ygqa_ref, w_ref, nw_ref,
                    g_ref, b_ref, x1_ref, h2t_ref):
    o = of_ref[...] + ob_ref[...]
    nw = nw_ref[...]
    parts = []
    for h in range(GLA_HEADS):
        oh = o[:, h * GLA_DV:(h + 1) * GLA_DV]
        parts.append(oh * lax.rsqrt(jnp.mean(oh * oh, axis=-1, keepdims=True) + EPS) * nw)
    gate = gate_ref[...]
    y_gla = (jnp.concatenate(parts, axis=-1) * (gate / (1.0 + jnp.exp(-gate)))).astype(BF16)
    y = (_dot(yna_ref[...], w_ref[0:NA_W, :]) + _dot(y_gla, w_ref[NA_W:NA_W + GLA_VW, :])
         + _dot(ygqa_ref[...], w_ref[NA_W + GLA_VW:, :]))
    x1 = _layer_norm(DEEPNORM_ALPHA * x_ref[...] + _mod_row(mod_ref, 2) * y) * g_ref[...] + b_ref[...]
    x1_ref[...] = x1
    h2 = _layer_norm(x1) * (1.0 + _mod_row(mod_ref, 4)) + _mod_row(mod_ref, 3)
    h2t_ref[...] = h2.T.astype(BF16)


def _mix_out(xa, mod, y_na, o_f, o_b, p_gla, y_gqa, w_out, norm_w, ln_g, ln_b, n_ctx, layer):
    tt, d = xa.shape
    tm = _row_block(tt, n_ctx)
    row = lambda i: (i, 0)
    const2 = lambda i: (0, 0)
    return pl.pallas_call(
        _mix_out_kernel,
        grid=(tt // tm,),
        in_specs=[pl.BlockSpec((tm, d), row),
                  _mod_spec(n_ctx, tm, d),
                  pl.BlockSpec((tm, NA_W), row),
                  pl.BlockSpec((tm, GLA_VW), row),
                  pl.BlockSpec((tm, GLA_VW), row),
                  pl.BlockSpec((tm, GLA_VW), lambda i: (i, 2)),
                  pl.BlockSpec((tm, GQA_QW), row),
                  _layer_block(w_out, layer),
                  pl.BlockSpec((1, GLA_DV), const2),
                  pl.BlockSpec((1, d), const2),
                  pl.BlockSpec((1, d), const2)],
        out_specs=[pl.BlockSpec((tm, d), row), pl.BlockSpec((d, tm), lambda i: (0, i))],
        out_shape=[jax.ShapeDtypeStruct((tt, d), F32), jax.ShapeDtypeStruct((d, tt), BF16)],
        compiler_params=_cparams(("arbitrary",)),
        name="mixer_out_postln",
    )(xa, mod, y_na, o_f, o_b, p_gla, y_gqa, w_out, norm_w, ln_g, ln_b)


def _top16_exact(s, key_iota):
    cur = s
    rank = jnp.full(s.shape, float(PEER_TOPK), F32)
    vals = []
    for r in range(PEER_TOPK):
        m = jnp.max(cur, axis=0, keepdims=True)
        idx = jnp.min(jnp.where(cur == m, key_iota, float(PEER_NKEYS)), axis=0, keepdims=True)
        sel = key_iota == idx
        rank = jnp.where(sel, float(r), rank)
        cur = jnp.where(sel, -jnp.inf, cur)
        vals.append(m)
    return jnp.concatenate(vals, axis=0), rank


_CODE_SCALE = 2.0 ** 100
_CODE_STEP = 64.0


def _rank_code(r):
    return -_CODE_SCALE * (1.0 + (r + 1) / _CODE_STEP)


def _top16_fast(s):
    cur = s
    vals = []
    for r in range(PEER_TOPK):
        m = jnp.max(cur, axis=0, keepdims=True)
        cur = jnp.where(cur == m, _rank_code(r), cur)
        vals.append(m)
    return jnp.concatenate(vals, axis=0), cur


def _coded(cur):
    return cur <= _rank_code(0)


def _decode_rank(cur):
    rank = (cur * (-1.0 / _CODE_SCALE) - 1.0) * _CODE_STEP - 1.0
    return jnp.where(_coded(cur), rank, float(PEER_TOPK))


def _cand_layout():
    k = PEER_TOPK
    rows = [(0, rb) for rb in range(k)]
    for ra in range(1, 5):
        rows += [(ra, rb) for rb in range(8)]
    rows += [(ra, 0) for ra in range(8, k)]
    rows += [(ra, 1) for ra in range(8)]
    rows += [(ra, 0) for ra in range(8)]
    seen, valid = set(), []
    for pair in rows:
        valid.append(pair not in seen)
        seen.add(pair)
    needed = {(ra, rb) for ra in range(k) for rb in range(k) if (ra + 1) * (rb + 1) <= k}
    assert needed <= seen
    ra = np.array([p[0] for p in rows], np.float32)
    rb = np.array([p[1] for p in rows], np.float32)
    return ra, rb, np.array(valid)


def _select_pairs(a, b, c_ra, c_flat, c_ok, exact):
    k = PEER_TOPK
    tt = a.shape[1]
    blocks = [a[0:1, :] + b]
    for ra in range(1, 5):
        blocks.append(a[ra:ra + 1, :] + b[0:8, :])
    blocks.append(a[8:k, :] + b[0:1, :])
    blocks.append(a[0:8, :] + b[1:2, :])
    blocks.append(a[0:8, :] + b[0:1, :])
    cand = jnp.where(c_ok, jnp.concatenate(blocks, axis=0), -jnp.inf)
    m0 = a[0:1, :] + b[0:1, :]
    z = jnp.zeros((1, tt), F32)
    if exact:
        rank_iota = lax.broadcasted_iota(jnp.int32, (k, tt), 0).astype(F32)
        cnt = jnp.zeros((k, tt), F32)
        for _ in range(k):
            m = jnp.max(cand, axis=0, keepdims=True)
            idx = jnp.min(jnp.where(cand == m, c_flat, 1e9), axis=0, keepdims=True)
            sel = c_flat == idx
            ra_sel = jnp.sum(jnp.where(sel, c_ra, 0.0), axis=0, keepdims=True)
            cnt = cnt + jnp.where(rank_iota == ra_sel, 1.0, 0.0)
            z = z + jnp.exp(m - m0)
            cand = jnp.where(sel, -jnp.inf, cand)
        return cnt, z, None
    for _ in range(k):
        m = jnp.max(cand, axis=0, keepdims=True)
        z = z + jnp.exp(m - m0)
        cand = jnp.where(cand == m, -jnp.inf, cand)
    sel = jnp.where(jnp.logical_and(c_ok, cand == -jnp.inf), 1.0, 0.0)
    low = sel[56:64, :] + sel[64:72, :]
    row8 = lax.broadcasted_iota(jnp.int32, (8, tt), 0)
    sums = [jnp.sum(sel[0:16, :], axis=0, keepdims=True)]
    sums += [jnp.sum(sel[16 + 8 * i:24 + 8 * i, :], axis=0, keepdims=True) for i in range(4)]
    for r, v in enumerate(sums):
        low = jnp.where(row8 == r, v, low)
    cnt = jnp.concatenate([low, sel[48:56, :]], axis=0)
    return cnt, z, jnp.sum(sel, axis=0, keepdims=True)


def _route_head(s1, s2, key_iota, c_ra, c_flat, c_ok, exact):
    k = PEER_TOPK
    if exact:
        a, r1 = _top16_exact(s1, key_iota)
        b, r2 = _top16_exact(s2, key_iota)
    else:
        a, cur1 = _top16_fast(s1)
        b, cur2 = _top16_fast(s2)
        r2 = _decode_rank(cur2)
    cnt, z, n_sel = _select_pairs(a, b, c_ra, c_flat, c_ok, exact)
    n1 = jnp.zeros(s1.shape, F32)
    for r in range(k):
        hit = (r1 == float(r)) if exact else (s1 == a[r:r + 1, :])
        n1 = jnp.where(hit, cnt[r:r + 1, :], n1)
    outs = (r2, jnp.exp(s2 - b[0:1, :]) / z, n1, jnp.exp(s1 - a[0:1, :]))
    if exact:
        return outs, None
    n_coded = lambda cur: jnp.sum(jnp.where(_coded(cur), 1.0, 0.0), axis=0, keepdims=True)
    clean = jnp.logical_and(n_sel == float(k),
                            jnp.logical_and(n_coded(cur1) == float(k), n_coded(cur2) == float(k)))
    return outs, jnp.where(clean, 0.0, 1.0)


def _route_kernel(h_ref, wq_ref, sk_ref, cst_ref, r2_ref, e2_ref, n1_ref, e1_ref):
    tt = h_ref.shape[1]
    key_iota = lax.broadcasted_iota(jnp.int32, (PEER_NKEYS, tt), 0).astype(F32)
    cst = cst_ref[...]
    n_rows = cst.shape[0]
    c_ra = jnp.broadcast_to(cst[:, 0:1], (n_rows, tt))
    c_flat = jnp.broadcast_to(cst[:, 1:2], (n_rows, tt))
    c_ok = jnp.broadcast_to(cst[:, 2:3], (n_rows, tt)) > 0.5
    half = PEER_KEY_DIM // 2

    def head(h, qt_h, exact):
        s1 = _dot(sk_ref[h, 0], qt_h[0:half, :].astype(BF16))
        s2 = _dot(sk_ref[h, 1], qt_h[half:2 * half, :].astype(BF16))
        outs, bad = _route_head(s1, s2, key_iota, c_ra, c_flat, c_ok, exact)
        for ref, val in zip((r2_ref, e2_ref, n1_ref, e1_ref), outs):
            ref[h] = val.astype(ref.dtype)
        return bad

    qt = _dot(wq_ref[...], h_ref[...])
    tied = [jnp.max(head(h, qt[h * PEER_KEY_DIM:(h + 1) * PEER_KEY_DIM, :], False)) for h in range(PEER_HEADS)]
    for h in range(PEER_HEADS):
        @pl.when(tied[h] > 0.0)
        def _(h=h):
            head(h, _dot(wq_ref[h * PEER_KEY_DIM:(h + 1) * PEER_KEY_DIM, :], h_ref[...]), True)


def _peer_route(h2t, wq_t, sub_keys, layer):
    d, tt = h2t.shape
    tb = _pick(tt, (256, 128))
    ra, rb, valid = _cand_layout()
    cst = np.zeros((ra.shape[0], LANES), np.float32)
    flat = np.where(valid, ra * PEER_TOPK + rb, 1000.0 + np.arange(ra.shape[0]))
    cst[:, 0], cst[:, 1], cst[:, 2] = ra, flat, valid
    shp = (PEER_HEADS, PEER_NKEYS, tt)
    out_spec = pl.BlockSpec((PEER_HEADS, PEER_NKEYS, tb), lambda i: (0, 0, i))
    return pl.pallas_call(
        _route_kernel,
        grid=(tt // tb,),
        in_specs=[pl.BlockSpec((d, tb), lambda i: (0, i)),
                  _layer_block(wq_t, layer),
                  _layer_block(sub_keys, layer),
                  pl.BlockSpec(cst.shape, lambda i: (0, 0))],
        out_specs=[out_spec] * 4,
        out_shape=[jax.ShapeDtypeStruct(shp, dt) for dt in (BF16, BF16, F32, F32)],
        compiler_params=_cparams(("arbitrary",)),
        name="peer_route",
    )(h2t, wq_t, sub_keys, jnp.asarray(cst))


def _gelu_tanh(x):
    c = math.sqrt(2.0 / math.pi)
    hx = 0.5 * x
    return hx * jnp.tanh(x * (c + (c * 0.044715) * (x * x))) + hx


def _peer_kernel(h_ref, u_ref, vt_ref, r2_ref, e2_ref, n1_ref, e1_ref, o_ref):
    e = pl.program_id(1)
    eb = u_ref.shape[0]
    n_i = eb // PEER_NKEYS

    @pl.when(e == 0)
    def _():
        o_ref[...] = jnp.zeros_like(o_ref)

    n_sub = eb // PEER_SUB
    sub_rows = lambda s: slice(s * PEER_SUB, (s + 1) * PEER_SUB)
    pre = _dot(u_ref[sub_rows(0), :].astype(BF16), h_ref[...])
    for s in range(n_sub):
        rows = sub_rows(s)
        cur = pre
        if s + 1 < n_sub:
            pre = _dot(u_ref[sub_rows(s + 1), :].astype(BF16), h_ref[...])
        act = _gelu_tanh(cur)
        parts = []
        for ii in range(PEER_SUB // PEER_NKEYS):
            i = e * n_i + s * (PEER_SUB // PEER_NKEYS) + ii
            tb = act.shape[1]
            tiles = PEER_NKEYS // BF16_SUBLANES
            g = None
            for h in range(PEER_HEADS):
                n1 = jnp.broadcast_to(n1_ref[h, pl.ds(i, 1), :], (BF16_SUBLANES, tb)).astype(BF16)
                e1 = jnp.broadcast_to(e1_ref[h, pl.ds(i, 1), :], (BF16_SUBLANES, tb)).astype(BF16)
                r2t = r2_ref[h].reshape(tiles, BF16_SUBLANES, tb)
                e2t = e2_ref[h].reshape(tiles, BF16_SUBLANES, tb)
                t = jnp.where(r2t < n1[None], e2t, jnp.zeros_like(e2t)) * e1[None]
                g = t if g is None else g + t
            a16 = act[ii * PEER_NKEYS:(ii + 1) * PEER_NKEYS, :].astype(BF16).reshape(tiles, BF16_SUBLANES, tb)
            parts.append((a16 * g).reshape(PEER_NKEYS, tb))
        o_ref[...] += _dot(vt_ref[:, rows], jnp.concatenate(parts, axis=0))


def _peer_experts(h2t, u_tab, vt_tab, r2, e2, n1, e1, layer):
    d, tt = h2t.shape
    n_exp = u_tab.shape[1]
    tb = _pick(tt, (768, 512, 256))
    eb = PEER_EXPERT_BLK
    route_spec = pl.BlockSpec((PEER_HEADS, PEER_NKEYS, tb), lambda i, e: (0, 0, i),
                              pipeline_mode=pl.Buffered(1))
    return pl.pallas_call(
        _peer_kernel,
        grid=(tt // tb, n_exp // eb),
        in_specs=[pl.BlockSpec((d, tb), lambda i, e: (0, i), pipeline_mode=pl.Buffered(1)),
                  pl.BlockSpec((None, eb, d), lambda i, e: (layer, e, 0)),
                  pl.BlockSpec((None, d, eb), lambda i, e: (layer, 0, e)),
                  route_spec, route_spec, route_spec, route_spec],
        out_specs=pl.BlockSpec((d, tb), lambda i, e: (0, i)),
        out_shape=jax.ShapeDtypeStruct((d, tt), F32),
        compiler_params=_cparams(("arbitrary", "arbitrary")),
        name="peer_experts",
    )(h2t, u_tab, vt_tab, r2, e2, n1, e1)


def _peer_out_kernel(x_ref, mod_ref, pt_ref, g_ref, b_ref, o_ref):
    y = pt_ref[...].T
    o_ref[...] = (_layer_norm(DEEPNORM_ALPHA * x_ref[...] + _mod_row(mod_ref, 5) * y)
                  * g_ref[...] + b_ref[...])


def _peer_out(x1, mod, peer_t, ln_g, ln_b, n_ctx, latent_only):
    tt, d = x1.shape
    tm = _row_block(tt, n_ctx)
    skip = n_ctx // tm if latent_only else 0
    return pl.pallas_call(
        _peer_out_kernel,
        grid=(tt // tm,),
        in_specs=[pl.BlockSpec((tm, d), lambda i: (i, 0)),
                  _mod_spec(n_ctx, tm, d),
                  pl.BlockSpec((d, tm), lambda i: (0, i)),
                  pl.BlockSpec((1, d), lambda i: (0, 0)),
                  pl.BlockSpec((1, d), lambda i: (0, 0))],
        out_specs=pl.BlockSpec((tm, d), lambda i: (jnp.maximum(i - skip, 0), 0)),
        out_shape=jax.ShapeDtypeStruct((tt - skip * tm, d), F32),
        compiler_params=_cparams(("arbitrary",)),
        name="peer_out_postln",
    )(x1, mod, peer_t, ln_g, ln_b)


def _rope_tables(n_ctx, n_lat):
    half = GQA_HEAD_DIM // 2
    inv = ROPE_BASE ** (-jnp.arange(0, half, 2, dtype=F32) / half)
    t = jnp.arange(n_lat)
    row = (t // GRID_W).astype(F32)
    col = (t % GRID_W).astype(F32)
    ang = jnp.concatenate([row[:, None] * inv, col[:, None] * inv], axis=-1)
    cos, sin = jnp.cos(ang), jnp.sin(ang)
    cos2 = jnp.concatenate([jnp.ones((n_ctx, 2 * half), F32), jnp.concatenate([cos, cos], -1)], 0)
    sin2 = jnp.concatenate([jnp.zeros((n_ctx, 2 * half), F32), jnp.concatenate([-sin, sin], -1)], 0)
    return cos2, sin2


_EVEN_ODD = np.concatenate([np.arange(0, GQA_HEAD_DIM, 2), np.arange(1, GQA_HEAD_DIM, 2)])
_ODD_EVEN = np.concatenate([np.arange(1, GQA_HEAD_DIM, 2), np.arange(0, GQA_HEAD_DIM, 2)])


def _head_cols(perm, n_heads):
    return np.concatenate([h * GQA_HEAD_DIM + perm for h in range(n_heads)])


def kernel(x, c, ctx, c_ctx, w_mod, b_mod, w_in, na_rpb, gla_gate_w, gla_gate_b, gla_norm_w,
           gqa_q_norm, gqa_k_norm, w_out, ln1_g, ln1_b, peer_wq, peer_subkeys, peer_u, peer_v,
           ln2_g, ln2_b):
    batch, n_lat, d = x.shape
    assert batch == 1 and d == D_MODEL
    n_ctx = ctx.shape[1]
    assert n_lat % NA_QBLK == 0 and n_lat // GRID_W >= NA_BAND + NA_QROWS
    depth = w_mod.shape[0]

    xa = jnp.concatenate([ctx[0], x[0]], axis=0)
    mods = _modulation(jnp.stack([c_ctx, c[0]]), w_mod, b_mod).reshape(depth, 2, 6, d)

    o = np.cumsum([0, NA_W, NA_W, NA_W, GLA_KW, GLA_KW, GLA_VW, GLA_VW, 2 * GLA_GATE_RANK,
                   GQA_QW, GQA_KVW, GQA_KVW])
    w_na = w_in[:, :, o[0]:o[3]].astype(BF16)
    w_gla = w_in[:, :, o[3]:o[8] + (128 - 2 * GLA_GATE_RANK)].astype(BF16)
    wq_c, wk_c, wv_c = w_in[:, :, o[8]:o[9]], w_in[:, :, o[9]:o[10]], w_in[:, :, o[10]:o[11]]
    w_gqa = jnp.concatenate([wq_c[:, :, _head_cols(_EVEN_ODD, GQA_HEADS)],
                             wq_c[:, :, _head_cols(_ODD_EVEN, GQA_HEADS)],
                             wk_c[:, :, _head_cols(_EVEN_ODD, GQA_KV_HEADS)],
                             wk_c[:, :, _head_cols(_ODD_EVEN, GQA_KV_HEADS)],
                             wv_c], -1).astype(BF16)
    wg = jnp.zeros((depth, 2, 128, GLA_KW), F32)
    wg = wg.at[:, 0, 0:GLA_GATE_RANK].set(gla_gate_w[:, 0])
    wg = wg.at[:, 1, GLA_GATE_RANK:2 * GLA_GATE_RANK].set(gla_gate_w[:, 1])
    wg_hi = wg.astype(BF16)
    wg_lo = (wg - wg_hi.astype(F32)).astype(BF16)
    wg = jnp.concatenate([wg_hi, wg_hi, wg_lo], axis=2)
    bg = gla_gate_b.reshape(depth, 2, 1, GLA_KW)
    w_out_b = w_out.astype(BF16)
    wq_t = jnp.swapaxes(peer_wq, 1, 2).astype(BF16)
    sub_keys = peer_subkeys.astype(BF16)
    u_tab = peer_u
    vt_tab = jnp.swapaxes(peer_v, 1, 2).astype(BF16)

    cos2, sin2 = _rope_tables(n_ctx, n_lat)
    q_scale = GQA_HEAD_DIM ** -0.5 * math.log2(math.e)
    na_bias = _na_bias(na_rpb)

    for l in range(depth):
        mod = mods[l]
        p_na, p_gla, p_gqa = _project(xa, mod, (w_na, w_gla, w_gqa), (BF16, F32, F32), n_ctx, l)

        y_na = _neighbourhood_attention(p_na, na_bias, n_ctx, l)
        o_f, o_b = _gla(p_gla, wg[l], bg[l])

        wqn, wkn = gqa_q_norm[l], gqa_k_norm[l]
        ta = jnp.concatenate([jnp.tile(cos2 * wqn[_EVEN_ODD] * q_scale, (1, 2)),
                              jnp.tile(cos2 * wkn[_EVEN_ODD], (1, 2))], -1)
        tb = jnp.concatenate([jnp.tile(sin2 * wqn[_ODD_EVEN] * q_scale, (1, 2)),
                              jnp.tile(sin2 * wkn[_ODD_EVEN], (1, 2))], -1)
        qt, kz, vt = _gqa_prep(p_gqa, ta, tb)
        y_gqa = _gqa_attention(qt, kz, vt, n_ctx)

        x1, h2t = _mix_out(xa, mod, y_na, o_f, o_b, p_gla, y_gqa, w_out_b,
                           gla_norm_w[l].reshape(1, GLA_DV), ln1_g[l].reshape(1, d), ln1_b[l].reshape(1, d),
                           n_ctx, l)
        r2, e2, n1, e1 = _peer_route(h2t, wq_t, sub_keys, l)
        peer_t = _peer_experts(h2t, u_tab, vt_tab, r2, e2, n1, e1, l)
        xa = _peer_out(x1, mod, peer_t, ln2_g[l].reshape(1, d), ln2_b[l].reshape(1, d), n_ctx,
                       latent_only=(l == depth - 1))

    return xa[None]
```
